```python
import jax, jax.numpy as jnp
from jax import lax
import numpy as np

D_MODEL = 2048
BATCH = 4
SEQ = 2048
DEPTH = 1

HEAD_DIM = 64
N_Q_HEADS = (D_MODEL // 2) // HEAD_DIM
N_KV_HEADS = N_Q_HEADS // 4
GROUP = N_Q_HEADS // N_KV_HEADS
ATTN_WIDTH = N_Q_HEADS * HEAD_DIM
KV_WIDTH = N_KV_HEADS * HEAD_DIM
CONV_WIDTH = D_MODEL // 2
CONV_K = 3
WINDOW = 128
BLOCK = 128
ROPE_THETA = 500000.0
ROT_DIM = HEAD_DIM // 4
D_FF = ((8 * D_MODEL // 3 + 255) // 256) * 256
RMS_EPS = 1e-6
ATTN_SCALE = HEAD_DIM ** -0.5
NEG_INF = -1e30
IN_WIDTHS = (CONV_WIDTH, CONV_WIDTH, CONV_WIDTH, ATTN_WIDTH, KV_WIDTH, KV_WIDTH, D_MODEL, D_MODEL)
IN_SPLITS = tuple(int(s) for s in np.cumsum(IN_WIDTHS)[:-1])
IN_TOTAL = int(sum(IN_WIDTHS))

kernel_name = "hybrid_macaron_conv_swa_gated"


def rms_norm(x, g):
    xf = x.astype(jnp.float32)
    y = xf * lax.rsqrt(jnp.mean(xf * xf, axis=-1, keepdims=True) + RMS_EPS)
    return (y * g.astype(jnp.float32)).astype(x.dtype)


def swiglu(h, w_gu, w_down):
    g, u = jnp.split(h @ w_gu, 2, axis=-1)
    return (jax.nn.silu(g) * u) @ w_down


def rope_tables(seq_len):
    inv_freq = 1.0 / (ROPE_THETA ** (jnp.arange(0, ROT_DIM, 2, dtype=jnp.float32) / ROT_DIM))
    ang = jnp.arange(seq_len, dtype=jnp.float32)[:, None] * inv_freq[None, :]
    return jnp.cos(ang)[None, :, None, :], jnp.sin(ang)[None, :, None, :]


def partial_rope(x, cos, sin):
    half = ROT_DIM // 2
    xf = x.astype(jnp.float32)
    x1, x2, xp = xf[..., :half], xf[..., half:ROT_DIM], xf[..., ROT_DIM:]
    out = jnp.concatenate([x1 * cos - x2 * sin, x2 * cos + x1 * sin, xp], axis=-1)
    return out.astype(x.dtype)


def causal_short_conv(u, w):
    S = u.shape[1]
    up = jnp.pad(u, ((0, 0), (CONV_K - 1, 0), (0, 0)))
    y = up[:, 0:S] * w[0]
    for j in range(1, CONV_K):
        y = y + up[:, j:j + S] * w[j]
    return y


def banded(t, nb):
    B = t.shape[0]
    tp = jnp.pad(t, ((0, 0), (BLOCK, 0), (0, 0), (0, 0)))
    tb = tp.reshape(B, nb + 1, BLOCK, t.shape[2], t.shape[3])
    return jnp.concatenate([tb[:, :-1], tb[:, 1:]], axis=2)


def sliding_window_gqa_sinks(q, k, v, sinks):
    B, S = q.shape[0], q.shape[1]
    nb = S // BLOCK
    qb = q.reshape(B, nb, BLOCK, N_KV_HEADS, GROUP, HEAD_DIM)
    kb, vb = banded(k, nb), banded(v, nb)
    s = jnp.einsum('bnqhgd,bnkhd->bnhgqk', qb, kb).astype(jnp.float32) * ATTN_SCALE
    qi = jnp.arange(BLOCK)[:, None] + BLOCK
    ki = jnp.arange(2 * BLOCK)[None, :]
    diff = qi - ki
    in_window = (diff >= 0) & (diff < WINDOW)
    key_pos = jnp.arange(nb)[:, None] * BLOCK + jnp.arange(2 * BLOCK)[None, :] - BLOCK
    valid = in_window[None] & (key_pos >= 0)[:, None, :]
    s = jnp.where(valid[None, :, None, None], s, NEG_INF)
    sink = sinks.astype(jnp.float32).reshape(1, 1, N_KV_HEADS, GROUP, 1, 1)
    m = jnp.maximum(jnp.max(s, axis=-1, keepdims=True), sink)
    p = jnp.exp(s - m)
    denom = jnp.sum(p, axis=-1, keepdims=True) + jnp.exp(sink - m)
    o = jnp.einsum('bnhgqk,bnkhd->bnqhgd', p / denom, vb.astype(jnp.float32))
    return o.reshape(B, S, ATTN_WIDTH).astype(q.dtype)


def hybrid_mixer(h, w_in, conv_w, q_norm_g, k_norm_g, sinks, w_out_conv, w_out_attn, w_o, cos, sin):
    B, S, _ = h.shape
    xc, bg, cg, q, k, v, ga, gb = jnp.split(h @ w_in, IN_SPLITS, axis=-1)
    ya = (bg * causal_short_conv(cg * xc, conv_w)) @ w_out_conv
    q = rms_norm(q.reshape(B, S, N_Q_HEADS, HEAD_DIM), q_norm_g)
    k = rms_norm(k.reshape(B, S, N_KV_HEADS, HEAD_DIM), k_norm_g)
    v = v.reshape(B, S, N_KV_HEADS, HEAD_DIM)
    q = partial_rope(q, cos, sin)
    k = partial_rope(k, cos, sin)
    yb = sliding_window_gqa_sinks(q, k, v, sinks) @ w_out_attn
    merged = jax.nn.sigmoid(ga) * ya + jax.nn.sigmoid(gb) * yb
    return merged @ w_o


def setup_inputs(seed: int = 0) -> dict:
    key = jax.random.key(seed)
    ks = jax.random.split(key, 18)
    f32 = jnp.float32

    def w(k, shape, fan_in):
        return jax.random.normal(k, shape, f32) * (fan_in ** -0.5)

    def gain(k, n):
        return 1.0 + 0.02 * jax.random.normal(k, (DEPTH, n), f32)

    return {
        "x": jax.random.normal(ks[0], (BATCH, SEQ, D_MODEL), f32),
        "g_ffn1": gain(ks[1], D_MODEL),
        "w_gu1": w(ks[2], (DEPTH, D_MODEL, 2 * D_FF), D_MODEL),
        "w_down1": w(ks[3], (DEPTH, D_FF, D_MODEL), D_FF),
        "g_mix": gain(ks[4], D_MODEL),
        "w_in": w(ks[5], (DEPTH, D_MODEL, IN_TOTAL), D_MODEL),
        "conv_w": w(ks[6], (DEPTH, CONV_K, CONV_WIDTH), CONV_K),
        "q_norm_g": gain(ks[7], HEAD_DIM),
        "k_norm_g": gain(ks[8], HEAD_DIM),
        "sinks": 0.5 * jax.random.normal(ks[9], (DEPTH, N_Q_HEADS), f32),
        "w_out_conv": w(ks[10], (DEPTH, CONV_WIDTH, D_MODEL), CONV_WIDTH),
        "w_out_attn": w(ks[11], (DEPTH, ATTN_WIDTH, D_MODEL), ATTN_WIDTH),
        "w_o": w(ks[12], (DEPTH, D_MODEL, D_MODEL), D_MODEL),
        "g_ffn2": gain(ks[13], D_MODEL),
        "w_gu2": w(ks[14], (DEPTH, D_MODEL, 2 * D_FF), D_MODEL),
        "w_down2": w(ks[15], (DEPTH, D_FF, D_MODEL), D_FF),
    }


def reference(x, g_ffn1, w_gu1, w_down1, g_mix, w_in, conv_w, q_norm_g, k_norm_g, sinks,
              w_out_conv, w_out_attn, w_o, g_ffn2, w_gu2, w_down2):
    cos, sin = rope_tables(x.shape[1])
    for l in range(DEPTH):
        x = x + 0.5 * swiglu(rms_norm(x, g_ffn1[l]), w_gu1[l], w_down1[l])
        x = x + hybrid_mixer(rms_norm(x, g_mix[l]), w_in[l], conv_w[l], q_norm_g[l], k_norm_g[l],
                             sinks[l], w_out_conv[l], w_out_attn[l], w_o[l], cos, sin)
        x = x + 0.5 * swiglu(rms_norm(x, g_ffn2[l]), w_gu2[l], w_down2[l])
    return x
```

```python
import functools

import jax
import jax.numpy as jnp
from jax import lax
from jax.experimental import pallas as pl
from jax.experimental.pallas import tpu as pltpu

F32 = jnp.float32
BF16 = jnp.bfloat16

HEAD_DIM = 64
GROUP = 4
CONV_K = 3
WINDOW = 128
ROT_DIM = HEAD_DIM // 4
ROPE_THETA = 500000.0
RMS_EPS = 1e-6
ATTN_SCALE = HEAD_DIM ** -0.5
NEG_INF = -1e30

LANES = 128
SUBLANES = 8
MXU_DIM = 256
VMEM_CAP_V7X = 60 * 1024 * 1024


def _nbytes(shape, dtype):
    n = 1
    for s in shape:
        n *= s
    return n * jnp.dtype(dtype).itemsize


def _compiler_params(n_grid_axes, pipelined_bytes, resident_bytes):
    need = 2 * pipelined_bytes + resident_bytes
    return pltpu.CompilerParams(
        dimension_semantics=("arbitrary",) * n_grid_axes,
        vmem_limit_bytes=min(VMEM_CAP_V7X, need + need // 8),
    )


def _dot(a, b):
    return jnp.dot(a, b, preferred_element_type=F32)


def _rmsnorm_kernel(x_ref, g_ref, o_ref):
    x = x_ref[...]
    ms = jnp.mean(x * x, axis=-1, keepdims=True)
    o_ref[...] = (x * lax.rsqrt(ms + RMS_EPS) * g_ref[...]).astype(o_ref.dtype)


def _rmsnorm(x, g, bm=512):
    m, d = x.shape
    return pl.pallas_call(
        _rmsnorm_kernel,
        grid=(m // bm,),
        in_specs=[pl.BlockSpec((bm, d), lambda i: (i, 0)),
                  pl.BlockSpec((1, d), lambda i: (0, 0))],
        out_specs=pl.BlockSpec((bm, d), lambda i: (i, 0)),
        out_shape=jax.ShapeDtypeStruct((m, d), BF16),
        compiler_params=_compiler_params(
            1, _nbytes((bm, d), F32) + _nbytes((bm, d), BF16), 2 * _nbytes((bm, d), F32)),
        name="rmsnorm",
    )(x, g.reshape(1, d))


def _gate_up_kernel(h_ref, wg_ref, wu_ref, o_ref, wg_bf, wu_bf):
    @pl.when(pl.program_id(1) == 0)
    def _():
        wg_bf[...] = wg_ref[...].astype(BF16)
        wu_bf[...] = wu_ref[...].astype(BF16)

    h = h_ref[...]
    g = _dot(h, wg_bf[...])
    u = _dot(h, wu_bf[...])
    o_ref[...] = (g * jax.nn.sigmoid(g) * u).astype(o_ref.dtype)


def _gate_up(h, w_gu, bm=1024, bn=512):
    m, d = h.shape
    f = w_gu.shape[1] // 2
    nj = f // bn
    return pl.pallas_call(
        _gate_up_kernel,
        grid=(nj, m // bm),
        in_specs=[pl.BlockSpec((bm, d), lambda j, i: (i, 0)),
                  pl.BlockSpec((d, bn), lambda j, i: (0, j)),
                  pl.BlockSpec((d, bn), lambda j, i: (0, j + nj))],
        out_specs=pl.BlockSpec((bm, bn), lambda j, i: (i, j)),
        out_shape=jax.ShapeDtypeStruct((m, f), BF16),
        scratch_shapes=[pltpu.VMEM((d, bn), BF16), pltpu.VMEM((d, bn), BF16)],
        compiler_params=_compiler_params(
            2,
            _nbytes((bm, d), BF16) + 2 * _nbytes((d, bn), F32) + _nbytes((bm, bn), BF16),
            2 * _nbytes((d, bn), BF16) + 3 * _nbytes((bm, bn), F32)),
        name="gate_up",
    )(h, w_gu, w_gu)


def _matmul_residual_kernel(a_ref, w_ref, r_ref, o_ref, w_bf, *, scale):
    @pl.when(pl.program_id(1) == 0)
    def _():
        w_bf[...] = w_ref[...].astype(BF16)

    o_ref[...] = r_ref[...] + scale * _dot(a_ref[...], w_bf[...])


def _matmul_residual(a, w, res, scale, bm, bn):
    m, k = a.shape
    n = w.shape[1]
    return pl.pallas_call(
        functools.partial(_matmul_residual_kernel, scale=scale),
        grid=(n // bn, m // bm),
        in_specs=[pl.BlockSpec((bm, k), lambda j, i: (i, 0)),
                  pl.BlockSpec((k, bn), lambda j, i: (0, j)),
                  pl.BlockSpec((bm, bn), lambda j, i: (i, j))],
        out_specs=pl.BlockSpec((bm, bn), lambda j, i: (i, j)),
        out_shape=jax.ShapeDtypeStruct((m, n), F32),
        scratch_shapes=[pltpu.VMEM((k, bn), BF16)],
        compiler_params=_compiler_params(
            2,
            _nbytes((bm, k), BF16) + _nbytes((k, bn), F32) + 2 * _nbytes((bm, bn), F32),
            _nbytes((k, bn), BF16) + 2 * _nbytes((bm, bn), F32)),
        name="matmul_residual",
    )(a, w, res)


def _conv_branch_kernel(h_ref, wx_ref, wb_ref, wc_ref, cw_ref, o_ref, wx_bf, wb_bf, wc_bf, u_pad):
    seq = h_ref.shape[0]

    @pl.when(pl.program_id(1) == 0)
    def _():
        wx_bf[...] = wx_ref[...].astype(BF16)
        wb_bf[...] = wb_ref[...].astype(BF16)
        wc_bf[...] = wc_ref[...].astype(BF16)
        u_pad[0:SUBLANES, :] = jnp.zeros((SUBLANES, u_pad.shape[1]), F32)

    h = h_ref[...]
    u = _dot(h, wc_bf[...]) * _dot(h, wx_bf[...])
    u_pad[SUBLANES:SUBLANES + seq, :] = u
    cw = cw_ref[...]
    y = (cw[0:1, :] * u_pad[SUBLANES - 2:SUBLANES - 2 + seq, :]
         + cw[1:2, :] * u_pad[SUBLANES - 1:SUBLANES - 1 + seq, :])
    y = y + cw[2:3, :] * u
    o_ref[...] = (_dot(h, wb_bf[...]) * y).astype(o_ref.dtype)


def _conv_branch(h, w_in, conv_w, batch, seq, width, bn=256):
    m, d = h.shape
    nj = width // bn
    return pl.pallas_call(
        _conv_branch_kernel,
        grid=(nj, batch),
        in_specs=[pl.BlockSpec((seq, d), lambda j, b: (b, 0)),
                  pl.BlockSpec((d, bn), lambda j, b: (0, j)),
                  pl.BlockSpec((d, bn), lambda j, b: (0, j + nj)),
                  pl.BlockSpec((d, bn), lambda j, b: (0, j + 2 * nj)),
                  pl.BlockSpec((CONV_K, bn), lambda j, b: (0, j))],
        out_specs=pl.BlockSpec((seq, bn), lambda j, b: (b, j)),
        out_shape=jax.ShapeDtypeStruct((m, width), BF16),
        scratch_shapes=[pltpu.VMEM((d, bn), BF16)] * 3
        + [pltpu.VMEM((SUBLANES + seq, bn), F32)],
        compiler_params=_compiler_params(
            2,
            _nbytes((seq, d), BF16) + 3 * _nbytes((d, bn), F32) + _nbytes((seq, bn), BF16),
            3 * _nbytes((d, bn), BF16) + 5 * _nbytes((seq, bn), F32)),
        name="conv_branch",
    )(h, w_in, w_in, w_in, conv_w)


def _head_norm_rope(x, gain, cos, sin_hi, sin_lo, scale):
    width = x.shape[1]
    r = lax.broadcasted_iota(jnp.int32, (MXU_DIM, MXU_DIM), 0) // HEAD_DIM
    c = lax.broadcasted_iota(jnp.int32, (MXU_DIM, MXU_DIM), 1) // HEAD_DIM
    ones_blockdiag = (r == c).astype(BF16)
    out = []
    for t in range(width // MXU_DIM):
        xt = x[:, t * MXU_DIM:(t + 1) * MXU_DIM]
        sq = xt * xt
        hi = sq.astype(BF16)
        lo = (sq - hi.astype(F32)).astype(BF16)
        ss = _dot(hi, ones_blockdiag) + _dot(lo, ones_blockdiag)
        y = xt * lax.rsqrt(ss * (1.0 / HEAD_DIM) + RMS_EPS) * gain[:, t * MXU_DIM:(t + 1) * MXU_DIM]
        for s in range(MXU_DIM // LANES):
            ys = y[:, s * LANES:(s + 1) * LANES]
            half = ROT_DIM // 2
            rot = (ys * cos
                   + pltpu.roll(ys, LANES - half, 1) * sin_hi
                   + pltpu.roll(ys, half, 1) * sin_lo)
            out.append(rot * scale)
    return jnp.concatenate(out, axis=1)


def _qkv_kernel(h_ref, w_ref, gain_ref, cos_ref, sin_hi_ref, sin_lo_ref, o_ref, w_bf, *, n_q_blocks, k_width):
    j = pl.program_id(0)

    @pl.when(pl.program_id(1) == 0)
    def _():
        w_bf[...] = w_ref[...].astype(BF16)

    acc = _dot(h_ref[...], w_bf[...])
    gain = gain_ref[...]
    tables = (cos_ref[...], sin_hi_ref[...], sin_lo_ref[...])

    @pl.when(j < n_q_blocks)
    def _():
        o_ref[...] = _head_norm_rope(acc, gain, *tables, ATTN_SCALE).astype(o_ref.dtype)

    @pl.when(j == n_q_blocks)
    def _():
        o_ref[:, :k_width] = _head_norm_rope(
            acc[:, :k_width], gain[:, :k_width], *tables, 1.0).astype(o_ref.dtype)
        o_ref[:, k_width:] = acc[:, k_width:].astype(o_ref.dtype)


def _qkv_proj(h, w_in, gains, rope, batch, seq, q_col, q_width, kv_width, bn=512):
    m, d = h.shape
    assert q_width % bn == 0 and 2 * kv_width == bn and q_col % bn == 0
    n_q_blocks = q_width // bn
    nj = n_q_blocks + 1
    first = q_col // bn
    table_spec = pl.BlockSpec((seq, LANES), lambda j, b: (0, 0))
    return pl.pallas_call(
        functools.partial(_qkv_kernel, n_q_blocks=n_q_blocks, k_width=kv_width),
        grid=(nj, batch),
        in_specs=[pl.BlockSpec((seq, d), lambda j, b: (b, 0)),
                  pl.BlockSpec((d, bn), lambda j, b: (0, first + j)),
                  pl.BlockSpec((None, 1, bn), lambda j, b: (j, 0, 0)),
                  table_spec, table_spec, table_spec],
        out_specs=pl.BlockSpec((seq, bn), lambda j, b: (b, j)),
        out_shape=jax.ShapeDtypeStruct((m, nj * bn), BF16),
        scratch_shapes=[pltpu.VMEM((d, bn), BF16)],
        compiler_params=_compiler_params(
            2,
            _nbytes((seq, d), BF16) + _nbytes((d, bn), F32) + _nbytes((seq, bn), BF16)
            + 3 * _nbytes((seq, LANES), F32),
            _nbytes((d, bn), BF16) + 4 * _nbytes((seq, bn), F32)),
        name="qkv_proj",
    )(h, w_in, gains, *rope)


def _attention_kernel(sink_ref, q_ref, kv_ref, o_ref, k_pad, v_pad, *, n_kv_heads):
    seq = q_ref.shape[0]
    kw = n_kv_heads * HEAD_DIM
    k_pad[0:WINDOW, :] = jnp.zeros((WINDOW, kw), BF16)
    v_pad[0:WINDOW, :] = jnp.zeros((WINDOW, kw), BF16)
    k_pad[WINDOW:WINDOW + seq, :] = kv_ref[:, 0:kw]
    v_pad[WINDOW:WINDOW + seq, :] = kv_ref[:, kw:2 * kw]

    qi = lax.broadcasted_iota(jnp.int32, (WINDOW, 2 * WINDOW), 0) + WINDOW
    ki = lax.broadcasted_iota(jnp.int32, (WINDOW, 2 * WINDOW), 1)
    diff = qi - ki
    in_window = (diff >= 0) & (diff < WINDOW)

    def block(n, carry):
        r0 = pl.multiple_of(n * WINDOW, WINDOW)
        kb = k_pad[pl.ds(r0, 2 * WINDOW), :]
        vb = v_pad[pl.ds(r0, 2 * WINDOW), :]
        qb = q_ref[pl.ds(r0, WINDOW), :]
        valid = in_window & (ki + (n - 1) * WINDOW >= 0)
        outs = []
        for hk in range(n_kv_heads):
            kh = kb[:, hk * HEAD_DIM:(hk + 1) * HEAD_DIM]
            vh = vb[:, hk * HEAD_DIM:(hk + 1) * HEAD_DIM]
            for g in range(GROUP):
                hq = hk * GROUP + g
                qh = qb[:, hq * HEAD_DIM:(hq + 1) * HEAD_DIM]
                s = lax.dot_general(qh, kh, (((1,), (1,)), ((), ())), preferred_element_type=F32)
                s = jnp.where(valid, s, NEG_INF)
                sink = sink_ref[hq]
                mx = jnp.maximum(jnp.max(s, axis=-1, keepdims=True), sink)
                p = jnp.exp(s - mx)
                denom = jnp.sum(p, axis=-1, keepdims=True) + jnp.exp(sink - mx)
                outs.append(_dot(p.astype(BF16), vh) / denom)
        o_ref[pl.ds(r0, WINDOW), :] = jnp.concatenate(outs, axis=1).astype(o_ref.dtype)
        return carry

    lax.fori_loop(0, seq // WINDOW, block, 0)


def _attention(qkv, sinks, batch, seq, q_width, kv_width):
    m = qkv.shape[0]
    assert q_width % (2 * kv_width) == 0
    return pl.pallas_call(
        functools.partial(_attention_kernel, n_kv_heads=kv_width // HEAD_DIM),
        grid=(batch,),
        in_specs=[pl.BlockSpec(memory_space=pltpu.SMEM),
                  pl.BlockSpec((seq, q_width), lambda b: (b, 0)),
                  pl.BlockSpec((seq, 2 * kv_width), lambda b: (b, q_width // (2 * kv_width)))],
        out_specs=pl.BlockSpec((seq, q_width), lambda b: (b, 0)),
        out_shape=jax.ShapeDtypeStruct((m, q_width), BF16),
        scratch_shapes=[pltpu.VMEM((WINDOW + seq, kv_width), BF16)] * 2,
        compiler_params=_compiler_params(
            1,
            2 * _nbytes((seq, q_width), BF16) + _nbytes((seq, 2 * kv_width), BF16),
            2 * _nbytes((WINDOW + seq, kv_width), BF16) + _nbytes((seq, q_width), F32)),
        name="swa_attention",
    )(sinks, qkv, qkv)


def _merge_kernel(h_ref, yc_ref, at_ref, wga_ref, wgb_ref, woc_ref, woa_ref, o_ref,
                  wga_bf, wgb_bf, woc_bf, woa_bf):
    @pl.when(pl.program_id(1) == 0)
    def _():
        wga_bf[...] = wga_ref[...].astype(BF16)
        wgb_bf[...] = wgb_ref[...].astype(BF16)
        woc_bf[...] = woc_ref[...].astype(BF16)
        woa_bf[...] = woa_ref[...].astype(BF16)

    h = h_ref[...]
    ya = _dot(yc_ref[...], woc_bf[...])
    yb = _dot(at_ref[...], woa_bf[...])
    ga = _dot(h, wga_bf[...])
    gb = _dot(h, wgb_bf[...])
    o_ref[...] = (jax.nn.sigmoid(ga) * ya + jax.nn.sigmoid(gb) * yb).astype(o_ref.dtype)


def _merge(h, yconv, attn, w_in, w_out_conv, w_out_attn, gate_col, bm=512, bn=512):
    m, d = h.shape
    kc, ka = yconv.shape[1], attn.shape[1]
    assert gate_col % bn == 0
    ga0 = gate_col // bn
    nj = d // bn
    return pl.pallas_call(
        _merge_kernel,
        grid=(nj, m // bm),
        in_specs=[pl.BlockSpec((bm, d), lambda j, i: (i, 0)),
                  pl.BlockSpec((bm, kc), lambda j, i: (i, 0)),
                  pl.BlockSpec((bm, ka), lambda j, i: (i, 0)),
                  pl.BlockSpec((d, bn), lambda j, i: (0, ga0 + j)),
                  pl.BlockSpec((d, bn), lambda j, i: (0, ga0 + nj + j)),
                  pl.BlockSpec((kc, bn), lambda j, i: (0, j)),
                  pl.BlockSpec((ka, bn), lambda j, i: (0, j))],
        out_specs=pl.BlockSpec((bm, bn), lambda j, i: (i, j)),
        out_shape=jax.ShapeDtypeStruct((m, d), BF16),
        scratch_shapes=[pltpu.VMEM((d, bn), BF16), pltpu.VMEM((d, bn), BF16),
                        pltpu.VMEM((kc, bn), BF16), pltpu.VMEM((ka, bn), BF16)],
        compiler_params=_compiler_params(
            2,
            _nbytes((bm, d + kc + ka), BF16) + _nbytes((2 * d + kc + ka, bn), F32)
            + _nbytes((bm, bn), BF16),
            _nbytes((2 * d + kc + ka, bn), BF16) + 5 * _nbytes((bm, bn), F32)),
        name="gated_merge",
    )(h, yconv, attn, w_in, w_in, w_out_conv, w_out_attn)


def _rope_tables(seq):
    half = ROT_DIM // 2
    inv_freq = 1.0 / (ROPE_THETA ** (jnp.arange(0, ROT_DIM, 2, dtype=F32) / ROT_DIM))
    ang = jnp.arange(seq, dtype=F32)[:, None] * inv_freq[None, :]
    cos, sin = jnp.cos(ang), jnp.sin(ang)
    ones = jnp.ones((seq, HEAD_DIM - ROT_DIM), F32)
    zeros = jnp.zeros((seq, half), F32)
    zeros_tail = jnp.zeros((seq, HEAD_DIM - ROT_DIM), F32)
    reps = LANES // HEAD_DIM
    cos_t = jnp.tile(jnp.concatenate([cos, cos, ones], axis=1), (1, reps))
    sin_hi = jnp.tile(jnp.concatenate([-sin, zeros, zeros_tail], axis=1), (1, reps))
    sin_lo = jnp.tile(jnp.concatenate([zeros, sin, zeros_tail], axis=1), (1, reps))
    return cos_t, sin_hi, sin_lo


def _swiglu_half_step(x, g, w_gu, w_down):
    h = _rmsnorm(x, g)
    act = _gate_up(h, w_gu)
    return _matmul_residual(act, w_down, x, 0.5, bm=512, bn=512)


def kernel(x, g_ffn1, w_gu1, w_down1, g_mix, w_in, conv_w, q_norm_g, k_norm_g, sinks,
           w_out_conv, w_out_attn, w_o, g_ffn2, w_gu2, w_down2):
    batch, seq, d = x.shape
    depth = w_in.shape[0]
    conv_width = conv_w.shape[2]
    q_width = w_out_attn.shape[1]
    kv_width = q_width // GROUP
    q_col = 3 * conv_width
    gate_col = q_col + q_width + 2 * kv_width
    rope = _rope_tables(seq)

    xf = x.reshape(batch * seq, d)
    for l in range(depth):
        xf = _swiglu_half_step(xf, g_ffn1[l], w_gu1[l], w_down1[l])

        h = _rmsnorm(xf, g_mix[l])
        yconv = _conv_branch(h, w_in[l], conv_w[l], batch, seq, conv_width)
        qkv_bn = 2 * kv_width
        gains = jnp.concatenate(
            [jnp.tile(q_norm_g[l], q_width // HEAD_DIM),
             jnp.tile(k_norm_g[l], kv_width // HEAD_DIM), jnp.ones((kv_width,), F32)]
        ).reshape(-1, 1, qkv_bn)
        qkv = _qkv_proj(h, w_in[l], gains, rope, batch, seq, q_col, q_width, kv_width, bn=qkv_bn)
        attn = _attention(qkv, sinks[l], batch, seq, q_width, kv_width)
        merged = _merge(h, yconv, attn, w_in[l], w_out_conv[l], w_out_attn[l], gate_col)
        xf = _matmul_residual(merged, w_o[l], xf, 1.0, bm=1024, bn=512)

        xf = _swiglu_half_step(xf, g_ffn2[l], w_gu2[l], w_down2[l])
    return xf.reshape(batch, seq, d)
```

```python
import functools

import jax
import jax.numpy as jnp
from jax import lax
from jax.experimental import pallas as pl
from jax.experimental.pallas import tpu as pltpu

F32 = jnp.float32
BF16 = jnp.bfloat16

HEAD_DIM = 64
GROUP = 4
CONV_K = 3
WINDOW = 128
ROT_DIM = HEAD_DIM // 4
ROPE_THETA = 500000.0
RMS_EPS = 1e-6
ATTN_SCALE = HEAD_DIM ** -0.5
NEG_INF = -1e30

LANES = 128
SUBLANES = 8
MXU_DIM = 256
VMEM_CAP_V7X = 60 * 1024 * 1024


def _nbytes(shape, dtype):
    n = 1
    for s in shape:
        n *= s
    return n * jnp.dtype(dtype).itemsize


def _compiler_params(n_grid_axes, pipelined_bytes, resident_bytes):
    need = 2 * pipelined_bytes + resident_bytes
    return pltpu.CompilerParams(
        dimension_semantics=("arbitrary",) * n_grid_axes,
        vmem_limit_bytes=min(VMEM_CAP_V7X, need + need // 8),
    )


def _dot(a, b):
    return jnp.dot(a, b, preferred_element_type=F32)


ROW_CHUNK = 512


def _row_chunks(n_rows, chunk=ROW_CHUNK):
    chunk = min(chunk, n_rows)
    assert n_rows % chunk == 0
    return [pl.ds(r * chunk, chunk) for r in range(n_rows // chunk)]


def _rmsnorm_kernel(x_ref, g_ref, o_ref):
    x = x_ref[...]
    ms = jnp.mean(x * x, axis=-1, keepdims=True)
    o_ref[...] = (x * lax.rsqrt(ms + RMS_EPS) * g_ref[...]).astype(o_ref.dtype)


def _rmsnorm(x, g, bm=1024):
    m, d = x.shape
    return pl.pallas_call(
        _rmsnorm_kernel,
        grid=(m // bm,),
        in_specs=[pl.BlockSpec((bm, d), lambda i: (i, 0)),
                  pl.BlockSpec((1, d), lambda i: (0, 0))],
        out_specs=pl.BlockSpec((bm, d), lambda i: (i, 0)),
        out_shape=jax.ShapeDtypeStruct((m, d), BF16),
        compiler_params=_compiler_params(
            1, _nbytes((bm, d), F32) + _nbytes((bm, d), BF16), 2 * _nbytes((bm, d), F32)),
        name="rmsnorm",
    )(x, g.reshape(1, d))


def _gate_up_kernel(h_ref, wg_ref, wu_ref, o_ref, wg_bf, wu_bf):
    @pl.when(pl.program_id(1) == 0)
    def _():
        wg_bf[...] = wg_ref[...].astype(BF16)
        wu_bf[...] = wu_ref[...].astype(BF16)

    for rows in _row_chunks(h_ref.shape[0]):
        h = h_ref[rows, :]
        g = _dot(h, wg_bf[...])
        u = _dot(h, wu_bf[...])
        o_ref[rows, :] = (g * jax.nn.sigmoid(g) * u).astype(o_ref.dtype)


def _gate_up(h, w_gu, bm=2048, bn=512):
    m, d = h.shape
    f = w_gu.shape[1] // 2
    nj = f // bn
    return pl.pallas_call(
        _gate_up_kernel,
        grid=(nj, m // bm),
        in_specs=[pl.BlockSpec((bm, d), lambda j, i: (i, 0)),
                  pl.BlockSpec((d, bn), lambda j, i: (0, j)),
                  pl.BlockSpec((d, bn), lambda j, i: (0, j + nj))],
        out_specs=pl.BlockSpec((bm, bn), lambda j, i: (i, j)),
        out_shape=jax.ShapeDtypeStruct((m, f), BF16),
        scratch_shapes=[pltpu.VMEM((d, bn), BF16), pltpu.VMEM((d, bn), BF16)],
        compiler_params=_compiler_params(
            2,
            _nbytes((bm, d), BF16) + 2 * _nbytes((d, bn), F32) + _nbytes((bm, bn), BF16),
            2 * _nbytes((d, bn), BF16) + 6 * _nbytes((ROW_CHUNK, bn), F32)),
        name="gate_up",
    )(h, w_gu, w_gu)


def _matmul_residual_kernel(a_ref, w_ref, r_ref, o_ref, w_bf, *, scale, row_chunk):
    @pl.when(pl.program_id(1) == 0)
    def _():
        w_bf[...] = w_ref[...].astype(BF16)

    for rows in _row_chunks(a_ref.shape[0], row_chunk):
        o_ref[rows, :] = r_ref[rows, :] + scale * _dot(a_ref[rows, :], w_bf[...])


def _matmul_residual(a, w, res, scale, bm, bn, row_chunk=ROW_CHUNK):
    m, k = a.shape
    n = w.shape[1]
    return pl.pallas_call(
        functools.partial(_matmul_residual_kernel, scale=scale, row_chunk=row_chunk),
        grid=(n // bn, m // bm),
        in_specs=[pl.BlockSpec((bm, k), lambda j, i: (i, 0)),
                  pl.BlockSpec((k, bn), lambda j, i: (0, j)),
                  pl.BlockSpec((bm, bn), lambda j, i: (i, j))],
        out_specs=pl.BlockSpec((bm, bn), lambda j, i: (i, j)),
        out_shape=jax.ShapeDtypeStruct((m, n), F32),
        scratch_shapes=[pltpu.VMEM((k, bn), BF16)],
        compiler_params=_compiler_params(
            2,
            _nbytes((bm, k), BF16) + _nbytes((k, bn), F32) + 2 * _nbytes((bm, bn), F32),
            _nbytes((k, bn), BF16) + 4 * _nbytes((row_chunk, bn), F32)),
        name="matmul_residual",
    )(a, w, res)


def _conv_branch_kernel(h_ref, wx_ref, wb_ref, wc_ref, cw_ref, o_ref, wx_bf, wb_bf, wc_bf, u_pad):
    seq = h_ref.shape[0]

    @pl.when(pl.program_id(1) == 0)
    def _():
        wx_bf[...] = wx_ref[...].astype(BF16)
        wb_bf[...] = wb_ref[...].astype(BF16)
        wc_bf[...] = wc_ref[...].astype(BF16)
        u_pad[0:SUBLANES, :] = jnp.zeros((SUBLANES, u_pad.shape[1]), F32)

    cw = cw_ref[...]
    rc = min(ROW_CHUNK, seq)
    for r0 in range(0, seq, rc):
        h = h_ref[r0:r0 + rc, :]
        u = _dot(h, wc_bf[...]) * _dot(h, wx_bf[...])
        p0 = SUBLANES + r0
        u_pad[p0:p0 + rc, :] = u
        y = (cw[0:1, :] * u_pad[p0 - 2:p0 - 2 + rc, :]
             + cw[1:2, :] * u_pad[p0 - 1:p0 - 1 + rc, :])
        y = y + cw[2:3, :] * u
        o_ref[r0:r0 + rc, :] = (_dot(h, wb_bf[...]) * y).astype(o_ref.dtype)


def _conv_branch(h, w_in, conv_w, batch, seq, width, bn=256):
    m, d = h.shape
    nj = width // bn
    return pl.pallas_call(
        _conv_branch_kernel,
        grid=(nj, batch),
        in_specs=[pl.BlockSpec((seq, d), lambda j, b: (b, 0)),
                  pl.BlockSpec((d, bn), lambda j, b: (0, j)),
                  pl.BlockSpec((d, bn), lambda j, b: (0, j + nj)),
                  pl.BlockSpec((d, bn), lambda j, b: (0, j + 2 * nj)),
                  pl.BlockSpec((CONV_K, bn), lambda j, b: (0, j))],
        out_specs=pl.BlockSpec((seq, bn), lambda j, b: (b, j)),
        out_shape=jax.ShapeDtypeStruct((m, width), BF16),
        scratch_shapes=[pltpu.VMEM((d, bn), BF16)] * 3
        + [pltpu.VMEM((SUBLANES + seq, bn), F32)],
        compiler_params=_compiler_params(
            2,
            _nbytes((seq, d), BF16) + 3 * _nbytes((d, bn), F32) + _nbytes((seq, bn), BF16),
            3 * _nbytes((d, bn), BF16) + _nbytes((seq, bn), F32) + 8 * _nbytes((ROW_CHUNK, bn), F32)),
        name="conv_branch",
    )(h, w_in, w_in, w_in, conv_w)


def _head_norm_rope(x, gain, cos, sin_hi, sin_lo):
    width = x.shape[1]
    r = lax.broadcasted_iota(jnp.int32, (MXU_DIM, MXU_DIM), 0) // HEAD_DIM
    c = lax.broadcasted_iota(jnp.int32, (MXU_DIM, MXU_DIM), 1) // HEAD_DIM
    ones_blockdiag = (r == c).astype(BF16)
    out = []
    for t in range(width // MXU_DIM):
        xt = x[:, t * MXU_DIM:(t + 1) * MXU_DIM]
        sq = xt * xt
        hi = sq.astype(BF16)
        lo = (sq - hi.astype(F32)).astype(BF16)
        ss = _dot(hi, ones_blockdiag) + _dot(lo, ones_blockdiag)
        y = xt * lax.rsqrt(ss * (1.0 / HEAD_DIM) + RMS_EPS) * gain[:, t * MXU_DIM:(t + 1) * MXU_DIM]
        for s in range(MXU_DIM // LANES):
            ys = y[:, s * LANES:(s + 1) * LANES]
            half = ROT_DIM // 2
            rot = (ys * cos
                   + pltpu.roll(ys, LANES - half, 1) * sin_hi
                   + pltpu.roll(ys, half, 1) * sin_lo)
            out.append(rot)
    return jnp.concatenate(out, axis=1)


def _qkv_kernel(h_ref, w_ref, gain_ref, raw_ref, cos_ref, sin_hi_ref, sin_lo_ref, o_ref, w_bf):
    @pl.when(pl.program_id(1) == 0)
    def _():
        w_bf[...] = w_ref[...].astype(BF16)

    gain = gain_ref[...]
    is_raw = raw_ref[...] != 0.0
    seq = h_ref.shape[0]
    rc = min(ROW_CHUNK, seq)
    for r0 in range(0, seq, rc):
        acc = _dot(h_ref[r0:r0 + rc, :], w_bf[...])
        y = _head_norm_rope(acc, gain, cos_ref[r0:r0 + rc, :], sin_hi_ref[r0:r0 + rc, :],
                            sin_lo_ref[r0:r0 + rc, :])
        o_ref[r0:r0 + rc, :] = jnp.where(is_raw, acc, y).astype(o_ref.dtype)


def _qkv_proj(h, w_in, gains, raw_cols, rope, batch, seq, q_col, bn):
    m, d = h.shape
    assert q_col % bn == 0
    nj = gains.shape[0]
    first = q_col // bn
    table_spec = pl.BlockSpec((seq, LANES), lambda j, b: (0, 0))
    col_spec = pl.BlockSpec((None, 1, bn), lambda j, b: (j, 0, 0))
    return pl.pallas_call(
        _qkv_kernel,
        grid=(nj, batch),
        in_specs=[pl.BlockSpec((seq, d), lambda j, b: (b, 0)),
                  pl.BlockSpec((d, bn), lambda j, b: (0, first + j)),
                  col_spec, col_spec,
                  table_spec, table_spec, table_spec],
        out_specs=pl.BlockSpec((seq, bn), lambda j, b: (b, j)),
        out_shape=jax.ShapeDtypeStruct((m, nj * bn), BF16),
        scratch_shapes=[pltpu.VMEM((d, bn), BF16)],
        compiler_params=_compiler_params(
            2,
            _nbytes((seq, d), BF16) + _nbytes((d, bn), F32) + _nbytes((seq, bn), BF16)
            + 3 * _nbytes((seq, LANES), F32),
            _nbytes((d, bn), BF16) + 10 * _nbytes((ROW_CHUNK, bn), F32)),
        name="qkv_proj",
    )(h, w_in, gains, raw_cols, *rope)


def _attention_kernel(sink_ref, q_ref, kv_ref, o_ref, k_pad, v_pad, *, n_kv_heads):
    seq = q_ref.shape[0]
    kw = n_kv_heads * HEAD_DIM
    k_pad[0:WINDOW, :] = jnp.zeros((WINDOW, kw), BF16)
    v_pad[0:WINDOW, :] = jnp.zeros((WINDOW, kw), BF16)
    k_pad[WINDOW:WINDOW + seq, :] = kv_ref[:, 0:kw]
    v_pad[WINDOW:WINDOW + seq, :] = kv_ref[:, kw:2 * kw]

    qi = lax.broadcasted_iota(jnp.int32, (WINDOW, 2 * WINDOW), 0) + WINDOW
    ki = lax.broadcasted_iota(jnp.int32, (WINDOW, 2 * WINDOW), 1)
    diff = qi - ki
    in_window = (diff >= 0) & (diff < WINDOW)

    def block(n, carry):
        r0 = pl.multiple_of(n * WINDOW, WINDOW)
        kb = k_pad[pl.ds(r0, 2 * WINDOW), :]
        vb = v_pad[pl.ds(r0, 2 * WINDOW), :]
        qb = q_ref[pl.ds(r0, WINDOW), :]
        valid = in_window & (ki + (n - 1) * WINDOW >= 0)
        outs = []
        for hk in range(n_kv_heads):
            kh = kb[:, hk * HEAD_DIM:(hk + 1) * HEAD_DIM]
            vh = vb[:, hk * HEAD_DIM:(hk + 1) * HEAD_DIM]
            for g in range(GROUP):
                hq = hk * GROUP + g
                qh = qb[:, hq * HEAD_DIM:(hq + 1) * HEAD_DIM]
                s = lax.dot_general(qh, kh, (((1,), (1,)), ((), ())), preferred_element_type=F32)
                s = jnp.where(valid, s, NEG_INF)
                sink = sink_ref[hq]
                mx = jnp.maximum(jnp.max(s, axis=-1, keepdims=True), sink)
                p = jnp.exp(s - mx)
                denom = jnp.sum(p, axis=-1, keepdims=True) + jnp.exp(sink - mx)
                outs.append(_dot(p.astype(BF16), vh) / denom)
        o_ref[pl.ds(r0, WINDOW), :] = jnp.concatenate(outs, axis=1).astype(o_ref.dtype)
        return carry

    lax.fori_loop(0, seq // WINDOW, block, 0)


def _attention(qkv, sinks, batch, seq, q_width, kv_width):
    m = qkv.shape[0]
    assert q_width % (2 * kv_width) == 0
    return pl.pallas_call(
        functools.partial(_attention_kernel, n_kv_heads=kv_width // HEAD_DIM),
        grid=(batch,),
        in_specs=[pl.BlockSpec(memory_space=pltpu.SMEM),
                  pl.BlockSpec((seq, q_width), lambda b: (b, 0)),
                  pl.BlockSpec((seq, 2 * kv_width), lambda b: (b, q_width // (2 * kv_width)))],
        out_specs=pl.BlockSpec((seq, q_width), lambda b: (b, 0)),
        out_shape=jax.ShapeDtypeStruct((m, q_width), BF16),
        scratch_shapes=[pltpu.VMEM((WINDOW + seq, kv_width), BF16)] * 2,
        compiler_params=_compiler_params(
            1,
            2 * _nbytes((seq, q_width), BF16) + _nbytes((seq, 2 * kv_width), BF16),
            2 * _nbytes((WINDOW + seq, kv_width), BF16) + _nbytes((seq, q_width), F32)),
        name="swa_attention",
    )(sinks, qkv, qkv)


MERGE_ROW_CHUNK = 256


def _merge_kernel(h_ref, yc_ref, at_ref, wga_ref, wgb_ref, woc_ref, woa_ref, o_ref,
                  wga_bf, wgb_bf, woc_bf, woa_bf):
    @pl.when(pl.program_id(1) == 0)
    def _():
        wga_bf[...] = wga_ref[...].astype(BF16)
        wgb_bf[...] = wgb_ref[...].astype(BF16)
        woc_bf[...] = woc_ref[...].astype(BF16)
        woa_bf[...] = woa_ref[...].astype(BF16)

    for rows in _row_chunks(h_ref.shape[0], MERGE_ROW_CHUNK):
        h = h_ref[rows, :]
        ya = _dot(yc_ref[rows, :], woc_bf[...])
        yb = _dot(at_ref[rows, :], woa_bf[...])
        ga = _dot(h, wga_bf[...])
        gb = _dot(h, wgb_bf[...])
        o_ref[rows, :] = (jax.nn.sigmoid(ga) * ya + jax.nn.sigmoid(gb) * yb).astype(o_ref.dtype)


def _merge(h, yconv, attn, w_in, w_out_conv, w_out_attn, gate_col, bm=512, bn=512):
    m, d = h.shape
    kc, ka = yconv.shape[1], attn.shape[1]
    assert gate_col % bn == 0
    ga0 = gate_col // bn
    nj = d // bn
    return pl.pallas_call(
        _merge_kernel,
        grid=(nj, m // bm),
        in_specs=[pl.BlockSpec((bm, d), lambda j, i: (i, 0)),
                  pl.BlockSpec((bm, kc), lambda j, i: (i, 0)),
                  pl.BlockSpec((bm, ka), lambda j, i: (i, 0)),
                  pl.BlockSpec((d, bn), lambda j, i: (0, ga0 + j)),
                  pl.BlockSpec((d, bn), lambda j, i: (0, ga0 + nj + j)),
                  pl.BlockSpec((kc, bn), lambda j, i: (0, j)),
                  pl.BlockSpec((ka, bn), lambda j, i: (0, j))],
        out_specs=pl.BlockSpec((bm, bn), lambda j, i: (i, j)),
        out_shape=jax.ShapeDtypeStruct((m, d), BF16),
        scratch_shapes=[pltpu.VMEM((d, bn), BF16), pltpu.VMEM((d, bn), BF16),
                        pltpu.VMEM((kc, bn), BF16), pltpu.VMEM((ka, bn), BF16)],
        compiler_params=_compiler_params(
            2,
            _nbytes((bm, d + kc + ka), BF16) + _nbytes((2 * d + kc + ka, bn), F32)
            + _nbytes((bm, bn), BF16),
            _nbytes((2 * d + kc + ka, bn), BF16) + 10 * _nbytes((MERGE_ROW_CHUNK, bn), F32)),
        name="gated_merge",
    )(h, yconv, attn, w_in, w_in, w_out_conv, w_out_attn)


def _rope_tables(seq):
    half = ROT_DIM // 2
    inv_freq = 1.0 / (ROPE_THETA ** (jnp.arange(0, ROT_DIM, 2, dtype=F32) / ROT_DIM))
    ang = jnp.arange(seq, dtype=F32)[:, None] * inv_freq[None, :]
    cos, sin = jnp.cos(ang), jnp.sin(ang)
    ones = jnp.ones((seq, HEAD_DIM - ROT_DIM), F32)
    zeros = jnp.zeros((seq, half), F32)
    zeros_tail = jnp.zeros((seq, HEAD_DIM - ROT_DIM), F32)
    reps = LANES // HEAD_DIM
    cos_t = jnp.tile(jnp.concatenate([cos, cos, ones], axis=1), (1, reps))
    sin_hi = jnp.tile(jnp.concatenate([-sin, zeros, zeros_tail], axis=1), (1, reps))
    sin_lo = jnp.tile(jnp.concatenate([zeros, sin, zeros_tail], axis=1), (1, reps))
    return cos_t, sin_hi, sin_lo


def _swiglu_half_step(x, g, w_gu, w_down):
    h = _rmsnorm(x, g)
    act = _gate_up(h, w_gu)
    return _matmul_residual(act, w_down, x, 0.5, bm=512, bn=512, row_chunk=256)


def kernel(x, g_ffn1, w_gu1, w_down1, g_mix, w_in, conv_w, q_norm_g, k_norm_g, sinks,
           w_out_conv, w_out_attn, w_o, g_ffn2, w_gu2, w_down2):
    batch, seq, d = x.shape
    depth = w_in.shape[0]
    conv_width = conv_w.shape[2]
    q_width = w_out_attn.shape[1]
    kv_width = q_width // GROUP
    q_col = 3 * conv_width
    gate_col = q_col + q_width + 2 * kv_width
    rope = _rope_tables(seq)

    xf = x.reshape(batch * seq, d)
    for l in range(depth):
        xf = _swiglu_half_step(xf, g_ffn1[l], w_gu1[l], w_down1[l])

        h = _rmsnorm(xf, g_mix[l])
        yconv = _conv_branch(h, w_in[l], conv_w[l], batch, seq, conv_width)
        qkv_bn = 2 * kv_width
        assert q_width % qkv_bn == 0
        gains = jnp.concatenate(
            [jnp.tile(q_norm_g[l] * ATTN_SCALE, q_width // HEAD_DIM),
             jnp.tile(k_norm_g[l], kv_width // HEAD_DIM), jnp.ones((kv_width,), F32)]
        ).reshape(-1, 1, qkv_bn)
        raw_cols = jnp.concatenate(
            [jnp.zeros((q_width + kv_width,), F32), jnp.ones((kv_width,), F32)]
        ).reshape(-1, 1, qkv_bn)
        qkv = _qkv_proj(h, w_in[l], gains, raw_cols, rope, batch, seq, q_col, bn=qkv_bn)
        attn = _attention(qkv, sinks[l], batch, seq, q_width, kv_width)
        merged = _merge(h, yconv, attn, w_in[l], w_out_conv[l], w_out_attn[l], gate_col)
        xf = _matmul_residual(merged, w_o[l], xf, 1.0, bm=2048, bn=512)

        xf = _swiglu_half_step(xf, g_ffn2[l], w_gu2[l], w_down2[l])
    return xf.reshape(batch, seq, d)
```

```python
import functools

import jax
import jax.numpy as jnp
from jax import lax
from jax.experimental import pallas as pl
from jax.experimental.pallas import tpu as pltpu

F32 = jnp.float32
BF16 = jnp.bfloat16

HEAD_DIM = 64
GROUP = 4
CONV_K = 3
WINDOW = 128
ROT_DIM = HEAD_DIM // 4
ROPE_THETA = 500000.0
RMS_EPS = 1e-6
ATTN_SCALE = HEAD_DIM ** -0.5
NEG_INF = -1e30

LANES = 128
SUBLANES = 8
MXU_DIM = 256
VMEM_CAP_V7X = 60 * 1024 * 1024


def _nbytes(shape, dtype):
    n = 1
    for s in shape:
        n *= s
    return n * jnp.dtype(dtype).itemsize


def _compiler_params(n_grid_axes, pipelined_bytes, resident_bytes):
    need = 2 * pipelined_bytes + resident_bytes
    return pltpu.CompilerParams(
        dimension_semantics=("arbitrary",) * n_grid_axes,
        vmem_limit_bytes=min(VMEM_CAP_V7X, need + need // 8),
    )


def _dot(a, b):
    return jnp.dot(a, b, preferred_element_type=F32)


ROW_CHUNK = 512


def _row_chunks(n_rows, chunk=ROW_CHUNK):
    chunk = min(chunk, n_rows)
    assert n_rows % chunk == 0
    return [pl.ds(r * chunk, chunk) for r in range(n_rows // chunk)]


def _rmsnorm_kernel(x_ref, g_ref, o_ref):
    x = x_ref[...]
    ms = jnp.mean(x * x, axis=-1, keepdims=True)
    o_ref[...] = (x * lax.rsqrt(ms + RMS_EPS) * g_ref[...]).astype(o_ref.dtype)


def _rmsnorm(x, g, bm=1024):
    m, d = x.shape
    return pl.pallas_call(
        _rmsnorm_kernel,
        grid=(m // bm,),
        in_specs=[pl.BlockSpec((bm, d), lambda i: (i, 0)),
                  pl.BlockSpec((1, d), lambda i: (0, 0))],
        out_specs=pl.BlockSpec((bm, d), lambda i: (i, 0)),
        out_shape=jax.ShapeDtypeStruct((m, d), BF16),
        compiler_params=_compiler_params(
            1, _nbytes((bm, d), F32) + _nbytes((bm, d), BF16), 2 * _nbytes((bm, d), F32)),
        name="rmsnorm",
    )(x, g.reshape(1, d))


def _side_cast_specs(w_next, n_steps, step_index):
    k, n = w_next.shape
    assert k % n_steps == 0 and (k // n_steps) % (2 * SUBLANES) == 0
    rows = k // n_steps
    spec = pl.BlockSpec((rows, n), lambda *ids: (step_index(*ids), 0))
    return spec, spec, jax.ShapeDtypeStruct((k, n), BF16), _nbytes((rows, n), F32) + _nbytes((rows, n), BF16)


def _gate_up_kernel(h_ref, wg_ref, wu_ref, wnext_ref, o_ref, wnext_bf_ref, wg_bf, wu_bf):
    @pl.when(pl.program_id(1) == 0)
    def _():
        wg_bf[...] = wg_ref[...].astype(BF16)
        wu_bf[...] = wu_ref[...].astype(BF16)

    wnext_bf_ref[...] = wnext_ref[...].astype(BF16)
    for rows in _row_chunks(h_ref.shape[0]):
        h = h_ref[rows, :]
        g = _dot(h, wg_bf[...])
        u = _dot(h, wu_bf[...])
        o_ref[rows, :] = (g * jax.nn.sigmoid(g) * u).astype(o_ref.dtype)


def _gate_up(h, w_gu, w_next, bm=2048, bn=512):
    m, d = h.shape
    f = w_gu.shape[1] // 2
    nj, ni = f // bn, m // bm
    side_in, side_out, side_shape, side_bytes = _side_cast_specs(w_next, nj * ni, lambda j, i: j * ni + i)
    return pl.pallas_call(
        _gate_up_kernel,
        grid=(nj, ni),
        in_specs=[pl.BlockSpec((bm, d), lambda j, i: (i, 0)),
                  pl.BlockSpec((d, bn), lambda j, i: (0, j)),
                  pl.BlockSpec((d, bn), lambda j, i: (0, j + nj)),
                  side_in],
        out_specs=[pl.BlockSpec((bm, bn), lambda j, i: (i, j)), side_out],
        out_shape=[jax.ShapeDtypeStruct((m, f), BF16), side_shape],
        scratch_shapes=[pltpu.VMEM((d, bn), BF16), pltpu.VMEM((d, bn), BF16)],
        compiler_params=_compiler_params(
            2,
            _nbytes((bm, d), BF16) + 2 * _nbytes((d, bn), F32) + _nbytes((bm, bn), BF16) + side_bytes,
            2 * _nbytes((d, bn), BF16) + 6 * _nbytes((ROW_CHUNK, bn), F32)),
        name="gate_up",
    )(h, w_gu, w_gu, w_next)


RESIDENT_ROW_CHUNK = 128


def _matmul_residual_kernel(*refs, scale, with_norm):
    if with_norm:
        a_ref, w_ref, r_ref, g_ref, x_ref, h_ref = refs
    else:
        a_ref, w_ref, r_ref, x_ref = refs
    for rows in _row_chunks(a_ref.shape[0], RESIDENT_ROW_CHUNK):
        x = r_ref[rows, :] + scale * _dot(a_ref[rows, :], w_ref[...])
        x_ref[rows, :] = x
        if with_norm:
            ms = jnp.mean(x * x, axis=-1, keepdims=True)
            h_ref[rows, :] = (x * lax.rsqrt(ms + RMS_EPS) * g_ref[...]).astype(h_ref.dtype)


def _matmul_residual(a, w_bf, res, scale, norm_gain=None):
    m, k = a.shape
    n = w_bf.shape[1]
    with_norm = norm_gain is not None
    resident = _nbytes((k, n), BF16) + 6 * _nbytes((RESIDENT_ROW_CHUNK, n), F32)

    def pipelined_bytes(rows):
        return (_nbytes((rows, k), BF16) + 2 * _nbytes((rows, n), F32)
                + (_nbytes((rows, n), BF16) if with_norm else 0))

    bm = next(rows for rows in (1024, 512, 256, RESIDENT_ROW_CHUNK)
              if m % rows == 0 and (2 * pipelined_bytes(rows) + resident) * 9 // 8 <= VMEM_CAP_V7X)
    row_spec = lambda width: pl.BlockSpec((bm, width), lambda i: (i, 0))
    in_specs = [row_spec(k),
                pl.BlockSpec((k, n), lambda i: (0, 0), pipeline_mode=pl.Buffered(1)),
                row_spec(n)]
    operands = [a, w_bf, res]
    out_specs = [row_spec(n)]
    out_shape = [jax.ShapeDtypeStruct((m, n), F32)]
    if with_norm:
        in_specs.append(pl.BlockSpec((1, n), lambda i: (0, 0)))
        operands.append(norm_gain.reshape(1, n))
        out_specs.append(row_spec(n))
        out_shape.append(jax.ShapeDtypeStruct((m, n), BF16))
    outs = pl.pallas_call(
        functools.partial(_matmul_residual_kernel, scale=scale, with_norm=with_norm),
        grid=(m // bm,),
        in_specs=in_specs,
        out_specs=out_specs,
        out_shape=out_shape,
        compiler_params=_compiler_params(1, pipelined_bytes(bm), resident),
        name="matmul_residual",
    )(*operands)
    return outs if with_norm else (outs[0], None)


def _conv_branch_kernel(h_ref, wx_ref, wb_ref, wc_ref, cw_ref, o_ref, wx_bf, wb_bf, wc_bf, u_pad):
    seq = h_ref.shape[0]

    @pl.when(pl.program_id(1) == 0)
    def _():
        wx_bf[...] = wx_ref[...].astype(BF16)
        wb_bf[...] = wb_ref[...].astype(BF16)
        wc_bf[...] = wc_ref[...].astype(BF16)
        u_pad[0:SUBLANES, :] = jnp.zeros((SUBLANES, u_pad.shape[1]), F32)

    cw = cw_ref[...]
    rc = min(ROW_CHUNK, seq)
    for r0 in range(0, seq, rc):
        h = h_ref[r0:r0 + rc, :]
        u = _dot(h, wc_bf[...]) * _dot(h, wx_bf[...])
        p0 = SUBLANES + r0
        u_pad[p0:p0 + rc, :] = u
        y = (cw[0:1, :] * u_pad[p0 - 2:p0 - 2 + rc, :]
             + cw[1:2, :] * u_pad[p0 - 1:p0 - 1 + rc, :])
        y = y + cw[2:3, :] * u
        o_ref[r0:r0 + rc, :] = (_dot(h, wb_bf[...]) * y).astype(o_ref.dtype)


def _conv_branch(h, w_in, conv_w, batch, seq, width, bn=256):
    m, d = h.shape
    nj = width // bn
    return pl.pallas_call(
        _conv_branch_kernel,
        grid=(nj, batch),
        in_specs=[pl.BlockSpec((seq, d), lambda j, b: (b, 0)),
                  pl.BlockSpec((d, bn), lambda j, b: (0, j)),
                  pl.BlockSpec((d, bn), lambda j, b: (0, j + nj)),
                  pl.BlockSpec((d, bn), lambda j, b: (0, j + 2 * nj)),
                  pl.BlockSpec((CONV_K, bn), lambda j, b: (0, j))],
        out_specs=pl.BlockSpec((seq, bn), lambda j, b: (b, j)),
        out_shape=jax.ShapeDtypeStruct((m, width), BF16),
        scratch_shapes=[pltpu.VMEM((d, bn), BF16)] * 3
        + [pltpu.VMEM((SUBLANES + seq, bn), F32)],
        compiler_params=_compiler_params(
            2,
            _nbytes((seq, d), BF16) + 3 * _nbytes((d, bn), F32) + _nbytes((seq, bn), BF16),
            3 * _nbytes((d, bn), BF16) + _nbytes((seq, bn), F32) + 8 * _nbytes((ROW_CHUNK, bn), F32)),
        name="conv_branch",
    )(h, w_in, w_in, w_in, conv_w)


def _head_norm_rope(x, gain, cos, sin_hi, sin_lo):
    width = x.shape[1]
    r = lax.broadcasted_iota(jnp.int32, (MXU_DIM, MXU_DIM), 0) // HEAD_DIM
    c = lax.broadcasted_iota(jnp.int32, (MXU_DIM, MXU_DIM), 1) // HEAD_DIM
    ones_blockdiag = (r == c).astype(BF16)
    out = []
    for t in range(width // MXU_DIM):
        xt = x[:, t * MXU_DIM:(t + 1) * MXU_DIM]
        sq = xt * xt
        hi = sq.astype(BF16)
        lo = (sq - hi.astype(F32)).astype(BF16)
        ss = _dot(hi, ones_blockdiag) + _dot(lo, ones_blockdiag)
        y = xt * lax.rsqrt(ss * (1.0 / HEAD_DIM) + RMS_EPS) * gain[:, t * MXU_DIM:(t + 1) * MXU_DIM]
        for s in range(MXU_DIM // LANES):
            ys = y[:, s * LANES:(s + 1) * LANES]
            half = ROT_DIM // 2
            rot = (ys * cos
                   + pltpu.roll(ys, LANES - half, 1) * sin_hi
                   + pltpu.roll(ys, half, 1) * sin_lo)
            out.append(rot)
    return jnp.concatenate(out, axis=1)


def _qkv_kernel(h_ref, w_ref, gain_ref, raw_ref, cos_ref, sin_hi_ref, sin_lo_ref, o_ref, w_bf):
    @pl.when(pl.program_id(1) == 0)
    def _():
        w_bf[...] = w_ref[...].astype(BF16)

    gain = gain_ref[...]
    is_raw = raw_ref[...] != 0.0
    seq = h_ref.shape[0]
    rc = min(ROW_CHUNK, seq)
    for r0 in range(0, seq, rc):
        acc = _dot(h_ref[r0:r0 + rc, :], w_bf[...])
        y = _head_norm_rope(acc, gain, cos_ref[r0:r0 + rc, :], sin_hi_ref[r0:r0 + rc, :],
                            sin_lo_ref[r0:r0 + rc, :])
        o_ref[r0:r0 + rc, :] = jnp.where(is_raw, acc, y).astype(o_ref.dtype)


def _qkv_proj(h, w_in, gains, raw_cols, rope, batch, seq, q_col, bn):
    m, d = h.shape
    assert q_col % bn == 0
    nj = gains.shape[0]
    first = q_col // bn
    table_spec = pl.BlockSpec((seq, LANES), lambda j, b: (0, 0))
    col_spec = pl.BlockSpec((None, 1, bn), lambda j, b: (j, 0, 0))
    return pl.pallas_call(
        _qkv_kernel,
        grid=(nj, batch),
        in_specs=[pl.BlockSpec((seq, d), lambda j, b: (b, 0)),
                  pl.BlockSpec((d, bn), lambda j, b: (0, first + j)),
                  col_spec, col_spec,
                  table_spec, table_spec, table_spec],
        out_specs=pl.BlockSpec((seq, bn), lambda j, b: (b, j)),
        out_shape=jax.ShapeDtypeStruct((m, nj * bn), BF16),
        scratch_shapes=[pltpu.VMEM((d, bn), BF16)],
        compiler_params=_compiler_params(
            2,
            _nbytes((seq, d), BF16) + _nbytes((d, bn), F32) + _nbytes((seq, bn), BF16)
            + 3 * _nbytes((seq, LANES), F32),
            _nbytes((d, bn), BF16) + 10 * _nbytes((ROW_CHUNK, bn), F32)),
        name="qkv_proj",
    )(h, w_in, gains, raw_cols, *rope)


def _attention_kernel(sink_ref, q_ref, kv_ref, o_ref, k_pad, v_pad, *, n_kv_heads):
    seq = q_ref.shape[0]
    kw = n_kv_heads * HEAD_DIM
    k_pad[0:WINDOW, :] = jnp.zeros((WINDOW, kw), BF16)
    v_pad[0:WINDOW, :] = jnp.zeros((WINDOW, kw), BF16)
    k_pad[WINDOW:WINDOW + seq, :] = kv_ref[:, 0:kw]
    v_pad[WINDOW:WINDOW + seq, :] = kv_ref[:, kw:2 * kw]

    qi = lax.broadcasted_iota(jnp.int32, (WINDOW, 2 * WINDOW), 0) + WINDOW
    ki = lax.broadcasted_iota(jnp.int32, (WINDOW, 2 * WINDOW), 1)
    diff = qi - ki
    in_window = (diff >= 0) & (diff < WINDOW)

    def block(n, carry):
        r0 = pl.multiple_of(n * WINDOW, WINDOW)
        kb = k_pad[pl.ds(r0, 2 * WINDOW), :]
        vb = v_pad[pl.ds(r0, 2 * WINDOW), :]
        qb = q_ref[pl.ds(r0, WINDOW), :]
        valid = in_window & (ki + (n - 1) * WINDOW >= 0)
        outs = []
        for hk in range(n_kv_heads):
            kh = kb[:, hk * HEAD_DIM:(hk + 1) * HEAD_DIM]
            vh = vb[:, hk * HEAD_DIM:(hk + 1) * HEAD_DIM]
            for g in range(GROUP):
                hq = hk * GROUP + g
                qh = qb[:, hq * HEAD_DIM:(hq + 1) * HEAD_DIM]
                s = lax.dot_general(qh, kh, (((1,), (1,)), ((), ())), preferred_element_type=F32)
                s = jnp.where(valid, s, NEG_INF)
                sink = sink_ref[hq]
                mx = jnp.maximum(jnp.max(s, axis=-1, keepdims=True), sink)
                p = jnp.exp(s - mx)
                denom = jnp.sum(p, axis=-1, keepdims=True) + jnp.exp(sink - mx)
                outs.append(_dot(p.astype(BF16), vh) / denom)
        o_ref[pl.ds(r0, WINDOW), :] = jnp.concatenate(outs, axis=1).astype(o_ref.dtype)
        return carry

    lax.fori_loop(0, seq // WINDOW, block, 0)


def _attention(qkv, sinks, batch, seq, q_width, kv_width):
    m = qkv.shape[0]
    assert q_width % (2 * kv_width) == 0
    return pl.pallas_call(
        functools.partial(_attention_kernel, n_kv_heads=kv_width // HEAD_DIM),
        grid=(batch,),
        in_specs=[pl.BlockSpec(memory_space=pltpu.SMEM),
                  pl.BlockSpec((seq, q_width), lambda b: (b, 0)),
                  pl.BlockSpec((seq, 2 * kv_width), lambda b: (b, q_width // (2 * kv_width)))],
        out_specs=pl.BlockSpec((seq, q_width), lambda b: (b, 0)),
        out_shape=jax.ShapeDtypeStruct((m, q_width), BF16),
        scratch_shapes=[pltpu.VMEM((WINDOW + seq, kv_width), BF16)] * 2,
        compiler_params=_compiler_params(
            1,
            2 * _nbytes((seq, q_width), BF16) + _nbytes((seq, 2 * kv_width), BF16),
            2 * _nbytes((WINDOW + seq, kv_width), BF16) + _nbytes((seq, q_width), F32)),
        name="swa_attention",
    )(sinks, qkv, qkv)


MERGE_ROW_CHUNK = 256


def _merge_kernel(h_ref, yc_ref, at_ref, wga_ref, wgb_ref, woc_ref, woa_ref, wnext_ref,
                  o_ref, wnext_bf_ref, wga_bf, wgb_bf, woc_bf, woa_bf):
    @pl.when(pl.program_id(1) == 0)
    def _():
        wga_bf[...] = wga_ref[...].astype(BF16)
        wgb_bf[...] = wgb_ref[...].astype(BF16)
        woc_bf[...] = woc_ref[...].astype(BF16)
        woa_bf[...] = woa_ref[...].astype(BF16)

    wnext_bf_ref[...] = wnext_ref[...].astype(BF16)
    for rows in _row_chunks(h_ref.shape[0], MERGE_ROW_CHUNK):
        h = h_ref[rows, :]
        ya = _dot(yc_ref[rows, :], woc_bf[...])
        yb = _dot(at_ref[rows, :], woa_bf[...])
        ga = _dot(h, wga_bf[...])
        gb = _dot(h, wgb_bf[...])
        o_ref[rows, :] = (jax.nn.sigmoid(ga) * ya + jax.nn.sigmoid(gb) * yb).astype(o_ref.dtype)


def _merge(h, yconv, attn, w_in, w_out_conv, w_out_attn, w_next, gate_col, bm=512, bn=512):
    m, d = h.shape
    kc, ka = yconv.shape[1], attn.shape[1]
    assert gate_col % bn == 0
    ga0 = gate_col // bn
    nj, ni = d // bn, m // bm
    side_in, side_out, side_shape, side_bytes = _side_cast_specs(w_next, nj * ni, lambda j, i: j * ni + i)
    return pl.pallas_call(
        _merge_kernel,
        grid=(nj, ni),
        in_specs=[pl.BlockSpec((bm, d), lambda j, i: (i, 0)),
                  pl.BlockSpec((bm, kc), lambda j, i: (i, 0)),
                  pl.BlockSpec((bm, ka), lambda j, i: (i, 0)),
                  pl.BlockSpec((d, bn), lambda j, i: (0, ga0 + j)),
                  pl.BlockSpec((d, bn), lambda j, i: (0, ga0 + nj + j)),
                  pl.BlockSpec((kc, bn), lambda j, i: (0, j)),
                  pl.BlockSpec((ka, bn), lambda j, i: (0, j)),
                  side_in],
        out_specs=[pl.BlockSpec((bm, bn), lambda j, i: (i, j)), side_out],
        out_shape=[jax.ShapeDtypeStruct((m, d), BF16), side_shape],
        scratch_shapes=[pltpu.VMEM((d, bn), BF16), pltpu.VMEM((d, bn), BF16),
                        pltpu.VMEM((kc, bn), BF16), pltpu.VMEM((ka, bn), BF16)],
        compiler_params=_compiler_params(
            2,
            _nbytes((bm, d + kc + ka), BF16) + _nbytes((2 * d + kc + ka, bn), F32)
            + _nbytes((bm, bn), BF16) + side_bytes,
            _nbytes((2 * d + kc + ka, bn), BF16) + 10 * _nbytes((MERGE_ROW_CHUNK, bn), F32)),
        name="gated_merge",
    )(h, yconv, attn, w_in, w_in, w_out_conv, w_out_attn, w_next)


def _rope_tables(seq):
    half = ROT_DIM // 2
    inv_freq = 1.0 / (ROPE_THETA ** (jnp.arange(0, ROT_DIM, 2, dtype=F32) / ROT_DIM))
    ang = jnp.arange(seq, dtype=F32)[:, None] * inv_freq[None, :]
    cos, sin = jnp.cos(ang), jnp.sin(ang)
    ones = jnp.ones((seq, HEAD_DIM - ROT_DIM), F32)
    zeros = jnp.zeros((seq, half), F32)
    zeros_tail = jnp.zeros((seq, HEAD_DIM - ROT_DIM), F32)
    reps = LANES // HEAD_DIM
    cos_t = jnp.tile(jnp.concatenate([cos, cos, ones], axis=1), (1, reps))
    sin_hi = jnp.tile(jnp.concatenate([-sin, zeros, zeros_tail], axis=1), (1, reps))
    sin_lo = jnp.tile(jnp.concatenate([zeros, sin, zeros_tail], axis=1), (1, reps))
    return cos_t, sin_hi, sin_lo


def _swiglu_half_step(x, h, w_gu, w_down, next_gain):
    act, w_down_bf = _gate_up(h, w_gu, w_down)
    return _matmul_residual(act, w_down_bf, x, 0.5, next_gain)


def kernel(x, g_ffn1, w_gu1, w_down1, g_mix, w_in, conv_w, q_norm_g, k_norm_g, sinks,
           w_out_conv, w_out_attn, w_o, g_ffn2, w_gu2, w_down2):
    batch, seq, d = x.shape
    depth = w_in.shape[0]
    conv_width = conv_w.shape[2]
    q_width = w_out_attn.shape[1]
    kv_width = q_width // GROUP
    q_col = 3 * conv_width
    gate_col = q_col + q_width + 2 * kv_width
    rope = _rope_tables(seq)

    xf = x.reshape(batch * seq, d)
    h = _rmsnorm(xf, g_ffn1[0])
    for l in range(depth):
        xf, h = _swiglu_half_step(xf, h, w_gu1[l], w_down1[l], g_mix[l])

        yconv = _conv_branch(h, w_in[l], conv_w[l], batch, seq, conv_width)
        qkv_bn = 2 * kv_width
        assert q_width % qkv_bn == 0
        gains = jnp.concatenate(
            [jnp.tile(q_norm_g[l] * ATTN_SCALE, q_width // HEAD_DIM),
             jnp.tile(k_norm_g[l], kv_width // HEAD_DIM), jnp.ones((kv_width,), F32)]
        ).reshape(-1, 1, qkv_bn)
        raw_cols = jnp.concatenate(
            [jnp.zeros((q_width + kv_width,), F32), jnp.ones((kv_width,), F32)]
        ).reshape(-1, 1, qkv_bn)
        qkv = _qkv_proj(h, w_in[l], gains, raw_cols, rope, batch, seq, q_col, bn=qkv_bn)
        attn = _attention(qkv, sinks[l], batch, seq, q_width, kv_width)
        merged, w_o_bf = _merge(h, yconv, attn, w_in[l], w_out_conv[l], w_out_attn[l], w_o[l], gate_col)
        xf, h = _matmul_residual(merged, w_o_bf, xf, 1.0, g_ffn2[l])

        next_gain = g_ffn1[l + 1] if l + 1 < depth else None
        xf, h = _swiglu_half_step(xf, h, w_gu2[l], w_down2[l], next_gain)
    return xf.reshape(batch, seq, d)
```

```python
import functools

import jax
import jax.numpy as jnp
from jax import lax
from jax.experimental import pallas as pl
from jax.experimental.pallas import tpu as pltpu

F32 = jnp.float32
BF16 = jnp.bfloat16

HEAD_DIM = 64
GROUP = 4
CONV_K = 3
WINDOW = 128
ROT_DIM = HEAD_DIM // 4
ROPE_THETA = 500000.0
RMS_EPS = 1e-6
ATTN_SCALE = HEAD_DIM ** -0.5
NEG_INF = -1e30

LANES = 128
SUBLANES = 8
MXU_DIM = 256
VMEM_CAP_V7X = 60 * 1024 * 1024


def _nbytes(shape, dtype):
    n = 1
    for s in shape:
        n *= s
    return n * jnp.dtype(dtype).itemsize


def _compiler_params(n_grid_axes, pipelined_bytes, resident_bytes):
    need = 2 * pipelined_bytes + resident_bytes
    return pltpu.CompilerParams(
        dimension_semantics=("arbitrary",) * n_grid_axes,
        vmem_limit_bytes=min(VMEM_CAP_V7X, need + need // 8),
    )


def _dot(a, b):
    return jnp.dot(a, b, preferred_element_type=F32)


ROW_CHUNK = 512


def _row_chunks(n_rows, chunk=ROW_CHUNK):
    chunk = min(chunk, n_rows)
    assert n_rows % chunk == 0
    return [pl.ds(r * chunk, chunk) for r in range(n_rows // chunk)]


def _rmsnorm_kernel(x_ref, g_ref, o_ref):
    x = x_ref[...]
    ms = jnp.mean(x * x, axis=-1, keepdims=True)
    o_ref[...] = (x * lax.rsqrt(ms + RMS_EPS) * g_ref[...]).astype(o_ref.dtype)


def _rmsnorm(x, g, bm=1024):
    m, d = x.shape
    return pl.pallas_call(
        _rmsnorm_kernel,
        grid=(m // bm,),
        in_specs=[pl.BlockSpec((bm, d), lambda i: (i, 0)),
                  pl.BlockSpec((1, d), lambda i: (0, 0))],
        out_specs=pl.BlockSpec((bm, d), lambda i: (i, 0)),
        out_shape=pltpu.HBM((m, d), BF16),
        compiler_params=_compiler_params(
            1, _nbytes((bm, d), F32) + _nbytes((bm, d), BF16), 2 * _nbytes((bm, d), F32)),
        name="rmsnorm",
    )(x, g.reshape(1, d))


def _side_cast_specs(w_next, n_steps, step_index):
    k, n = w_next.shape
    assert k % n_steps == 0 and (k // n_steps) % (2 * SUBLANES) == 0
    rows = k // n_steps
    spec = pl.BlockSpec((rows, n), lambda *ids: (step_index(*ids), 0))
    return spec, spec, pltpu.HBM((k, n), BF16), _nbytes((rows, n), F32) + _nbytes((rows, n), BF16)


def _gate_up_kernel(h_ref, wg_ref, wu_ref, wnext_ref, o_ref, wnext_bf_ref, wg_bf, wu_bf):
    @pl.when(pl.program_id(1) == 0)
    def _():
        wg_bf[...] = wg_ref[...].astype(BF16)
        wu_bf[...] = wu_ref[...].astype(BF16)

    wnext_bf_ref[...] = wnext_ref[...].astype(BF16)
    for rows in _row_chunks(h_ref.shape[0]):
        h = h_ref[rows, :]
        g = _dot(h, wg_bf[...])
        u = _dot(h, wu_bf[...])
        o_ref[rows, :] = (g * jax.nn.sigmoid(g) * u).astype(o_ref.dtype)


def _gate_up(h, w_gu, w_next, bm=2048, bn=512):
    m, d = h.shape
    f = w_gu.shape[1] // 2
    nj, ni = f // bn, m // bm
    side_in, side_out, side_shape, side_bytes = _side_cast_specs(w_next, nj * ni, lambda j, i: j * ni + i)
    return pl.pallas_call(
        _gate_up_kernel,
        grid=(nj, ni),
        in_specs=[pl.BlockSpec((bm, d), lambda j, i: (i, 0)),
                  pl.BlockSpec((d, bn), lambda j, i: (0, j)),
                  pl.BlockSpec((d, bn), lambda j, i: (0, j + nj)),
                  side_in],
        out_specs=[pl.BlockSpec((bm, bn), lambda j, i: (i, j)), side_out],
        out_shape=[pltpu.HBM((m, f), BF16), side_shape],
        scratch_shapes=[pltpu.VMEM((d, bn), BF16), pltpu.VMEM((d, bn), BF16)],
        compiler_params=_compiler_params(
            2,
            _nbytes((bm, d), BF16) + 2 * _nbytes((d, bn), F32) + _nbytes((bm, bn), BF16) + side_bytes,
            2 * _nbytes((d, bn), BF16) + 6 * _nbytes((ROW_CHUNK, bn), F32)),
        name="gate_up",
    )(h, w_gu, w_gu, w_next)


RESIDENT_ROW_CHUNK = 128


def _matmul_residual_kernel(*refs, scale, with_norm):
    if with_norm:
        a_ref, w_ref, r_ref, g_ref, x_ref, h_ref = refs
    else:
        a_ref, w_ref, r_ref, x_ref = refs
    for rows in _row_chunks(a_ref.shape[0], RESIDENT_ROW_CHUNK):
        x = r_ref[rows, :] + scale * _dot(a_ref[rows, :], w_ref[...])
        x_ref[rows, :] = x
        if with_norm:
            ms = jnp.mean(x * x, axis=-1, keepdims=True)
            h_ref[rows, :] = (x * lax.rsqrt(ms + RMS_EPS) * g_ref[...]).astype(h_ref.dtype)


def _matmul_residual(a, w_bf, res, scale, norm_gain=None):
    m, k = a.shape
    n = w_bf.shape[1]
    with_norm = norm_gain is not None
    resident = _nbytes((k, n), BF16) + 6 * _nbytes((RESIDENT_ROW_CHUNK, n), F32)

    def pipelined_bytes(rows):
        return (_nbytes((rows, k), BF16) + 2 * _nbytes((rows, n), F32)
                + (_nbytes((rows, n), BF16) if with_norm else 0))

    bm = next(rows for rows in (1024, 512, 256, RESIDENT_ROW_CHUNK)
              if m % rows == 0 and (2 * pipelined_bytes(rows) + resident) * 9 // 8 <= VMEM_CAP_V7X)
    row_spec = lambda width: pl.BlockSpec((bm, width), lambda i: (i, 0))
    in_specs = [row_spec(k),
                pl.BlockSpec((k, n), lambda i: (0, 0), pipeline_mode=pl.Buffered(1)),
                row_spec(n)]
    operands = [a, pltpu.with_memory_space_constraint(w_bf, pltpu.HBM), res]
    out_specs = [row_spec(n)]
    out_shape = [pltpu.HBM((m, n), F32)]
    if with_norm:
        in_specs.append(pl.BlockSpec((1, n), lambda i: (0, 0)))
        operands.append(norm_gain.reshape(1, n))
        out_specs.append(row_spec(n))
        out_shape.append(pltpu.HBM((m, n), BF16))
    outs = pl.pallas_call(
        functools.partial(_matmul_residual_kernel, scale=scale, with_norm=with_norm),
        grid=(m // bm,),
        in_specs=in_specs,
        out_specs=out_specs,
        out_shape=out_shape,
        compiler_params=_compiler_params(1, pipelined_bytes(bm), resident),
        name="matmul_residual",
    )(*operands)
    return outs if with_norm else (outs[0], None)


def _conv_branch_kernel(h_ref, wx_ref, wb_ref, wc_ref, cw_ref, o_ref, wx_bf, wb_bf, wc_bf, u_pad):
    seq = h_ref.shape[0]

    @pl.when(pl.program_id(1) == 0)
    def _():
        wx_bf[...] = wx_ref[...].astype(BF16)
        wb_bf[...] = wb_ref[...].astype(BF16)
        wc_bf[...] = wc_ref[...].astype(BF16)
        u_pad[0:SUBLANES, :] = jnp.zeros((SUBLANES, u_pad.shape[1]), F32)

    cw = cw_ref[...]
    rc = min(ROW_CHUNK, seq)
    for r0 in range(0, seq, rc):
        h = h_ref[r0:r0 + rc, :]
        u = _dot(h, wc_bf[...]) * _dot(h, wx_bf[...])
        p0 = SUBLANES + r0
        u_pad[p0:p0 + rc, :] = u
        y = (cw[0:1, :] * u_pad[p0 - 2:p0 - 2 + rc, :]
             + cw[1:2, :] * u_pad[p0 - 1:p0 - 1 + rc, :])
        y = y + cw[2:3, :] * u
        o_ref[r0:r0 + rc, :] = (_dot(h, wb_bf[...]) * y).astype(o_ref.dtype)


def _conv_branch(h, w_in, conv_w, batch, seq, width, bn=256):
    m, d = h.shape
    nj = width // bn
    return pl.pallas_call(
        _conv_branch_kernel,
        grid=(nj, batch),
        in_specs=[pl.BlockSpec((seq, d), lambda j, b: (b, 0)),
                  pl.BlockSpec((d, bn), lambda j, b: (0, j)),
                  pl.BlockSpec((d, bn), lambda j, b: (0, j + nj)),
                  pl.BlockSpec((d, bn), lambda j, b: (0, j + 2 * nj)),
                  pl.BlockSpec((CONV_K, bn), lambda j, b: (0, j))],
        out_specs=pl.BlockSpec((seq, bn), lambda j, b: (b, j)),
        out_shape=pltpu.HBM((m, width), BF16),
        scratch_shapes=[pltpu.VMEM((d, bn), BF16)] * 3
        + [pltpu.VMEM((SUBLANES + seq, bn), F32)],
        compiler_params=_compiler_params(
            2,
            _nbytes((seq, d), BF16) + 3 * _nbytes((d, bn), F32) + _nbytes((seq, bn), BF16),
            3 * _nbytes((d, bn), BF16) + _nbytes((seq, bn), F32) + 8 * _nbytes((ROW_CHUNK, bn), F32)),
        name="conv_branch",
    )(h, w_in, w_in, w_in, conv_w)


def _head_norm_rope(x, gain, rope):
    width = x.shape[1]
    cos, sin_hi, sin_lo = (rope[:, i * LANES:(i + 1) * LANES] for i in range(3))
    r = lax.broadcasted_iota(jnp.int32, (MXU_DIM, MXU_DIM), 0) // HEAD_DIM
    c = lax.broadcasted_iota(jnp.int32, (MXU_DIM, MXU_DIM), 1) // HEAD_DIM
    ones_blockdiag = (r == c).astype(BF16)
    out = []
    for t in range(width // MXU_DIM):
        xt = x[:, t * MXU_DIM:(t + 1) * MXU_DIM]
        ss = _dot((xt * xt).astype(BF16), ones_blockdiag)
        y = xt * lax.rsqrt(ss * (1.0 / HEAD_DIM) + RMS_EPS) * gain[:, t * MXU_DIM:(t + 1) * MXU_DIM]
        for s in range(MXU_DIM // LANES):
            ys = y[:, s * LANES:(s + 1) * LANES]
            half = ROT_DIM // 2
            rot = (ys * cos
                   + pltpu.roll(ys, LANES - half, 1) * sin_hi
                   + pltpu.roll(ys, half, 1) * sin_lo)
            out.append(rot)
    return jnp.concatenate(out, axis=1)


def _qkv_kernel(h_ref, w_ref, gain_ref, raw_ref, rope_ref, o_ref, w_bf):
    @pl.when(pl.program_id(1) == 0)
    def _():
        w_bf[...] = w_ref[...].astype(BF16)

    gain = gain_ref[...]
    is_raw = raw_ref[...] != 0.0
    seq = h_ref.shape[0]
    rc = min(ROW_CHUNK, seq)
    for r0 in range(0, seq, rc):
        acc = _dot(h_ref[r0:r0 + rc, :], w_bf[...])
        y = _head_norm_rope(acc, gain, rope_ref[r0:r0 + rc, :])
        o_ref[r0:r0 + rc, :] = jnp.where(is_raw, acc, y).astype(o_ref.dtype)


def _qkv_proj(h, w_in, gains, raw_cols, rope, batch, seq, q_col, bn):
    m, d = h.shape
    assert q_col % bn == 0
    nj = gains.shape[0]
    first = q_col // bn
    table_spec = pl.BlockSpec((seq, 3 * LANES), lambda j, b: (0, 0))
    col_spec = pl.BlockSpec((None, 1, bn), lambda j, b: (j, 0, 0))
    return pl.pallas_call(
        _qkv_kernel,
        grid=(nj, batch),
        in_specs=[pl.BlockSpec((seq, d), lambda j, b: (b, 0)),
                  pl.BlockSpec((d, bn), lambda j, b: (0, first + j)),
                  col_spec, col_spec, table_spec],
        out_specs=pl.BlockSpec((seq, bn), lambda j, b: (b, j)),
        out_shape=pltpu.HBM((m, nj * bn), BF16),
        scratch_shapes=[pltpu.VMEM((d, bn), BF16)],
        compiler_params=_compiler_params(
            2,
            _nbytes((seq, d), BF16) + _nbytes((d, bn), F32) + _nbytes((seq, bn), BF16)
            + _nbytes((seq, 3 * LANES), F32),
            _nbytes((d, bn), BF16) + 10 * _nbytes((ROW_CHUNK, bn), F32)),
        name="qkv_proj",
    )(h, w_in, gains, raw_cols, rope)


def _attention_kernel(sink_ref, q_ref, kv_ref, o_ref, k_pad, v_pad, *, n_kv_heads):
    seq = q_ref.shape[0]
    kw = n_kv_heads * HEAD_DIM
    k_pad[0:WINDOW, :] = jnp.zeros((WINDOW, kw), BF16)
    v_pad[0:WINDOW, :] = jnp.zeros((WINDOW, kw), BF16)
    k_pad[WINDOW:WINDOW + seq, :] = kv_ref[:, 0:kw]
    v_pad[WINDOW:WINDOW + seq, :] = kv_ref[:, kw:2 * kw]

    qi = lax.broadcasted_iota(jnp.int32, (WINDOW, 2 * WINDOW), 0) + WINDOW
    ki = lax.broadcasted_iota(jnp.int32, (WINDOW, 2 * WINDOW), 1)
    diff = qi - ki
    in_window = (diff >= 0) & (diff < WINDOW)

    def block(n, carry):
        r0 = pl.multiple_of(n * WINDOW, WINDOW)
        kb = k_pad[pl.ds(r0, 2 * WINDOW), :]
        vb = v_pad[pl.ds(r0, 2 * WINDOW), :]
        qb = q_ref[pl.ds(r0, WINDOW), :]
        valid = in_window & (ki + (n - 1) * WINDOW >= 0)
        outs = []
        for hk in range(n_kv_heads):
            kh = kb[:, hk * HEAD_DIM:(hk + 1) * HEAD_DIM]
            vh = vb[:, hk * HEAD_DIM:(hk + 1) * HEAD_DIM]
            for g in range(GROUP):
                hq = hk * GROUP + g
                qh = qb[:, hq * HEAD_DIM:(hq + 1) * HEAD_DIM]
                s = lax.dot_general(qh, kh, (((1,), (1,)), ((), ())), preferred_element_type=F32)
                s = jnp.where(valid, s, NEG_INF)
                sink = sink_ref[hq]
                mx = jnp.maximum(jnp.max(s, axis=-1, keepdims=True), sink)
                p = jnp.exp(s - mx)
                denom = jnp.sum(p, axis=-1, keepdims=True) + jnp.exp(sink - mx)
                outs.append(_dot(p.astype(BF16), vh) / denom)
        o_ref[pl.ds(r0, WINDOW), :] = jnp.concatenate(outs, axis=1).astype(o_ref.dtype)
        return carry

    lax.fori_loop(0, seq // WINDOW, block, 0)


def _attention(qkv, sinks, batch, seq, q_width, kv_width):
    m = qkv.shape[0]
    assert q_width % (2 * kv_width) == 0
    return pl.pallas_call(
        functools.partial(_attention_kernel, n_kv_heads=kv_width // HEAD_DIM),
        grid=(batch,),
        in_specs=[pl.BlockSpec(memory_space=pltpu.SMEM),
                  pl.BlockSpec((seq, q_width), lambda b: (b, 0)),
                  pl.BlockSpec((seq, 2 * kv_width), lambda b: (b, q_width // (2 * kv_width)))],
        out_specs=pl.BlockSpec((seq, q_width), lambda b: (b, 0)),
        out_shape=pltpu.HBM((m, q_width), BF16),
        scratch_shapes=[pltpu.VMEM((WINDOW + seq, kv_width), BF16)] * 2,
        compiler_params=_compiler_params(
            1,
            2 * _nbytes((seq, q_width), BF16) + _nbytes((seq, 2 * kv_width), BF16),
            2 * _nbytes((WINDOW + seq, kv_width), BF16) + _nbytes((seq, q_width), F32)),
        name="swa_attention",
    )(sinks, qkv, qkv)


MERGE_ROW_CHUNK = 256


def _merge_kernel(h_ref, yc_ref, at_ref, wga_ref, wgb_ref, woc_ref, woa_ref, wnext_ref,
                  o_ref, wnext_bf_ref, wga_bf, wgb_bf, woc_bf, woa_bf):
    @pl.when(pl.program_id(1) == 0)
    def _():
        wga_bf[...] = wga_ref[...].astype(BF16)
        wgb_bf[...] = wgb_ref[...].astype(BF16)
        woc_bf[...] = woc_ref[...].astype(BF16)
        woa_bf[...] = woa_ref[...].astype(BF16)

    wnext_bf_ref[...] = wnext_ref[...].astype(BF16)
    for rows in _row_chunks(h_ref.shape[0], MERGE_ROW_CHUNK):
        h = h_ref[rows, :]
        ya = _dot(yc_ref[rows, :], woc_bf[...])
        yb = _dot(at_ref[rows, :], woa_bf[...])
        ga = _dot(h, wga_bf[...])
        gb = _dot(h, wgb_bf[...])
        o_ref[rows, :] = (jax.nn.sigmoid(ga) * ya + jax.nn.sigmoid(gb) * yb).astype(o_ref.dtype)


def _merge(h, yconv, attn, w_in, w_out_conv, w_out_attn, w_next, gate_col, bm=512, bn=512):
    m, d = h.shape
    kc, ka = yconv.shape[1], attn.shape[1]
    assert gate_col % bn == 0
    ga0 = gate_col // bn
    nj, ni = d // bn, m // bm
    side_in, side_out, side_shape, side_bytes = _side_cast_specs(w_next, nj * ni, lambda j, i: j * ni + i)
    return pl.pallas_call(
        _merge_kernel,
        grid=(nj, ni),
        in_specs=[pl.BlockSpec((bm, d), lambda j, i: (i, 0)),
                  pl.BlockSpec((bm, kc), lambda j, i: (i, 0)),
                  pl.BlockSpec((bm, ka), lambda j, i: (i, 0)),
                  pl.BlockSpec((d, bn), lambda j, i: (0, ga0 + j)),
                  pl.BlockSpec((d, bn), lambda j, i: (0, ga0 + nj + j)),
                  pl.BlockSpec((kc, bn), lambda j, i: (0, j)),
                  pl.BlockSpec((ka, bn), lambda j, i: (0, j)),
                  side_in],
        out_specs=[pl.BlockSpec((bm, bn), lambda j, i: (i, j)), side_out],
        out_shape=[pltpu.HBM((m, d), BF16), side_shape],
        scratch_shapes=[pltpu.VMEM((d, bn), BF16), pltpu.VMEM((d, bn), BF16),
                        pltpu.VMEM((kc, bn), BF16), pltpu.VMEM((ka, bn), BF16)],
        compiler_params=_compiler_params(
            2,
            _nbytes((bm, d + kc + ka), BF16) + _nbytes((2 * d + kc + ka, bn), F32)
            + _nbytes((bm, bn), BF16) + side_bytes,
            _nbytes((2 * d + kc + ka, bn), BF16) + 10 * _nbytes((MERGE_ROW_CHUNK, bn), F32)),
        name="gated_merge",
    )(h, yconv, attn, w_in, w_in, w_out_conv, w_out_attn, w_next)


def _rope_tables(seq):
    half = ROT_DIM // 2
    inv_freq = 1.0 / (ROPE_THETA ** (jnp.arange(0, ROT_DIM, 2, dtype=F32) / ROT_DIM))
    ang = jnp.arange(seq, dtype=F32)[:, None] * inv_freq[None, :]
    cos, sin = jnp.cos(ang), jnp.sin(ang)
    ones = jnp.ones((seq, HEAD_DIM - ROT_DIM), F32)
    zeros = jnp.zeros((seq, half), F32)
    zeros_tail = jnp.zeros((seq, HEAD_DIM - ROT_DIM), F32)
    per_head = [jnp.concatenate([cos, cos, ones], axis=1),
                jnp.concatenate([-sin, zeros, zeros_tail], axis=1),
                jnp.concatenate([zeros, sin, zeros_tail], axis=1)]
    return jnp.concatenate([t for table in per_head for t in [table] * (LANES // HEAD_DIM)], axis=1)


def _swiglu_half_step(x, h, w_gu, w_down, next_gain):
    act, w_down_bf = _gate_up(h, w_gu, w_down)
    return _matmul_residual(act, w_down_bf, x, 0.5, next_gain)


def kernel(x, g_ffn1, w_gu1, w_down1, g_mix, w_in, conv_w, q_norm_g, k_norm_g, sinks,
           w_out_conv, w_out_attn, w_o, g_ffn2, w_gu2, w_down2):
    batch, seq, d = x.shape
    depth = w_in.shape[0]
    conv_width = conv_w.shape[2]
    q_width = w_out_attn.shape[1]
    kv_width = q_width // GROUP
    q_col = 3 * conv_width
    gate_col = q_col + q_width + 2 * kv_width
    rope = _rope_tables(seq)

    xf = x.reshape(batch * seq, d)
    h = _rmsnorm(xf, g_ffn1[0])
    for l in range(depth):
        xf, h = _swiglu_half_step(xf, h, w_gu1[l], w_down1[l], g_mix[l])

        yconv = _conv_branch(h, w_in[l], conv_w[l], batch, seq, conv_width)
        qkv_bn = 2 * kv_width
        assert q_width % qkv_bn == 0
        gains = jnp.concatenate(
            [jnp.tile(q_norm_g[l] * ATTN_SCALE, q_width // HEAD_DIM),
             jnp.tile(k_norm_g[l], kv_width // HEAD_DIM), jnp.ones((kv_width,), F32)]
        ).reshape(-1, 1, qkv_bn)
        raw_cols = jnp.concatenate(
            [jnp.zeros((q_width + kv_width,), F32), jnp.ones((kv_width,), F32)]
        ).reshape(-1, 1, qkv_bn)
        qkv = _qkv_proj(h, w_in[l], gains, raw_cols, rope, batch, seq, q_col, bn=qkv_bn)
        attn = _attention(qkv, sinks[l], batch, seq, q_width, kv_width)
        merged, w_o_bf = _merge(h, yconv, attn, w_in[l], w_out_conv[l], w_out_attn[l], w_o[l], gate_col)
        xf, h = _matmul_residual(merged, w_o_bf, xf, 1.0, g_ffn2[l])

        next_gain = g_ffn1[l + 1] if l + 1 < depth else None
        xf, h = _swiglu_half_step(xf, h, w_gu2[l], w_down2[l], next_gain)
    return xf.reshape(batch, seq, d)
```

```python
import functools

import jax
import jax.numpy as jnp
from jax import lax
from jax.experimental import pallas as pl
from jax.experimental.pallas import tpu as pltpu

F32 = jnp.float32
BF16 = jnp.bfloat16

HEAD_DIM = 64
GROUP = 4
CONV_K = 3
WINDOW = 128
ROT_DIM = HEAD_DIM // 4
ROPE_THETA = 500000.0
RMS_EPS = 1e-6
ATTN_SCALE = HEAD_DIM ** -0.5
NEG_INF = -1e30

LANES = 128
SUBLANES = 8
MXU_DIM = 256
VMEM_CAP_V7X = 60 * 1024 * 1024


def _nbytes(shape, dtype):
    n = 1
    for s in shape:
        n *= s
    return n * jnp.dtype(dtype).itemsize


def _compiler_params(n_grid_axes, pipelined_bytes, resident_bytes):
    need = 2 * pipelined_bytes + resident_bytes
    return pltpu.CompilerParams(
        dimension_semantics=("arbitrary",) * n_grid_axes,
        vmem_limit_bytes=min(VMEM_CAP_V7X, need + need // 8),
    )


def _dot(a, b):
    return jnp.dot(a, b, preferred_element_type=F32)


ROW_CHUNK = 512


def _row_chunks(n_rows, chunk=ROW_CHUNK):
    chunk = min(chunk, n_rows)
    assert n_rows % chunk == 0
    return [pl.ds(r * chunk, chunk) for r in range(n_rows // chunk)]


def _rmsnorm_kernel(x_ref, g_ref, o_ref):
    x = x_ref[...]
    ms = jnp.mean(x * x, axis=-1, keepdims=True)
    o_ref[...] = (x * lax.rsqrt(ms + RMS_EPS) * g_ref[...]).astype(o_ref.dtype)


def _rmsnorm(x, g, bm=1024):
    m, d = x.shape
    return pl.pallas_call(
        _rmsnorm_kernel,
        grid=(m // bm,),
        in_specs=[pl.BlockSpec((bm, d), lambda i: (i, 0)),
                  pl.BlockSpec((1, d), lambda i: (0, 0))],
        out_specs=pl.BlockSpec((bm, d), lambda i: (i, 0)),
        out_shape=pltpu.HBM((m, d), BF16),
        compiler_params=_compiler_params(
            1, _nbytes((bm, d), F32) + _nbytes((bm, d), BF16), 2 * _nbytes((bm, d), F32)),
        name="rmsnorm",
    )(x, g.reshape(1, d))


def _side_cast_specs(w_next, n_steps, step_index):
    k, n = w_next.shape
    assert k % n_steps == 0 and (k // n_steps) % (2 * SUBLANES) == 0
    rows = k // n_steps
    spec = pl.BlockSpec((rows, n), lambda *ids: (step_index(*ids), 0))
    return spec, spec, pltpu.HBM((k, n), BF16), _nbytes((rows, n), F32) + _nbytes((rows, n), BF16)


def _gate_up_kernel(h_ref, wg_ref, wu_ref, wnext_ref, o_ref, wnext_bf_ref, wg_bf, wu_bf):
    @pl.when(pl.program_id(1) == 0)
    def _():
        wg_bf[...] = wg_ref[...].astype(BF16)
        wu_bf[...] = wu_ref[...].astype(BF16)

    wnext_bf_ref[...] = wnext_ref[...].astype(BF16)
    for rows in _row_chunks(h_ref.shape[0]):
        h = h_ref[rows, :]
        g = _dot(h, wg_bf[...])
        u = _dot(h, wu_bf[...])
        o_ref[rows, :] = (g * jax.nn.sigmoid(g) * u).astype(o_ref.dtype)


def _gate_up(h, w_gu, w_next, bm=2048, bn=512):
    m, d = h.shape
    f = w_gu.shape[1] // 2
    nj, ni = f // bn, m // bm
    side_in, side_out, side_shape, side_bytes = _side_cast_specs(w_next, nj * ni, lambda j, i: j * ni + i)
    return pl.pallas_call(
        _gate_up_kernel,
        grid=(nj, ni),
        in_specs=[pl.BlockSpec((bm, d), lambda j, i: (i, 0)),
                  pl.BlockSpec((d, bn), lambda j, i: (0, j)),
                  pl.BlockSpec((d, bn), lambda j, i: (0, j + nj)),
                  side_in],
        out_specs=[pl.BlockSpec((bm, bn), lambda j, i: (i, j)), side_out],
        out_shape=[pltpu.HBM((m, f), BF16), side_shape],
        scratch_shapes=[pltpu.VMEM((d, bn), BF16), pltpu.VMEM((d, bn), BF16)],
        compiler_params=_compiler_params(
            2,
            _nbytes((bm, d), BF16) + 2 * _nbytes((d, bn), F32) + _nbytes((bm, bn), BF16) + side_bytes,
            2 * _nbytes((d, bn), BF16) + 6 * _nbytes((ROW_CHUNK, bn), F32)),
        name="gate_up",
    )(h, w_gu, w_gu, w_next)


RESIDENT_ROW_CHUNK = 128


def _matmul_residual_kernel(*refs, scale, with_norm):
    if with_norm:
        a_ref, w_ref, r_ref, g_ref, x_ref, h_ref = refs
    else:
        a_ref, w_ref, r_ref, x_ref = refs
    for rows in _row_chunks(a_ref.shape[0], RESIDENT_ROW_CHUNK):
        x = r_ref[rows, :] + scale * _dot(a_ref[rows, :], w_ref[...])
        x_ref[rows, :] = x
        if with_norm:
            ms = jnp.mean(x * x, axis=-1, keepdims=True)
            h_ref[rows, :] = (x * lax.rsqrt(ms + RMS_EPS) * g_ref[...]).astype(h_ref.dtype)


def _matmul_residual(a, w_bf, res, scale, norm_gain=None):
    m, k = a.shape
    n = w_bf.shape[1]
    with_norm = norm_gain is not None
    resident = _nbytes((k, n), BF16) + 6 * _nbytes((RESIDENT_ROW_CHUNK, n), F32)

    def pipelined_bytes(rows):
        return (_nbytes((rows, k), BF16) + 2 * _nbytes((rows, n), F32)
                + (_nbytes((rows, n), BF16) if with_norm else 0))

    bm = next(rows for rows in (1024, 512, 256, RESIDENT_ROW_CHUNK)
              if m % rows == 0 and (2 * pipelined_bytes(rows) + resident) * 9 // 8 <= VMEM_CAP_V7X)
    row_spec = lambda width: pl.BlockSpec((bm, width), lambda i: (i, 0))
    in_specs = [row_spec(k),
                pl.BlockSpec((k, n), lambda i: (0, 0), pipeline_mode=pl.Buffered(1)),
                row_spec(n)]
    operands = [a, pltpu.with_memory_space_constraint(w_bf, pltpu.HBM), res]
    out_specs = [row_spec(n)]
    out_shape = [pltpu.HBM((m, n), F32)]
    if with_norm:
        in_specs.append(pl.BlockSpec((1, n), lambda i: (0, 0)))
        operands.append(norm_gain.reshape(1, n))
        out_specs.append(row_spec(n))
        out_shape.append(pltpu.HBM((m, n), BF16))
    outs = pl.pallas_call(
        functools.partial(_matmul_residual_kernel, scale=scale, with_norm=with_norm),
        grid=(m // bm,),
        in_specs=in_specs,
        out_specs=out_specs,
        out_shape=out_shape,
        compiler_params=_compiler_params(1, pipelined_bytes(bm), resident),
        name="matmul_residual",
    )(*operands)
    return outs if with_norm else (outs[0], None)


def _conv_branch_kernel(h_ref, wx_ref, wb_ref, wc_ref, cw_ref, o_ref, wx_bf, wb_bf, wc_bf, u_pad):
    seq = h_ref.shape[0]

    @pl.when(pl.program_id(1) == 0)
    def _():
        wx_bf[...] = wx_ref[...].astype(BF16)
        wb_bf[...] = wb_ref[...].astype(BF16)
        wc_bf[...] = wc_ref[...].astype(BF16)
        u_pad[0:SUBLANES, :] = jnp.zeros((SUBLANES, u_pad.shape[1]), F32)

    cw = cw_ref[...]
    rc = min(ROW_CHUNK, seq)
    for r0 in range(0, seq, rc):
        h = h_ref[r0:r0 + rc, :]
        u = _dot(h, wc_bf[...]) * _dot(h, wx_bf[...])
        p0 = SUBLANES + r0
        u_pad[p0:p0 + rc, :] = u
        y = (cw[0:1, :] * u_pad[p0 - 2:p0 - 2 + rc, :]
             + cw[1:2, :] * u_pad[p0 - 1:p0 - 1 + rc, :])
        y = y + cw[2:3, :] * u
        o_ref[r0:r0 + rc, :] = (_dot(h, wb_bf[...]) * y).astype(o_ref.dtype)


def _conv_branch(h, w_in, conv_w, batch, seq, width, bn=256):
    m, d = h.shape
    nj = width // bn
    return pl.pallas_call(
        _conv_branch_kernel,
        grid=(nj, batch),
        in_specs=[pl.BlockSpec((seq, d), lambda j, b: (b, 0)),
                  pl.BlockSpec((d, bn), lambda j, b: (0, j)),
                  pl.BlockSpec((d, bn), lambda j, b: (0, j + nj)),
                  pl.BlockSpec((d, bn), lambda j, b: (0, j + 2 * nj)),
                  pl.BlockSpec((CONV_K, bn), lambda j, b: (0, j))],
        out_specs=pl.BlockSpec((seq, bn), lambda j, b: (b, j)),
        out_shape=pltpu.HBM((m, width), BF16),
        scratch_shapes=[pltpu.VMEM((d, bn), BF16)] * 3
        + [pltpu.VMEM((SUBLANES + seq, bn), F32)],
        compiler_params=_compiler_params(
            2,
            _nbytes((seq, d), BF16) + 3 * _nbytes((d, bn), F32) + _nbytes((seq, bn), BF16),
            3 * _nbytes((d, bn), BF16) + _nbytes((seq, bn), F32) + 8 * _nbytes((ROW_CHUNK, bn), F32)),
        name="conv_branch",
    )(h, w_in, w_in, w_in, conv_w)


def _head_norm_rope(x, gain, rope):
    width = x.shape[1]
    cos, sin_hi, sin_lo = (rope[:, i * LANES:(i + 1) * LANES] for i in range(3))
    r = lax.broadcasted_iota(jnp.int32, (MXU_DIM, MXU_DIM), 0) // HEAD_DIM
    c = lax.broadcasted_iota(jnp.int32, (MXU_DIM, MXU_DIM), 1) // HEAD_DIM
    ones_blockdiag = (r == c).astype(BF16)
    out = []
    for t in range(width // MXU_DIM):
        xt = x[:, t * MXU_DIM:(t + 1) * MXU_DIM]
        ss = _dot((xt * xt).astype(BF16), ones_blockdiag)
        y = xt * lax.rsqrt(ss * (1.0 / HEAD_DIM) + RMS_EPS) * gain[:, t * MXU_DIM:(t + 1) * MXU_DIM]
        for s in range(MXU_DIM // LANES):
            ys = y[:, s * LANES:(s + 1) * LANES]
            half = ROT_DIM // 2
            rot = (ys * cos
                   + pltpu.roll(ys, LANES - half, 1) * sin_hi
                   + pltpu.roll(ys, half, 1) * sin_lo)
            out.append(rot)
    return jnp.concatenate(out, axis=1)


def _qkv_kernel(h_ref, w_ref, gain_ref, raw_ref, rope_ref, o_ref, w_bf):
    @pl.when(pl.program_id(1) == 0)
    def _():
        w_bf[...] = w_ref[...].astype(BF16)

    gain = gain_ref[...]
    is_raw = raw_ref[...] != 0.0
    seq = h_ref.shape[0]
    rc = min(ROW_CHUNK, seq)
    for r0 in range(0, seq, rc):
        acc = _dot(h_ref[r0:r0 + rc, :], w_bf[...])
        y = _head_norm_rope(acc, gain, rope_ref[r0:r0 + rc, :])
        o_ref[r0:r0 + rc, :] = jnp.where(is_raw, acc, y).astype(o_ref.dtype)


def _qkv_proj(h, w_in, gains, raw_cols, rope, batch, seq, q_col, bn):
    m, d = h.shape
    assert q_col % bn == 0
    nj = gains.shape[0]
    first = q_col // bn
    table_spec = pl.BlockSpec((seq, 3 * LANES), lambda j, b: (0, 0))
    col_spec = pl.BlockSpec((None, 1, bn), lambda j, b: (j, 0, 0))
    return pl.pallas_call(
        _qkv_kernel,
        grid=(nj, batch),
        in_specs=[pl.BlockSpec((seq, d), lambda j, b: (b, 0)),
                  pl.BlockSpec((d, bn), lambda j, b: (0, first + j)),
                  col_spec, col_spec, table_spec],
        out_specs=pl.BlockSpec((seq, bn), lambda j, b: (b, j)),
        out_shape=pltpu.HBM((m, nj * bn), BF16),
        scratch_shapes=[pltpu.VMEM((d, bn), BF16)],
        compiler_params=_compiler_params(
            2,
            _nbytes((seq, d), BF16) + _nbytes((d, bn), F32) + _nbytes((seq, bn), BF16)
            + _nbytes((seq, 3 * LANES), F32),
            _nbytes((d, bn), BF16) + 10 * _nbytes((ROW_CHUNK, bn), F32)),
        name="qkv_proj",
    )(h, w_in, gains, raw_cols, rope)


def _attention_kernel(sink_ref, q_ref, kv_ref, o_ref, k_lo, k_hi, v_lo, v_hi, bias, s_scr, p_scr,
                      *, n_kv_heads):
    seq = q_ref.shape[0]
    kw = n_kv_heads * HEAD_DIM
    two_w = 2 * WINDOW
    assert LANES == 2 * HEAD_DIM and GROUP == 4

    lane = lax.broadcasted_iota(jnp.int32, (seq, LANES), 1)
    zero_block = jnp.zeros((WINDOW, LANES), BF16)
    for src0, lo_ref, hi_ref in ((0, k_lo, k_hi), (kw, v_lo, v_hi)):
        for grp in range(kw // LANES):
            x = kv_ref[:, src0 + grp * LANES:src0 + (grp + 1) * LANES].astype(F32)
            swapped = pltpu.roll(x, HEAD_DIM, 1)
            even_lo, odd_lo = jnp.where(lane < HEAD_DIM, x, 0.0), jnp.where(lane < HEAD_DIM, swapped, 0.0)
            even_hi, odd_hi = jnp.where(lane >= HEAD_DIM, swapped, 0.0), jnp.where(lane >= HEAD_DIM, x, 0.0)
            for hk, lo, hi in ((2 * grp, even_lo, even_hi), (2 * grp + 1, odd_lo, odd_hi)):
                lo_ref[hk, 0:WINDOW, :] = zero_block
                hi_ref[hk, 0:WINDOW, :] = zero_block
                lo_ref[hk, WINDOW:WINDOW + seq, :] = lo.astype(BF16)
                hi_ref[hk, WINDOW:WINDOW + seq, :] = hi.astype(BF16)

    key = lax.broadcasted_iota(jnp.int32, (2 * two_w, two_w), 0) % two_w
    qry = lax.broadcasted_iota(jnp.int32, (2 * two_w, two_w), 1) % WINDOW
    diff = qry + WINDOW - key
    in_window = (diff >= 0) & (diff < WINDOW)
    bias[0] = jnp.where(in_window & (key >= WINDOW), 0.0, NEG_INF)
    bias[1] = jnp.where(in_window, 0.0, NEG_INF)
    first_pair = lax.broadcasted_iota(jnp.int32, (1, two_w), 1) < WINDOW

    def block(n, carry):
        r0 = pl.multiple_of(n * WINDOW, WINDOW)
        t = jnp.minimum(n, 1)
        for hk in range(n_kv_heads):
            c0 = hk * GROUP * HEAD_DIM
            q_pairs = jnp.concatenate([q_ref[pl.ds(r0, WINDOW), c0:c0 + LANES],
                                       q_ref[pl.ds(r0, WINDOW), c0 + LANES:c0 + 2 * LANES]], axis=0)
            keys = jnp.concatenate([k_lo[hk, pl.ds(r0, two_w), :], k_hi[hk, pl.ds(r0, two_w), :]], axis=0)
            s_scr[hk] = lax.dot_general(keys, q_pairs, (((1,), (1,)), ((), ())), preferred_element_type=F32)
        inv = []
        for hk in range(n_kv_heads):
            for half in range(2):
                rows = slice(half * two_w, (half + 1) * two_w)
                s_h = s_scr[hk, rows, :] + bias[t, rows, :]
                sink = jnp.where(first_pair, sink_ref[hk * GROUP + half], sink_ref[hk * GROUP + 2 + half])
                mx = jnp.maximum(jnp.max(s_h, axis=0, keepdims=True), sink)
                p = jnp.exp(s_h - mx)
                inv.append(1.0 / (jnp.sum(p, axis=0, keepdims=True) + jnp.exp(sink - mx)))
                p_scr[hk, rows, :] = p.astype(BF16)
        for hk in range(n_kv_heads):
            c0 = hk * GROUP * HEAD_DIM
            vals = jnp.concatenate([v_lo[hk, pl.ds(r0, two_w), :], v_hi[hk, pl.ds(r0, two_w), :]], axis=0)
            o_t = lax.dot_general(vals, p_scr[hk], (((0,), (0,)), ((), ())), preferred_element_type=F32)
            o_t = jnp.concatenate([o_t[0:HEAD_DIM, :] * inv[2 * hk],
                                   o_t[HEAD_DIM:LANES, :] * inv[2 * hk + 1]], axis=0)
            for pair in range(2):
                o_ref[pl.ds(r0, WINDOW), c0 + pair * LANES:c0 + (pair + 1) * LANES] = (
                    o_t[:, pair * WINDOW:(pair + 1) * WINDOW].T.astype(o_ref.dtype))
        return carry

    lax.fori_loop(0, seq // WINDOW, block, 0)


def _attention(qkv, sinks, batch, seq, q_width, kv_width):
    m = qkv.shape[0]
    assert q_width % (2 * kv_width) == 0
    n_kv_heads = kv_width // HEAD_DIM
    return pl.pallas_call(
        functools.partial(_attention_kernel, n_kv_heads=n_kv_heads),
        grid=(batch,),
        in_specs=[pl.BlockSpec(memory_space=pltpu.SMEM),
                  pl.BlockSpec((seq, q_width), lambda b: (b, 0)),
                  pl.BlockSpec((seq, 2 * kv_width), lambda b: (b, q_width // (2 * kv_width)))],
        out_specs=pl.BlockSpec((seq, q_width), lambda b: (b, 0)),
        out_shape=pltpu.HBM((m, q_width), BF16),
        scratch_shapes=[pltpu.VMEM((n_kv_heads, WINDOW + seq, LANES), BF16)] * 4
        + [pltpu.VMEM((2, 4 * WINDOW, 2 * WINDOW), F32),
           pltpu.VMEM((n_kv_heads, 4 * WINDOW, 2 * WINDOW), F32),
           pltpu.VMEM((n_kv_heads, 4 * WINDOW, 2 * WINDOW), BF16)],
        compiler_params=_compiler_params(
            1,
            2 * _nbytes((seq, q_width), BF16) + _nbytes((seq, 2 * kv_width), BF16),
            4 * _nbytes((n_kv_heads, WINDOW + seq, LANES), BF16)
            + (8 + 2 * n_kv_heads) * _nbytes((4 * WINDOW, 2 * WINDOW), F32)
            + 4 * _nbytes((seq, LANES), F32)),
        name="swa_attention",
    )(sinks, qkv, qkv)


MERGE_ROW_CHUNK = 256


def _merge_kernel(h_ref, yc_ref, at_ref, wga_ref, wgb_ref, woc_ref, woa_ref, wnext_ref,
                  o_ref, wnext_bf_ref, wga_bf, wgb_bf, woc_bf, woa_bf):
    @pl.when(pl.program_id(1) == 0)
    def _():
        wga_bf[...] = wga_ref[...].astype(BF16)
        wgb_bf[...] = wgb_ref[...].astype(BF16)
        woc_bf[...] = woc_ref[...].astype(BF16)
        woa_bf[...] = woa_ref[...].astype(BF16)

    wnext_bf_ref[...] = wnext_ref[...].astype(BF16)
    for rows in _row_chunks(h_ref.shape[0], MERGE_ROW_CHUNK):
        h = h_ref[rows, :]
        ya = _dot(yc_ref[rows, :], woc_bf[...])
        yb = _dot(at_ref[rows, :], woa_bf[...])
        ga = _dot(h, wga_bf[...])
        gb = _dot(h, wgb_bf[...])
        o_ref[rows, :] = (jax.nn.sigmoid(ga) * ya + jax.nn.sigmoid(gb) * yb).astype(o_ref.dtype)


def _merge(h, yconv, attn, w_in, w_out_conv, w_out_attn, w_next, gate_col, bm=512, bn=512):
    m, d = h.shape
    kc, ka = yconv.shape[1], attn.shape[1]
    assert gate_col % bn == 0
    ga0 = gate_col // bn
    nj, ni = d // bn, m // bm
    side_in, side_out, side_shape, side_bytes = _side_cast_specs(w_next, nj * ni, lambda j, i: j * ni + i)
    return pl.pallas_call(
        _merge_kernel,
        grid=(nj, ni),
        in_specs=[pl.BlockSpec((bm, d), lambda j, i: (i, 0)),
                  pl.BlockSpec((bm, kc), lambda j, i: (i, 0)),
                  pl.BlockSpec((bm, ka), lambda j, i: (i, 0)),
                  pl.BlockSpec((d, bn), lambda j, i: (0, ga0 + j)),
                  pl.BlockSpec((d, bn), lambda j, i: (0, ga0 + nj + j)),
                  pl.BlockSpec((kc, bn), lambda j, i: (0, j)),
                  pl.BlockSpec((ka, bn), lambda j, i: (0, j)),
                  side_in],
        out_specs=[pl.BlockSpec((bm, bn), lambda j, i: (i, j)), side_out],
        out_shape=[pltpu.HBM((m, d), BF16), side_shape],
        scratch_shapes=[pltpu.VMEM((d, bn), BF16), pltpu.VMEM((d, bn), BF16),
                        pltpu.VMEM((kc, bn), BF16), pltpu.VMEM((ka, bn), BF16)],
        compiler_params=_compiler_params(
            2,
            _nbytes((bm, d + kc + ka), BF16) + _nbytes((2 * d + kc + ka, bn), F32)
            + _nbytes((bm, bn), BF16) + side_bytes,
            _nbytes((2 * d + kc + ka, bn), BF16) + 10 * _nbytes((MERGE_ROW_CHUNK, bn), F32)),
        name="gated_merge",
    )(h, yconv, attn, w_in, w_in, w_out_conv, w_out_attn, w_next)


def _rope_tables(seq):
    half = ROT_DIM // 2
    inv_freq = 1.0 / (ROPE_THETA ** (jnp.arange(0, ROT_DIM, 2, dtype=F32) / ROT_DIM))
    lane = jnp.arange(3 * LANES)
    kind, dim = lane // LANES, lane % HEAD_DIM
    ang = jnp.arange(seq, dtype=F32)[:, None] * jnp.tile(inv_freq, 3 * LANES // half)[None, :]
    cos, sin = jnp.cos(ang), jnp.sin(ang)
    cos_t = jnp.where(dim < ROT_DIM, cos, 1.0)
    sin_hi = jnp.where(dim < half, -sin, 0.0)
    sin_lo = jnp.where((dim >= half) & (dim < ROT_DIM), sin, 0.0)
    return jnp.where(kind == 0, cos_t, jnp.where(kind == 1, sin_hi, sin_lo))


def _swiglu_half_step(x, h, w_gu, w_down, next_gain):
    act, w_down_bf = _gate_up(h, w_gu, w_down)
    return _matmul_residual(act, w_down_bf, x, 0.5, next_gain)


def kernel(x, g_ffn1, w_gu1, w_down1, g_mix, w_in, conv_w, q_norm_g, k_norm_g, sinks,
           w_out_conv, w_out_attn, w_o, g_ffn2, w_gu2, w_down2):
    batch, seq, d = x.shape
    depth = w_in.shape[0]
    conv_width = conv_w.shape[2]
    q_width = w_out_attn.shape[1]
    kv_width = q_width // GROUP
    q_col = 3 * conv_width
    gate_col = q_col + q_width + 2 * kv_width
    rope = _rope_tables(seq)

    xf = x.reshape(batch * seq, d)
    h = _rmsnorm(xf, g_ffn1[0])
    for l in range(depth):
        xf, h = _swiglu_half_step(xf, h, w_gu1[l], w_down1[l], g_mix[l])

        yconv = _conv_branch(h, w_in[l], conv_w[l], batch, seq, conv_width)
        qkv_bn = 2 * kv_width
        assert q_width % qkv_bn == 0
        gains = jnp.concatenate(
            [jnp.tile(q_norm_g[l] * ATTN_SCALE, q_width // HEAD_DIM),
             jnp.tile(k_norm_g[l], kv_width // HEAD_DIM), jnp.ones((kv_width,), F32)]
        ).reshape(-1, 1, qkv_bn)
        raw_cols = jnp.concatenate(
            [jnp.zeros((q_width + kv_width,), F32), jnp.ones((kv_width,), F32)]
        ).reshape(-1, 1, qkv_bn)
        qkv = _qkv_proj(h, w_in[l], gains, raw_cols, rope, batch, seq, q_col, bn=qkv_bn)
        attn = _attention(qkv, sinks[l], batch, seq, q_width, kv_width)
        merged, w_o_bf = _merge(h, yconv, attn, w_in[l], w_out_conv[l], w_out_attn[l], w_o[l], gate_col)
        xf, h = _matmul_residual(merged, w_o_bf, xf, 1.0, g_ffn2[l])

        next_gain = g_ffn1[l + 1] if l + 1 < depth else None
        xf, h = _swiglu_half_step(xf, h, w_gu2[l], w_down2[l], next_gain)
    return xf.reshape(batch, seq, d)
```

```python
import functools

import jax
import jax.numpy as jnp
import numpy as np
from jax import lax
from jax.experimental import pallas as pl
from jax.experimental.pallas import tpu as pltpu

F32 = jnp.float32
BF16 = jnp.bfloat16

HEAD_DIM = 64
GROUP = 4
CONV_K = 3
WINDOW = 128
ROT_DIM = HEAD_DIM // 4
ROPE_THETA = 500000.0
RMS_EPS = 1e-6
ATTN_SCALE = HEAD_DIM ** -0.5
NEG_INF = -1e30

LANES = 128
SUBLANES = 8
MXU_DIM = 256
VMEM_CAP_V7X = 60 * 1024 * 1024


def _nbytes(shape, dtype):
    n = 1
    for s in shape:
        n *= s
    return n * jnp.dtype(dtype).itemsize


def _compiler_params(n_grid_axes, pipelined_bytes, resident_bytes):
    need = 2 * pipelined_bytes + resident_bytes
    return pltpu.CompilerParams(
        dimension_semantics=("arbitrary",) * n_grid_axes,
        vmem_limit_bytes=min(VMEM_CAP_V7X, need + need // 8),
    )


def _dot(a, b):
    return jnp.dot(a, b, preferred_element_type=F32)


ROW_CHUNK = 512


def _row_chunks(n_rows, chunk=ROW_CHUNK):
    chunk = min(chunk, n_rows)
    assert n_rows % chunk == 0
    return [pl.ds(r * chunk, chunk) for r in range(n_rows // chunk)]


def _rmsnorm_kernel(x_ref, g_ref, o_ref):
    x = x_ref[...]
    ms = jnp.mean(x * x, axis=-1, keepdims=True)
    o_ref[...] = (x * lax.rsqrt(ms + RMS_EPS) * g_ref[...]).astype(o_ref.dtype)


def _rmsnorm(x, g, bm=1024):
    m, d = x.shape
    return pl.pallas_call(
        _rmsnorm_kernel,
        grid=(m // bm,),
        in_specs=[pl.BlockSpec((bm, d), lambda i: (i, 0)),
                  pl.BlockSpec((1, d), lambda i: (0, 0))],
        out_specs=pl.BlockSpec((bm, d), lambda i: (i, 0)),
        out_shape=pltpu.HBM((m, d), BF16),
        compiler_params=_compiler_params(
            1, _nbytes((bm, d), F32) + _nbytes((bm, d), BF16), 2 * _nbytes((bm, d), F32)),
        name="rmsnorm",
    )(x, g.reshape(1, d))


def _side_cast_specs(w_next, n_steps, step_index):
    k, n = w_next.shape
    assert k % n_steps == 0 and (k // n_steps) % (2 * SUBLANES) == 0
    rows = k // n_steps
    spec = pl.BlockSpec((rows, n), lambda *ids: (step_index(*ids), 0))
    return spec, spec, pltpu.HBM((k, n), BF16), _nbytes((rows, n), F32) + _nbytes((rows, n), BF16)


def _gate_up_kernel(h_ref, wg_ref, wu_ref, wnext_ref, o_ref, wnext_bf_ref, wg_bf, wu_bf):
    @pl.when(pl.program_id(1) == 0)
    def _():
        wg_bf[...] = wg_ref[...].astype(BF16)
        wu_bf[...] = wu_ref[...].astype(BF16)

    wnext_bf_ref[...] = wnext_ref[...].astype(BF16)
    for rows in _row_chunks(h_ref.shape[0]):
        h = h_ref[rows, :]
        g = _dot(h, wg_bf[...])
        u = _dot(h, wu_bf[...])
        o_ref[rows, :] = (g * jax.nn.sigmoid(g) * u).astype(o_ref.dtype)


def _gate_up(h, w_gu, w_next, bm=2048, bn=512):
    m, d = h.shape
    f = w_gu.shape[1] // 2
    nj, ni = f // bn, m // bm
    side_in, side_out, side_shape, side_bytes = _side_cast_specs(w_next, nj * ni, lambda j, i: j * ni + i)
    return pl.pallas_call(
        _gate_up_kernel,
        grid=(nj, ni),
        in_specs=[pl.BlockSpec((bm, d), lambda j, i: (i, 0)),
                  pl.BlockSpec((d, bn), lambda j, i: (0, j)),
                  pl.BlockSpec((d, bn), lambda j, i: (0, j + nj)),
                  side_in],
        out_specs=[pl.BlockSpec((bm, bn), lambda j, i: (i, j)), side_out],
        out_shape=[pltpu.HBM((m, f), BF16), side_shape],
        scratch_shapes=[pltpu.VMEM((d, bn), BF16), pltpu.VMEM((d, bn), BF16)],
        compiler_params=_compiler_params(
            2,
            _nbytes((bm, d), BF16) + 2 * _nbytes((d, bn), F32) + _nbytes((bm, bn), BF16) + side_bytes,
            2 * _nbytes((d, bn), BF16) + 6 * _nbytes((ROW_CHUNK, bn), F32)),
        name="gate_up",
    )(h, w_gu, w_gu, w_next)


RESIDENT_CHUNKS = 2


def _matmul_residual_kernel(*refs, scale, with_norm):
    if with_norm:
        a_ref, w_ref, r_ref, g_ref, x_ref, h_ref = refs
    else:
        a_ref, w_ref, r_ref, x_ref = refs
    for rows in _row_chunks(a_ref.shape[0], a_ref.shape[0] // RESIDENT_CHUNKS):
        x = r_ref[rows, :] + scale * _dot(a_ref[rows, :], w_ref[...])
        x_ref[rows, :] = x
        if with_norm:
            ms = jnp.mean(x * x, axis=-1, keepdims=True)
            h_ref[rows, :] = (x * lax.rsqrt(ms + RMS_EPS) * g_ref[...]).astype(h_ref.dtype)


def _matmul_residual(a, w_bf, res, scale, norm_gain=None):
    m, k = a.shape
    n = w_bf.shape[1]
    with_norm = norm_gain is not None
    def resident_bytes(rows):
        return _nbytes((k, n), BF16) + 6 * _nbytes((rows // RESIDENT_CHUNKS, n), F32)

    def pipelined_bytes(rows):
        return (_nbytes((rows, k), BF16) + 2 * _nbytes((rows, n), F32)
                + (_nbytes((rows, n), BF16) if with_norm else 0))

    bm = next(rows for rows in (1024, 512, 256, 128)
              if m % rows == 0
              and (2 * pipelined_bytes(rows) + resident_bytes(rows)) * 9 // 8 <= VMEM_CAP_V7X)
    row_spec = lambda width: pl.BlockSpec((bm, width), lambda i: (i, 0))
    in_specs = [row_spec(k),
                pl.BlockSpec((k, n), lambda i: (0, 0), pipeline_mode=pl.Buffered(1)),
                row_spec(n)]
    operands = [a, pltpu.with_memory_space_constraint(w_bf, pltpu.HBM), res]
    out_specs = [row_spec(n)]
    out_shape = [pltpu.HBM((m, n), F32)]
    if with_norm:
        in_specs.append(pl.BlockSpec((1, n), lambda i: (0, 0)))
        operands.append(norm_gain.reshape(1, n))
        out_specs.append(row_spec(n))
        out_shape.append(pltpu.HBM((m, n), BF16))
    outs = pl.pallas_call(
        functools.partial(_matmul_residual_kernel, scale=scale, with_norm=with_norm),
        grid=(m // bm,),
        in_specs=in_specs,
        out_specs=out_specs,
        out_shape=out_shape,
        compiler_params=_compiler_params(1, pipelined_bytes(bm), resident_bytes(bm)),
        name="matmul_residual",
    )(*operands)
    return outs if with_norm else (outs[0], None)


def _conv_branch_kernel(h_ref, wx_ref, wb_ref, wc_ref, cw_ref, o_ref, wx_bf, wb_bf, wc_bf, u_pad):
    seq = h_ref.shape[0]

    @pl.when(pl.program_id(1) == 0)
    def _():
        wx_bf[...] = wx_ref[...].astype(BF16)
        wb_bf[...] = wb_ref[...].astype(BF16)
        wc_bf[...] = wc_ref[...].astype(BF16)
        u_pad[0:SUBLANES, :] = jnp.zeros((SUBLANES, u_pad.shape[1]), F32)

    cw = cw_ref[...]
    rc = min(ROW_CHUNK, seq)
    for r0 in range(0, seq, rc):
        h = h_ref[r0:r0 + rc, :]
        u = _dot(h, wc_bf[...]) * _dot(h, wx_bf[...])
        p0 = SUBLANES + r0
        u_pad[p0:p0 + rc, :] = u
        y = (cw[0:1, :] * u_pad[p0 - 2:p0 - 2 + rc, :]
             + cw[1:2, :] * u_pad[p0 - 1:p0 - 1 + rc, :])
        y = y + cw[2:3, :] * u
        o_ref[r0:r0 + rc, :] = (_dot(h, wb_bf[...]) * y).astype(o_ref.dtype)


def _conv_branch(h, w_in, conv_w, batch, seq, width, bn=256):
    m, d = h.shape
    nj = width // bn
    return pl.pallas_call(
        _conv_branch_kernel,
        grid=(nj, batch),
        in_specs=[pl.BlockSpec((seq, d), lambda j, b: (b, 0)),
                  pl.BlockSpec((d, bn), lambda j, b: (0, j)),
                  pl.BlockSpec((d, bn), lambda j, b: (0, j + nj)),
                  pl.BlockSpec((d, bn), lambda j, b: (0, j + 2 * nj)),
                  pl.BlockSpec((CONV_K, bn), lambda j, b: (0, j))],
        out_specs=pl.BlockSpec((seq, bn), lambda j, b: (b, j)),
        out_shape=pltpu.HBM((m, width), BF16),
        scratch_shapes=[pltpu.VMEM((d, bn), BF16)] * 3
        + [pltpu.VMEM((SUBLANES + seq, bn), F32)],
        compiler_params=_compiler_params(
            2,
            _nbytes((seq, d), BF16) + 3 * _nbytes((d, bn), F32) + _nbytes((seq, bn), BF16),
            3 * _nbytes((d, bn), BF16) + _nbytes((seq, bn), F32) + 8 * _nbytes((ROW_CHUNK, bn), F32)),
        name="conv_branch",
    )(h, w_in, w_in, w_in, conv_w)


def _head_norm_rope(x, gain, rope):
    width = x.shape[1]
    cos, sin_hi, sin_lo = (rope[:, i * LANES:(i + 1) * LANES] for i in range(3))
    r = lax.broadcasted_iota(jnp.int32, (MXU_DIM, MXU_DIM), 0) // HEAD_DIM
    c = lax.broadcasted_iota(jnp.int32, (MXU_DIM, MXU_DIM), 1) // HEAD_DIM
    ones_blockdiag = (r == c).astype(BF16)
    out = []
    for t in range(width // MXU_DIM):
        xt = x[:, t * MXU_DIM:(t + 1) * MXU_DIM]
        ss = _dot((xt * xt).astype(BF16), ones_blockdiag)
        y = xt * lax.rsqrt(ss * (1.0 / HEAD_DIM) + RMS_EPS) * gain[:, t * MXU_DIM:(t + 1) * MXU_DIM]
        for s in range(MXU_DIM // LANES):
            ys = y[:, s * LANES:(s + 1) * LANES]
            half = ROT_DIM // 2
            rot = (ys * cos
                   + pltpu.roll(ys, LANES - half, 1) * sin_hi
                   + pltpu.roll(ys, half, 1) * sin_lo)
            out.append(rot)
    return jnp.concatenate(out, axis=1)


def _qkv_kernel(h_ref, w_ref, gain_ref, raw_ref, rope_ref, o_ref, w_bf):
    @pl.when(pl.program_id(1) == 0)
    def _():
        w_bf[...] = w_ref[...].astype(BF16)

    gain = gain_ref[...]
    is_raw = raw_ref[...] != 0.0
    seq = h_ref.shape[0]
    rc = min(ROW_CHUNK, seq)
    for r0 in range(0, seq, rc):
        acc = _dot(h_ref[r0:r0 + rc, :], w_bf[...])
        y = _head_norm_rope(acc, gain, rope_ref[r0:r0 + rc, :])
        o_ref[r0:r0 + rc, :] = jnp.where(is_raw, acc, y).astype(o_ref.dtype)


def _qkv_proj(h, w_in, gains, raw_cols, rope, batch, seq, q_col, bn):
    m, d = h.shape
    assert q_col % bn == 0
    nj = gains.shape[0]
    first = q_col // bn
    table_spec = pl.BlockSpec((seq, 3 * LANES), lambda j, b: (0, 0))
    col_spec = pl.BlockSpec((None, 1, bn), lambda j, b: (j, 0, 0))
    return pl.pallas_call(
        _qkv_kernel,
        grid=(nj, batch),
        in_specs=[pl.BlockSpec((seq, d), lambda j, b: (b, 0)),
                  pl.BlockSpec((d, bn), lambda j, b: (0, first + j)),
                  col_spec, col_spec, table_spec],
        out_specs=pl.BlockSpec((seq, bn), lambda j, b: (b, j)),
        out_shape=pltpu.HBM((m, nj * bn), BF16),
        scratch_shapes=[pltpu.VMEM((d, bn), BF16)],
        compiler_params=_compiler_params(
            2,
            _nbytes((seq, d), BF16) + _nbytes((d, bn), F32) + _nbytes((seq, bn), BF16)
            + _nbytes((seq, 3 * LANES), F32),
            _nbytes((d, bn), BF16) + 10 * _nbytes((ROW_CHUNK, bn), F32)),
        name="qkv_proj",
    )(h, w_in, gains, raw_cols, rope)


ATTN_BLOCKS_PER_ITER = 2


def _attention_kernel(sink_ref, q_ref, kv_ref, o_ref, k_lo, k_hi, v_lo, v_hi, bias, s_scr, p_scr,
                      *, n_kv_heads):
    seq = q_ref.shape[0]
    kw = n_kv_heads * HEAD_DIM
    two_w = 2 * WINDOW
    assert LANES == 2 * HEAD_DIM and GROUP == 4

    lane = lax.broadcasted_iota(jnp.int32, (seq, LANES), 1)
    zero_block = jnp.zeros((WINDOW, LANES), BF16)
    for src0, lo_ref, hi_ref in ((0, k_lo, k_hi), (kw, v_lo, v_hi)):
        for grp in range(kw // LANES):
            x = kv_ref[:, src0 + grp * LANES:src0 + (grp + 1) * LANES].astype(F32)
            swapped = pltpu.roll(x, HEAD_DIM, 1)
            even_lo, odd_lo = jnp.where(lane < HEAD_DIM, x, 0.0), jnp.where(lane < HEAD_DIM, swapped, 0.0)
            even_hi, odd_hi = jnp.where(lane >= HEAD_DIM, swapped, 0.0), jnp.where(lane >= HEAD_DIM, x, 0.0)
            for hk, lo, hi in ((2 * grp, even_lo, even_hi), (2 * grp + 1, odd_lo, odd_hi)):
                lo_ref[hk, 0:WINDOW, :] = zero_block
                hi_ref[hk, 0:WINDOW, :] = zero_block
                lo_ref[hk, WINDOW:WINDOW + seq, :] = lo.astype(BF16)
                hi_ref[hk, WINDOW:WINDOW + seq, :] = hi.astype(BF16)

    @pl.when(pl.program_id(0) == 0)
    def _():
        key = lax.broadcasted_iota(jnp.int32, (2 * two_w, two_w), 0) % two_w
        qry = lax.broadcasted_iota(jnp.int32, (2 * two_w, two_w), 1) % WINDOW
        diff = qry + WINDOW - key
        in_window = (diff >= 0) & (diff < WINDOW)
        bias[0] = jnp.where(in_window & (key >= WINDOW), 0.0, NEG_INF)
        bias[1] = jnp.where(in_window, 0.0, NEG_INF)

    first_pair = lax.broadcasted_iota(jnp.int32, (1, two_w), 1) < WINDOW

    n_slabs = s_scr.shape[0]
    blocks_per_iter = n_slabs // n_kv_heads

    def blocks(i, carry):
        items = [(u * n_kv_heads + hk, hk, i * blocks_per_iter + u)
                 for u in range(blocks_per_iter) for hk in range(n_kv_heads)]
        for slab, hk, n in items:
            r0 = pl.multiple_of(n * WINDOW, WINDOW)
            c0 = hk * GROUP * HEAD_DIM
            q_pairs = jnp.concatenate([q_ref[pl.ds(r0, WINDOW), c0:c0 + LANES],
                                       q_ref[pl.ds(r0, WINDOW), c0 + LANES:c0 + 2 * LANES]], axis=0)
            keys = jnp.concatenate([k_lo[hk, pl.ds(r0, two_w), :], k_hi[hk, pl.ds(r0, two_w), :]], axis=0)
            s_scr[slab] = lax.dot_general(keys, q_pairs, (((1,), (1,)), ((), ())), preferred_element_type=F32)
        inv = []
        for slab, hk, n in items:
            t = jnp.minimum(n, 1)
            for half in range(2):
                rows = slice(half * two_w, (half + 1) * two_w)
                s_h = s_scr[slab, rows, :] + bias[t, rows, :]
                sink = jnp.where(first_pair, sink_ref[hk * GROUP + half], sink_ref[hk * GROUP + 2 + half])
                mx = jnp.maximum(jnp.max(s_h, axis=0, keepdims=True), sink)
                p = jnp.exp(s_h - mx)
                inv.append(1.0 / (jnp.sum(p, axis=0, keepdims=True) + jnp.exp(sink - mx)))
                p_scr[slab, rows, :] = p.astype(BF16)
        for slab, hk, n in items:
            r0 = pl.multiple_of(n * WINDOW, WINDOW)
            c0 = hk * GROUP * HEAD_DIM
            vals = jnp.concatenate([v_lo[hk, pl.ds(r0, two_w), :], v_hi[hk, pl.ds(r0, two_w), :]], axis=0)
            o_t = lax.dot_general(vals, p_scr[slab], (((0,), (0,)), ((), ())), preferred_element_type=F32)
            o_t = jnp.concatenate([o_t[0:HEAD_DIM, :] * inv[2 * slab],
                                   o_t[HEAD_DIM:LANES, :] * inv[2 * slab + 1]], axis=0)
            for pair in range(2):
                o_ref[pl.ds(r0, WINDOW), c0 + pair * LANES:c0 + (pair + 1) * LANES] = (
                    o_t[:, pair * WINDOW:(pair + 1) * WINDOW].T.astype(o_ref.dtype))
        return carry

    assert (seq // WINDOW) % blocks_per_iter == 0
    lax.fori_loop(0, seq // WINDOW // blocks_per_iter, blocks, 0)


def _attention(qkv, sinks, batch, seq, q_width, kv_width):
    m = qkv.shape[0]
    assert q_width % (2 * kv_width) == 0
    n_kv_heads = kv_width // HEAD_DIM
    n_slabs = ATTN_BLOCKS_PER_ITER * n_kv_heads
    return pl.pallas_call(
        functools.partial(_attention_kernel, n_kv_heads=n_kv_heads),
        grid=(batch,),
        in_specs=[pl.BlockSpec(memory_space=pltpu.SMEM),
                  pl.BlockSpec((seq, q_width), lambda b: (b, 0)),
                  pl.BlockSpec((seq, 2 * kv_width), lambda b: (b, q_width // (2 * kv_width)))],
        out_specs=pl.BlockSpec((seq, q_width), lambda b: (b, 0)),
        out_shape=pltpu.HBM((m, q_width), BF16),
        scratch_shapes=[pltpu.VMEM((n_kv_heads, WINDOW + seq, LANES), BF16)] * 4
        + [pltpu.VMEM((2, 4 * WINDOW, 2 * WINDOW), F32),
           pltpu.VMEM((n_slabs, 4 * WINDOW, 2 * WINDOW), F32),
           pltpu.VMEM((n_slabs, 4 * WINDOW, 2 * WINDOW), BF16)],
        compiler_params=_compiler_params(
            1,
            2 * _nbytes((seq, q_width), BF16) + _nbytes((seq, 2 * kv_width), BF16),
            4 * _nbytes((n_kv_heads, WINDOW + seq, LANES), BF16)
            + (8 + 2 * n_slabs) * _nbytes((4 * WINDOW, 2 * WINDOW), F32)
            + 4 * _nbytes((seq, LANES), F32)),
        name="swa_attention",
    )(sinks, qkv, qkv)


MERGE_ROW_CHUNK = 256


def _merge_kernel(h_ref, yc_ref, at_ref, wga_ref, wgb_ref, woc_ref, woa_ref, wnext_ref,
                  o_ref, wnext_bf_ref, wga_bf, wgb_bf, woc_bf, woa_bf):
    @pl.when(pl.program_id(1) == 0)
    def _():
        wga_bf[...] = wga_ref[...].astype(BF16)
        wgb_bf[...] = wgb_ref[...].astype(BF16)
        woc_bf[...] = woc_ref[...].astype(BF16)
        woa_bf[...] = woa_ref[...].astype(BF16)

    wnext_bf_ref[...] = wnext_ref[...].astype(BF16)
    for rows in _row_chunks(h_ref.shape[0], MERGE_ROW_CHUNK):
        h = h_ref[rows, :]
        ya = _dot(yc_ref[rows, :], woc_bf[...])
        yb = _dot(at_ref[rows, :], woa_bf[...])
        ga = _dot(h, wga_bf[...])
        gb = _dot(h, wgb_bf[...])
        o_ref[rows, :] = (jax.nn.sigmoid(ga) * ya + jax.nn.sigmoid(gb) * yb).astype(o_ref.dtype)


def _merge(h, yconv, attn, w_in, w_out_conv, w_out_attn, w_next, gate_col, bm=512, bn=512):
    m, d = h.shape
    kc, ka = yconv.shape[1], attn.shape[1]
    assert gate_col % bn == 0
    ga0 = gate_col // bn
    nj, ni = d // bn, m // bm
    side_in, side_out, side_shape, side_bytes = _side_cast_specs(w_next, nj * ni, lambda j, i: j * ni + i)
    return pl.pallas_call(
        _merge_kernel,
        grid=(nj, ni),
        in_specs=[pl.BlockSpec((bm, d), lambda j, i: (i, 0)),
                  pl.BlockSpec((bm, kc), lambda j, i: (i, 0)),
                  pl.BlockSpec((bm, ka), lambda j, i: (i, 0)),
                  pl.BlockSpec((d, bn), lambda j, i: (0, ga0 + j)),
                  pl.BlockSpec((d, bn), lambda j, i: (0, ga0 + nj + j)),
                  pl.BlockSpec((kc, bn), lambda j, i: (0, j)),
                  pl.BlockSpec((ka, bn), lambda j, i: (0, j)),
                  side_in],
        out_specs=[pl.BlockSpec((bm, bn), lambda j, i: (i, j)), side_out],
        out_shape=[pltpu.HBM((m, d), BF16), side_shape],
        scratch_shapes=[pltpu.VMEM((d, bn), BF16), pltpu.VMEM((d, bn), BF16),
                        pltpu.VMEM((kc, bn), BF16), pltpu.VMEM((ka, bn), BF16)],
        compiler_params=_compiler_params(
            2,
            _nbytes((bm, d + kc + ka), BF16) + _nbytes((2 * d + kc + ka, bn), F32)
            + _nbytes((bm, bn), BF16) + side_bytes,
            _nbytes((2 * d + kc + ka, bn), BF16) + 10 * _nbytes((MERGE_ROW_CHUNK, bn), F32)),
        name="gated_merge",
    )(h, yconv, attn, w_in, w_in, w_out_conv, w_out_attn, w_next)


def _rope_tables(seq):
    half = ROT_DIM // 2
    inv_freq = 1.0 / (ROPE_THETA ** (jnp.arange(0, ROT_DIM, 2, dtype=F32) / ROT_DIM))
    ang = jnp.arange(seq, dtype=F32)[:, None] * inv_freq[None, :]
    cos_sin = jnp.concatenate([jnp.cos(ang), jnp.sin(ang)], axis=1)
    spread = np.zeros((2 * half, 3 * LANES), np.float32)
    base = np.zeros((1, 3 * LANES), np.float32)
    for lane in range(3 * LANES):
        kind, dim = lane // LANES, lane % HEAD_DIM
        if kind == 0 and dim < ROT_DIM:
            spread[dim % half, lane] = 1.0
        elif kind == 0:
            base[0, lane] = 1.0
        elif kind == 1 and dim < half:
            spread[half + dim, lane] = -1.0
        elif kind == 2 and half <= dim < ROT_DIM:
            spread[half + dim - half, lane] = 1.0
    return jnp.dot(cos_sin, spread, precision=lax.Precision.HIGHEST) + base


def _swiglu_half_step(x, h, w_gu, w_down, next_gain):
    act, w_down_bf = _gate_up(h, w_gu, w_down)
    return _matmul_residual(act, w_down_bf, x, 0.5, next_gain)


def kernel(x, g_ffn1, w_gu1, w_down1, g_mix, w_in, conv_w, q_norm_g, k_norm_g, sinks,
           w_out_conv, w_out_attn, w_o, g_ffn2, w_gu2, w_down2):
    batch, seq, d = x.shape
    depth = w_in.shape[0]
    conv_width = conv_w.shape[2]
    q_width = w_out_attn.shape[1]
    kv_width = q_width // GROUP
    q_col = 3 * conv_width
    gate_col = q_col + q_width + 2 * kv_width
    rope = _rope_tables(seq)

    xf = x.reshape(batch * seq, d)
    h = _rmsnorm(xf, g_ffn1[0])
    for l in range(depth):
        xf, h = _swiglu_half_step(xf, h, w_gu1[l], w_down1[l], g_mix[l])

        yconv = _conv_branch(h, w_in[l], conv_w[l], batch, seq, conv_width)
        qkv_bn = 2 * kv_width
        assert q_width % qkv_bn == 0
        gains = jnp.concatenate(
            [jnp.tile(q_norm_g[l] * ATTN_SCALE, q_width // HEAD_DIM),
             jnp.tile(k_norm_g[l], kv_width // HEAD_DIM), jnp.ones((kv_width,), F32)]
        ).reshape(-1, 1, qkv_bn)
        raw_cols = jnp.concatenate(
            [jnp.zeros((q_width + kv_width,), F32), jnp.ones((kv_width,), F32)]
        ).reshape(-1, 1, qkv_bn)
        qkv = _qkv_proj(h, w_in[l], gains, raw_cols, rope, batch, seq, q_col, bn=qkv_bn)
        attn = _attention(qkv, sinks[l], batch, seq, q_width, kv_width)
        merged, w_o_bf = _merge(h, yconv, attn, w_in[l], w_out_conv[l], w_out_attn[l], w_o[l], gate_col)
        xf, h = _matmul_residual(merged, w_o_bf, xf, 1.0, g_ffn2[l])

        next_gain = g_ffn1[l + 1] if l + 1 < depth else None
        xf, h = _swiglu_half_step(xf, h, w_gu2[l], w_down2[l], next_gain)
    return xf.reshape(batch, seq, d)
```

```python
import functools

import jax
import jax.numpy as jnp
import numpy as np
from jax import lax
from jax.experimental import pallas as pl
from jax.experimental.pallas import tpu as pltpu

F32 = jnp.float32
BF16 = jnp.bfloat16

HEAD_DIM = 64
GROUP = 4
CONV_K = 3
WINDOW = 128
ROT_DIM = HEAD_DIM // 4
ROPE_THETA = 500000.0
RMS_EPS = 1e-6
ATTN_SCALE = HEAD_DIM ** -0.5
NEG_INF = -1e30

LANES = 128
SUBLANES = 8
MXU_DIM = 256
VMEM_CAP_V7X = 60 * 1024 * 1024


def _nbytes(shape, dtype):
    n = 1
    for s in shape:
        n *= s
    return n * jnp.dtype(dtype).itemsize


def _compiler_params(n_grid_axes, pipelined_bytes, resident_bytes):
    need = 2 * pipelined_bytes + resident_bytes
    return pltpu.CompilerParams(
        dimension_semantics=("arbitrary",) * n_grid_axes,
        vmem_limit_bytes=min(VMEM_CAP_V7X, need + need // 8),
    )


def _dot(a, b):
    return jnp.dot(a, b, preferred_element_type=F32)


ROW_CHUNK = 512


def _row_chunks(n_rows, chunk=ROW_CHUNK):
    chunk = min(chunk, n_rows)
    assert n_rows % chunk == 0
    return [pl.ds(r * chunk, chunk) for r in range(n_rows // chunk)]


def _rmsnorm_kernel(x_ref, g_ref, o_ref):
    x = x_ref[...]
    ms = jnp.mean(x * x, axis=-1, keepdims=True)
    o_ref[...] = (x * lax.rsqrt(ms + RMS_EPS) * g_ref[...]).astype(o_ref.dtype)


def _rmsnorm(x, g, bm=1024):
    m, d = x.shape
    return pl.pallas_call(
        _rmsnorm_kernel,
        grid=(m // bm,),
        in_specs=[pl.BlockSpec((bm, d), lambda i: (i, 0)),
                  pl.BlockSpec((1, d), lambda i: (0, 0))],
        out_specs=pl.BlockSpec((bm, d), lambda i: (i, 0)),
        out_shape=pltpu.HBM((m, d), BF16),
        compiler_params=_compiler_params(
            1, _nbytes((bm, d), F32) + _nbytes((bm, d), BF16), 2 * _nbytes((bm, d), F32)),
        name="rmsnorm",
    )(x, g.reshape(1, d))


def _side_cast_specs(w_next, n_steps, step_index):
    k, n = w_next.shape
    assert k % n_steps == 0 and (k // n_steps) % (2 * SUBLANES) == 0
    rows = k // n_steps
    spec = pl.BlockSpec((rows, n), lambda *ids: (step_index(*ids), 0))
    return spec, spec, pltpu.HBM((k, n), BF16), _nbytes((rows, n), F32) + _nbytes((rows, n), BF16)


def _gate_up_kernel(h_ref, wg_ref, wu_ref, wnext_ref, o_ref, wnext_bf_ref, wg_bf, wu_bf):
    @pl.when(pl.program_id(1) == 0)
    def _():
        wg_bf[...] = wg_ref[...].astype(BF16)
        wu_bf[...] = wu_ref[...].astype(BF16)

    wnext_bf_ref[...] = wnext_ref[...].astype(BF16)
    for rows in _row_chunks(h_ref.shape[0]):
        h = h_ref[rows, :]
        g = _dot(h, wg_bf[...])
        u = _dot(h, wu_bf[...])
        o_ref[rows, :] = (g * jax.nn.sigmoid(g) * u).astype(o_ref.dtype)


def _gate_up(h, w_gu, w_next, bm=2048, bn=512):
    m, d = h.shape
    f = w_gu.shape[1] // 2
    nj, ni = f // bn, m // bm
    side_in, side_out, side_shape, side_bytes = _side_cast_specs(w_next, nj * ni, lambda j, i: j * ni + i)
    return pl.pallas_call(
        _gate_up_kernel,
        grid=(nj, ni),
        in_specs=[pl.BlockSpec((bm, d), lambda j, i: (i, 0)),
                  pl.BlockSpec((d, bn), lambda j, i: (0, j)),
                  pl.BlockSpec((d, bn), lambda j, i: (0, j + nj)),
                  side_in],
        out_specs=[pl.BlockSpec((bm, bn), lambda j, i: (i, j)), side_out],
        out_shape=[pltpu.HBM((m, f), BF16), side_shape],
        scratch_shapes=[pltpu.VMEM((d, bn), BF16), pltpu.VMEM((d, bn), BF16)],
        compiler_params=_compiler_params(
            2,
            _nbytes((bm, d), BF16) + 2 * _nbytes((d, bn), F32) + _nbytes((bm, bn), BF16) + side_bytes,
            2 * _nbytes((d, bn), BF16) + 6 * _nbytes((ROW_CHUNK, bn), F32)),
        name="gate_up",
    )(h, w_gu, w_gu, w_next)


RESIDENT_CHUNKS = 2


def _matmul_residual_kernel(*refs, scale, with_norm):
    if with_norm:
        a_ref, w_ref, r_ref, g_ref, x_ref, h_ref = refs
    else:
        a_ref, w_ref, r_ref, x_ref = refs
    for rows in _row_chunks(a_ref.shape[0], a_ref.shape[0] // RESIDENT_CHUNKS):
        x = r_ref[rows, :] + scale * _dot(a_ref[rows, :], w_ref[...])
        x_ref[rows, :] = x
        if with_norm:
            ms = jnp.mean(x * x, axis=-1, keepdims=True)
            h_ref[rows, :] = (x * lax.rsqrt(ms + RMS_EPS) * g_ref[...]).astype(h_ref.dtype)


def _matmul_residual(a, w_bf, res, scale, norm_gain=None):
    m, k = a.shape
    n = w_bf.shape[1]
    with_norm = norm_gain is not None
    def resident_bytes(rows):
        return _nbytes((k, n), BF16) + 6 * _nbytes((rows // RESIDENT_CHUNKS, n), F32)

    def pipelined_bytes(rows):
        return (_nbytes((rows, k), BF16) + 2 * _nbytes((rows, n), F32)
                + (_nbytes((rows, n), BF16) if with_norm else 0))

    bm = next(rows for rows in (1024, 512, 256, 128)
              if m % rows == 0
              and (2 * pipelined_bytes(rows) + resident_bytes(rows)) * 9 // 8 <= VMEM_CAP_V7X)
    row_spec = lambda width: pl.BlockSpec((bm, width), lambda i: (i, 0))
    in_specs = [row_spec(k),
                pl.BlockSpec((k, n), lambda i: (0, 0), pipeline_mode=pl.Buffered(1)),
                row_spec(n)]
    operands = [a, pltpu.with_memory_space_constraint(w_bf, pltpu.HBM), res]
    out_specs = [row_spec(n)]
    out_shape = [pltpu.HBM((m, n), F32)]
    if with_norm:
        in_specs.append(pl.BlockSpec((1, n), lambda i: (0, 0)))
        operands.append(norm_gain.reshape(1, n))
        out_specs.append(row_spec(n))
        out_shape.append(pltpu.HBM((m, n), BF16))
    outs = pl.pallas_call(
        functools.partial(_matmul_residual_kernel, scale=scale, with_norm=with_norm),
        grid=(m // bm,),
        in_specs=in_specs,
        out_specs=out_specs,
        out_shape=out_shape,
        compiler_params=_compiler_params(1, pipelined_bytes(bm), resident_bytes(bm)),
        name="matmul_residual",
    )(*operands)
    return outs if with_norm else (outs[0], None)


def _conv_branch_kernel(h_ref, wx_ref, wb_ref, wc_ref, cw_ref, o_ref, wx_bf, wb_bf, wc_bf, u_pad):
    seq = h_ref.shape[0]

    @pl.when(pl.program_id(1) == 0)
    def _():
        wx_bf[...] = wx_ref[...].astype(BF16)
        wb_bf[...] = wb_ref[...].astype(BF16)
        wc_bf[...] = wc_ref[...].astype(BF16)
        u_pad[0:SUBLANES, :] = jnp.zeros((SUBLANES, u_pad.shape[1]), F32)

    cw = cw_ref[...]
    rc = min(ROW_CHUNK, seq)
    for r0 in range(0, seq, rc):
        h = h_ref[r0:r0 + rc, :]
        u = _dot(h, wc_bf[...]) * _dot(h, wx_bf[...])
        p0 = SUBLANES + r0
        u_pad[p0:p0 + rc, :] = u
        y = (cw[0:1, :] * u_pad[p0 - 2:p0 - 2 + rc, :]
             + cw[1:2, :] * u_pad[p0 - 1:p0 - 1 + rc, :])
        y = y + cw[2:3, :] * u
        o_ref[r0:r0 + rc, :] = (_dot(h, wb_bf[...]) * y).astype(o_ref.dtype)


def _conv_branch(h, w_in, conv_w, batch, seq, width, bn=256):
    m, d = h.shape
    nj = width // bn
    return pl.pallas_call(
        _conv_branch_kernel,
        grid=(nj, batch),
        in_specs=[pl.BlockSpec((seq, d), lambda j, b: (b, 0)),
                  pl.BlockSpec((d, bn), lambda j, b: (0, j)),
                  pl.BlockSpec((d, bn), lambda j, b: (0, j + nj)),
                  pl.BlockSpec((d, bn), lambda j, b: (0, j + 2 * nj)),
                  pl.BlockSpec((CONV_K, bn), lambda j, b: (0, j))],
        out_specs=pl.BlockSpec((seq, bn), lambda j, b: (b, j)),
        out_shape=pltpu.HBM((m, width), BF16),
        scratch_shapes=[pltpu.VMEM((d, bn), BF16)] * 3
        + [pltpu.VMEM((SUBLANES + seq, bn), F32)],
        compiler_params=_compiler_params(
            2,
            _nbytes((seq, d), BF16) + 3 * _nbytes((d, bn), F32) + _nbytes((seq, bn), BF16),
            3 * _nbytes((d, bn), BF16) + _nbytes((seq, bn), F32) + 8 * _nbytes((ROW_CHUNK, bn), F32)),
        name="conv_branch",
    )(h, w_in, w_in, w_in, conv_w)


def _head_norm_rope(x, gain, rope):
    width = x.shape[1]
    cos, sin_hi, sin_lo = (rope[:, i * LANES:(i + 1) * LANES] for i in range(3))
    r = lax.broadcasted_iota(jnp.int32, (MXU_DIM, MXU_DIM), 0) // HEAD_DIM
    c = lax.broadcasted_iota(jnp.int32, (MXU_DIM, MXU_DIM), 1) // HEAD_DIM
    ones_blockdiag = (r == c).astype(BF16)
    out = []
    for t in range(width // MXU_DIM):
        xt = x[:, t * MXU_DIM:(t + 1) * MXU_DIM]
        ss = _dot((xt * xt).astype(BF16), ones_blockdiag)
        y = xt * lax.rsqrt(ss * (1.0 / HEAD_DIM) + RMS_EPS) * gain[:, t * MXU_DIM:(t + 1) * MXU_DIM]
        for s in range(MXU_DIM // LANES):
            ys = y[:, s * LANES:(s + 1) * LANES]
            half = ROT_DIM // 2
            rot = (ys * cos
                   + pltpu.roll(ys, LANES - half, 1) * sin_hi
                   + pltpu.roll(ys, half, 1) * sin_lo)
            out.append(rot)
    return jnp.concatenate(out, axis=1)


def _qkv_kernel(h_ref, w_ref, gain_ref, raw_ref, rope_ref, o_ref, w_bf):
    @pl.when(pl.program_id(1) == 0)
    def _():
        w_bf[...] = w_ref[...].astype(BF16)

    gain = gain_ref[...]
    is_raw = raw_ref[...] != 0.0
    seq = h_ref.shape[0]
    rc = min(ROW_CHUNK, seq)
    for r0 in range(0, seq, rc):
        acc = _dot(h_ref[r0:r0 + rc, :], w_bf[...])
        y = _head_norm_rope(acc, gain, rope_ref[r0:r0 + rc, :])
        o_ref[r0:r0 + rc, :] = jnp.where(is_raw, acc, y).astype(o_ref.dtype)


def _qkv_proj(h, w_in, gains, raw_cols, rope, batch, seq, q_col, bn):
    m, d = h.shape
    assert q_col % bn == 0
    nj = gains.shape[0]
    first = q_col // bn
    table_spec = pl.BlockSpec((seq, 3 * LANES), lambda j, b: (0, 0))
    col_spec = pl.BlockSpec((None, 1, bn), lambda j, b: (j, 0, 0))
    return pl.pallas_call(
        _qkv_kernel,
        grid=(nj, batch),
        in_specs=[pl.BlockSpec((seq, d), lambda j, b: (b, 0)),
                  pl.BlockSpec((d, bn), lambda j, b: (0, first + j)),
                  col_spec, col_spec, table_spec],
        out_specs=pl.BlockSpec((seq, bn), lambda j, b: (b, j)),
        out_shape=pltpu.HBM((m, nj * bn), BF16),
        scratch_shapes=[pltpu.VMEM((d, bn), BF16)],
        compiler_params=_compiler_params(
            2,
            _nbytes((seq, d), BF16) + _nbytes((d, bn), F32) + _nbytes((seq, bn), BF16)
            + _nbytes((seq, 3 * LANES), F32),
            _nbytes((d, bn), BF16) + 10 * _nbytes((ROW_CHUNK, bn), F32)),
        name="qkv_proj",
    )(h, w_in, gains, raw_cols, rope)


ATTN_BLOCKS_PER_ITER = 2


def _attention_kernel(sink_ref, q_ref, kv_ref, o_ref, k_lo, k_hi, v_lo, v_hi, bias, s_scr, p_scr,
                      *, n_kv_heads):
    seq = q_ref.shape[0]
    kw = n_kv_heads * HEAD_DIM
    two_w = 2 * WINDOW
    assert LANES == 2 * HEAD_DIM and GROUP == 4

    lane = lax.broadcasted_iota(jnp.int32, (seq, LANES), 1)
    zero_block = jnp.zeros((WINDOW, LANES), BF16)
    for src0, lo_ref, hi_ref in ((0, k_lo, k_hi), (kw, v_lo, v_hi)):
        for grp in range(kw // LANES):
            x = kv_ref[:, src0 + grp * LANES:src0 + (grp + 1) * LANES].astype(F32)
            swapped = pltpu.roll(x, HEAD_DIM, 1)
            even_lo, odd_lo = jnp.where(lane < HEAD_DIM, x, 0.0), jnp.where(lane < HEAD_DIM, swapped, 0.0)
            even_hi, odd_hi = jnp.where(lane >= HEAD_DIM, swapped, 0.0), jnp.where(lane >= HEAD_DIM, x, 0.0)
            for hk, lo, hi in ((2 * grp, even_lo, even_hi), (2 * grp + 1, odd_lo, odd_hi)):
                lo_ref[hk, 0:WINDOW, :] = zero_block
                hi_ref[hk, 0:WINDOW, :] = zero_block
                lo_ref[hk, WINDOW:WINDOW + seq, :] = lo.astype(BF16)
                hi_ref[hk, WINDOW:WINDOW + seq, :] = hi.astype(BF16)

    @pl.when(pl.program_id(0) == 0)
    def _():
        key = lax.broadcasted_iota(jnp.int32, (2 * two_w, two_w), 0) % two_w
        qry = lax.broadcasted_iota(jnp.int32, (2 * two_w, two_w), 1) % WINDOW
        diff = qry + WINDOW - key
        in_window = (diff >= 0) & (diff < WINDOW)
        bias[0] = jnp.where(in_window & (key >= WINDOW), 0.0, NEG_INF)
        bias[1] = jnp.where(in_window, 0.0, NEG_INF)

    first_pair = lax.broadcasted_iota(jnp.int32, (1, two_w), 1) < WINDOW

    n_slabs = s_scr.shape[0]
    blocks_per_iter = n_slabs // n_kv_heads

    def blocks(i, carry):
        items = [(u * n_kv_heads + hk, hk, i * blocks_per_iter + u)
                 for u in range(blocks_per_iter) for hk in range(n_kv_heads)]
        for slab, hk, n in items:
            r0 = pl.multiple_of(n * WINDOW, WINDOW)
            c0 = hk * GROUP * HEAD_DIM
            q_pairs = jnp.concatenate([q_ref[pl.ds(r0, WINDOW), c0:c0 + LANES],
                                       q_ref[pl.ds(r0, WINDOW), c0 + LANES:c0 + 2 * LANES]], axis=0)
            keys = jnp.concatenate([k_lo[hk, pl.ds(r0, two_w), :], k_hi[hk, pl.ds(r0, two_w), :]], axis=0)
            s_scr[slab] = lax.dot_general(keys, q_pairs, (((1,), (1,)), ((), ())), preferred_element_type=F32)
        inv = []
        for slab, hk, n in items:
            t = jnp.minimum(n, 1)
            for half in range(2):
                rows = slice(half * two_w, (half + 1) * two_w)
                s_h = s_scr[slab, rows, :] + bias[t, rows, :]
                sink = jnp.where(first_pair, sink_ref[hk * GROUP + half], sink_ref[hk * GROUP + 2 + half])
                mx = jnp.maximum(jnp.max(s_h, axis=0, keepdims=True), sink)
                p = jnp.exp(s_h - mx)
                inv.append(1.0 / (jnp.sum(p, axis=0, keepdims=True) + jnp.exp(sink - mx)))
                p_scr[slab, rows, :] = p.astype(BF16)
        for slab, hk, n in items:
            r0 = pl.multiple_of(n * WINDOW, WINDOW)
            c0 = hk * GROUP * HEAD_DIM
            vals = jnp.concatenate([v_lo[hk, pl.ds(r0, two_w), :], v_hi[hk, pl.ds(r0, two_w), :]], axis=0)
            o_t = lax.dot_general(vals, p_scr[slab], (((0,), (0,)), ((), ())), preferred_element_type=F32)
            o_t = jnp.concatenate([o_t[0:HEAD_DIM, :] * inv[2 * slab],
                                   o_t[HEAD_DIM:LANES, :] * inv[2 * slab + 1]], axis=0)
            for pair in range(2):
                o_ref[pl.ds(r0, WINDOW), c0 + pair * LANES:c0 + (pair + 1) * LANES] = (
                    o_t[:, pair * WINDOW:(pair + 1) * WINDOW].T.astype(o_ref.dtype))
        return carry

    assert (seq // WINDOW) % blocks_per_iter == 0
    lax.fori_loop(0, seq // WINDOW // blocks_per_iter, blocks, 0)


def _attention(qkv, sinks, batch, seq, q_width, kv_width):
    m = qkv.shape[0]
    assert q_width % (2 * kv_width) == 0
    n_kv_heads = kv_width // HEAD_DIM
    n_slabs = ATTN_BLOCKS_PER_ITER * n_kv_heads
    return pl.pallas_call(
        functools.partial(_attention_kernel, n_kv_heads=n_kv_heads),
        grid=(batch,),
        in_specs=[pl.BlockSpec(memory_space=pltpu.SMEM),
                  pl.BlockSpec((seq, q_width), lambda b: (b, 0)),
                  pl.BlockSpec((seq, 2 * kv_width), lambda b: (b, q_width // (2 * kv_width)))],
        out_specs=pl.BlockSpec((seq, q_width), lambda b: (b, 0)),
        out_shape=pltpu.HBM((m, q_width), BF16),
        scratch_shapes=[pltpu.VMEM((n_kv_heads, WINDOW + seq, LANES), BF16)] * 4
        + [pltpu.VMEM((2, 4 * WINDOW, 2 * WINDOW), F32),
           pltpu.VMEM((n_slabs, 4 * WINDOW, 2 * WINDOW), F32),
           pltpu.VMEM((n_slabs, 4 * WINDOW, 2 * WINDOW), BF16)],
        compiler_params=_compiler_params(
            1,
            2 * _nbytes((seq, q_width), BF16) + _nbytes((seq, 2 * kv_width), BF16),
            4 * _nbytes((n_kv_heads, WINDOW + seq, LANES), BF16)
            + (8 + 2 * n_slabs) * _nbytes((4 * WINDOW, 2 * WINDOW), F32)
            + 4 * _nbytes((seq, LANES), F32)),
        name="swa_attention",
    )(sinks, qkv, qkv)


MERGE_ROW_CHUNK = 128


def _merge_kernel(h_ref, yc_ref, at_ref, wga_ref, wgb_ref, woc_ref, woa_ref, wnext_ref,
                  o_ref, wnext_bf_ref, wga_bf, wgb_bf, woc_bf, woa_bf):
    @pl.when(pl.program_id(1) == 0)
    def _():
        wga_bf[...] = wga_ref[...].astype(BF16)
        wgb_bf[...] = wgb_ref[...].astype(BF16)
        woc_bf[...] = woc_ref[...].astype(BF16)
        woa_bf[...] = woa_ref[...].astype(BF16)

    wnext_bf_ref[...] = wnext_ref[...].astype(BF16)
    for rows in _row_chunks(h_ref.shape[0], MERGE_ROW_CHUNK):
        h = h_ref[rows, :]
        ya = _dot(yc_ref[rows, :], woc_bf[...])
        yb = _dot(at_ref[rows, :], woa_bf[...])
        ga = _dot(h, wga_bf[...])
        gb = _dot(h, wgb_bf[...])
        o_ref[rows, :] = (jax.nn.sigmoid(ga) * ya + jax.nn.sigmoid(gb) * yb).astype(o_ref.dtype)


def _merge(h, yconv, attn, w_in, w_out_conv, w_out_attn, w_next, gate_col, bm=512, bn=512):
    m, d = h.shape
    kc, ka = yconv.shape[1], attn.shape[1]
    assert gate_col % bn == 0
    ga0 = gate_col // bn
    nj, ni = d // bn, m // bm
    side_in, side_out, side_shape, side_bytes = _side_cast_specs(w_next, nj * ni, lambda j, i: j * ni + i)
    return pl.pallas_call(
        _merge_kernel,
        grid=(nj, ni),
        in_specs=[pl.BlockSpec((bm, d), lambda j, i: (i, 0)),
                  pl.BlockSpec((bm, kc), lambda j, i: (i, 0)),
                  pl.BlockSpec((bm, ka), lambda j, i: (i, 0)),
                  pl.BlockSpec((d, bn), lambda j, i: (0, ga0 + j)),
                  pl.BlockSpec((d, bn), lambda j, i: (0, ga0 + nj + j)),
                  pl.BlockSpec((kc, bn), lambda j, i: (0, j)),
                  pl.BlockSpec((ka, bn), lambda j, i: (0, j)),
                  side_in],
        out_specs=[pl.BlockSpec((bm, bn), lambda j, i: (i, j)), side_out],
        out_shape=[pltpu.HBM((m, d), BF16), side_shape],
        scratch_shapes=[pltpu.VMEM((d, bn), BF16), pltpu.VMEM((d, bn), BF16),
                        pltpu.VMEM((kc, bn), BF16), pltpu.VMEM((ka, bn), BF16)],
        compiler_params=_compiler_params(
            2,
            _nbytes((bm, d + kc + ka), BF16) + _nbytes((2 * d + kc + ka, bn), F32)
            + _nbytes((bm, bn), BF16) + side_bytes,
            _nbytes((2 * d + kc + ka, bn), BF16) + 10 * _nbytes((MERGE_ROW_CHUNK, bn), F32)),
        name="gated_merge",
    )(h, yconv, attn, w_in, w_in, w_out_conv, w_out_attn, w_next)


def _rope_tables(seq):
    half = ROT_DIM // 2
    inv_freq = (1.0 / (np.float32(ROPE_THETA) ** (np.arange(0, ROT_DIM, 2, dtype=np.float32) / ROT_DIM))
                ).astype(np.float32)
    ang = (np.arange(seq, dtype=np.float32)[:, None] * inv_freq[None, :]).astype(np.float64)
    cos, sin = np.cos(ang), np.sin(ang)
    ones = np.ones((seq, HEAD_DIM - ROT_DIM))
    zeros, zeros_tail = np.zeros((seq, half)), np.zeros((seq, HEAD_DIM - ROT_DIM))
    per_head = [np.concatenate([cos, cos, ones], axis=1),
                np.concatenate([-sin, zeros, zeros_tail], axis=1),
                np.concatenate([zeros, sin, zeros_tail], axis=1)]
    table = np.concatenate([np.tile(t, (1, LANES // HEAD_DIM)) for t in per_head], axis=1)
    return jnp.asarray(table.astype(np.float32))


def _swiglu_half_step(x, h, w_gu, w_down, next_gain):
    act, w_down_bf = _gate_up(h, w_gu, w_down)
    return _matmul_residual(act, w_down_bf, x, 0.5, next_gain)


def kernel(x, g_ffn1, w_gu1, w_down1, g_mix, w_in, conv_w, q_norm_g, k_norm_g, sinks,
           w_out_conv, w_out_attn, w_o, g_ffn2, w_gu2, w_down2):
    batch, seq, d = x.shape
    depth = w_in.shape[0]
    conv_width = conv_w.shape[2]
    q_width = w_out_attn.shape[1]
    kv_width = q_width // GROUP
    q_col = 3 * conv_width
    gate_col = q_col + q_width + 2 * kv_width
    rope = _rope_tables(seq)

    xf = x.reshape(batch * seq, d)
    h = _rmsnorm(xf, g_ffn1[0])
    for l in range(depth):
        xf, h = _swiglu_half_step(xf, h, w_gu1[l], w_down1[l], g_mix[l])

        yconv = _conv_branch(h, w_in[l], conv_w[l], batch, seq, conv_width)
        qkv_bn = 2 * kv_width
        assert q_width % qkv_bn == 0
        gains = jnp.concatenate(
            [jnp.tile(q_norm_g[l] * ATTN_SCALE, q_width // HEAD_DIM),
             jnp.tile(k_norm_g[l], kv_width // HEAD_DIM), jnp.ones((kv_width,), F32)]
        ).reshape(-1, 1, qkv_bn)
        raw_cols = jnp.concatenate(
            [jnp.zeros((q_width + kv_width,), F32), jnp.ones((kv_width,), F32)]
        ).reshape(-1, 1, qkv_bn)
        qkv = _qkv_proj(h, w_in[l], gains, raw_cols, rope, batch, seq, q_col, bn=qkv_bn)
        attn = _attention(qkv, sinks[l], batch, seq, q_width, kv_width)
        merged, w_o_bf = _merge(h, yconv, attn, w_in[l], w_out_conv[l], w_out_attn[l], w_o[l], gate_col)
        xf, h = _matmul_residual(merged, w_o_bf, xf, 1.0, g_ffn2[l])

        next_gain = g_ffn1[l + 1] if l + 1 < depth else None
        xf, h = _swiglu_half_step(xf, h, w_gu2[l], w_down2[l], next_gain)
    return xf.reshape(batch, seq, d)
```

```python
import functools

import jax
import jax.numpy as jnp
import numpy as np
from jax import lax
from jax.experimental import pallas as pl
from jax.experimental.pallas import tpu as pltpu

F32 = jnp.float32
BF16 = jnp.bfloat16

HEAD_DIM = 64
GROUP = 4
CONV_K = 3
WINDOW = 128
ROT_DIM = HEAD_DIM // 4
ROPE_THETA = 500000.0
RMS_EPS = 1e-6
ATTN_SCALE = HEAD_DIM ** -0.5
LOG2_E = 1.4426950408889634
NEG_INF = -1e30

LANES = 128
SUBLANES = 8
MXU_DIM = 256
VMEM_CAP_V7X = 60 * 1024 * 1024


def _nbytes(shape, dtype):
    n = 1
    for s in shape:
        n *= s
    return n * jnp.dtype(dtype).itemsize


def _compiler_params(n_grid_axes, pipelined_bytes, resident_bytes):
    need = 2 * pipelined_bytes + resident_bytes
    return pltpu.CompilerParams(
        dimension_semantics=("arbitrary",) * n_grid_axes,
        vmem_limit_bytes=min(VMEM_CAP_V7X, need + need // 8),
    )


def _dot(a, b):
    return jnp.dot(a, b, preferred_element_type=F32)


ROW_CHUNK = 512


def _row_chunks(n_rows, chunk=ROW_CHUNK):
    chunk = min(chunk, n_rows)
    assert n_rows % chunk == 0
    return [pl.ds(r * chunk, chunk) for r in range(n_rows // chunk)]


def _rmsnorm_kernel(x_ref, g_ref, o_ref):
    x = x_ref[...]
    ms = jnp.mean(x * x, axis=-1, keepdims=True)
    o_ref[...] = (x * lax.rsqrt(ms + RMS_EPS) * g_ref[...]).astype(o_ref.dtype)


def _rmsnorm(x, g, bm=1024):
    m, d = x.shape
    return pl.pallas_call(
        _rmsnorm_kernel,
        grid=(m // bm,),
        in_specs=[pl.BlockSpec((bm, d), lambda i: (i, 0)),
                  pl.BlockSpec((1, d), lambda i: (0, 0))],
        out_specs=pl.BlockSpec((bm, d), lambda i: (i, 0)),
        out_shape=pltpu.HBM((m, d), BF16),
        compiler_params=_compiler_params(
            1, _nbytes((bm, d), F32) + _nbytes((bm, d), BF16), 2 * _nbytes((bm, d), F32)),
        name="rmsnorm",
    )(x, g.reshape(1, d))


def _side_cast_specs(w_next, n_steps, step_index):
    k, n = w_next.shape
    assert k % n_steps == 0 and (k // n_steps) % (2 * SUBLANES) == 0
    rows = k // n_steps
    spec = pl.BlockSpec((rows, n), lambda *ids: (step_index(*ids), 0))
    return spec, spec, pltpu.HBM((k, n), BF16), _nbytes((rows, n), F32) + _nbytes((rows, n), BF16)


def _gate_up_kernel(h_ref, wg_ref, wu_ref, wnext_ref, o_ref, wnext_bf_ref, wgu_bf):
    bn = wg_ref.shape[1]

    @pl.when(pl.program_id(1) == 0)
    def _():
        wgu_bf[:, :bn] = wg_ref[...].astype(BF16)
        wgu_bf[:, bn:] = wu_ref[...].astype(BF16)

    wnext_bf_ref[...] = wnext_ref[...].astype(BF16)
    for rows in _row_chunks(h_ref.shape[0]):
        gu = _dot(h_ref[rows, :], wgu_bf[...])
        g, u = gu[:, :bn], gu[:, bn:]
        o_ref[rows, :] = (g * jax.nn.sigmoid(g) * u).astype(o_ref.dtype)


def _gate_up(h, w_gu, w_next, bm=2048, bn=512):
    m, d = h.shape
    f = w_gu.shape[1] // 2
    nj, ni = f // bn, m // bm
    side_in, side_out, side_shape, side_bytes = _side_cast_specs(w_next, nj * ni, lambda j, i: j * ni + i)
    return pl.pallas_call(
        _gate_up_kernel,
        grid=(nj, ni),
        in_specs=[pl.BlockSpec((bm, d), lambda j, i: (i, 0)),
                  pl.BlockSpec((d, bn), lambda j, i: (0, j)),
                  pl.BlockSpec((d, bn), lambda j, i: (0, j + nj)),
                  side_in],
        out_specs=[pl.BlockSpec((bm, bn), lambda j, i: (i, j)), side_out],
        out_shape=[pltpu.HBM((m, f), BF16), side_shape],
        scratch_shapes=[pltpu.VMEM((d, 2 * bn), BF16)],
        compiler_params=_compiler_params(
            2,
            _nbytes((bm, d), BF16) + 2 * _nbytes((d, bn), F32) + _nbytes((bm, bn), BF16) + side_bytes,
            2 * _nbytes((d, bn), BF16) + 6 * _nbytes((ROW_CHUNK, bn), F32)),
        name="gate_up",
    )(h, w_gu, w_gu, w_next)


RESIDENT_CHUNKS = 2


def _matmul_residual_kernel(*refs, scale, with_norm):
    if with_norm:
        a_ref, w_ref, r_ref, g_ref, x_ref, h_ref = refs
    else:
        a_ref, w_ref, r_ref, x_ref = refs
    for rows in _row_chunks(a_ref.shape[0], a_ref.shape[0] // RESIDENT_CHUNKS):
        x = r_ref[rows, :] + scale * _dot(a_ref[rows, :], w_ref[...])
        x_ref[rows, :] = x
        if with_norm:
            ms = jnp.mean(x * x, axis=-1, keepdims=True)
            h_ref[rows, :] = (x * lax.rsqrt(ms + RMS_EPS) * g_ref[...]).astype(h_ref.dtype)


def _matmul_residual(a, w_bf, res, scale, norm_gain=None):
    m, k = a.shape
    n = w_bf.shape[1]
    with_norm = norm_gain is not None
    def resident_bytes(rows):
        return _nbytes((k, n), BF16) + 6 * _nbytes((rows // RESIDENT_CHUNKS, n), F32)

    def pipelined_bytes(rows):
        return (_nbytes((rows, k), BF16) + 2 * _nbytes((rows, n), F32)
                + (_nbytes((rows, n), BF16) if with_norm else 0))

    bm = next(rows for rows in (1024, 512, 256, 128)
              if m % rows == 0
              and (2 * pipelined_bytes(rows) + resident_bytes(rows)) * 9 // 8 <= VMEM_CAP_V7X)
    row_spec = lambda width: pl.BlockSpec((bm, width), lambda i: (i, 0))
    in_specs = [row_spec(k),
                pl.BlockSpec((k, n), lambda i: (0, 0), pipeline_mode=pl.Buffered(1)),
                row_spec(n)]
    operands = [a, pltpu.with_memory_space_constraint(w_bf, pltpu.HBM), res]
    out_specs = [row_spec(n)]
    out_shape = [pltpu.HBM((m, n), F32)]
    if with_norm:
        in_specs.append(pl.BlockSpec((1, n), lambda i: (0, 0)))
        operands.append(norm_gain.reshape(1, n))
        out_specs.append(row_spec(n))
        out_shape.append(pltpu.HBM((m, n), BF16))
    outs = pl.pallas_call(
        functools.partial(_matmul_residual_kernel, scale=scale, with_norm=with_norm),
        grid=(m // bm,),
        in_specs=in_specs,
        out_specs=out_specs,
        out_shape=out_shape,
        compiler_params=_compiler_params(1, pipelined_bytes(bm), resident_bytes(bm)),
        name="matmul_residual",
    )(*operands)
    return outs if with_norm else (outs[0], None)


def _conv_branch_kernel(h_ref, wx_ref, wb_ref, wc_ref, cw_ref, o_ref, wx_bf, wb_bf, wc_bf, u_pad):
    seq = h_ref.shape[0]

    @pl.when(pl.program_id(1) == 0)
    def _():
        wx_bf[...] = wx_ref[...].astype(BF16)
        wb_bf[...] = wb_ref[...].astype(BF16)
        wc_bf[...] = wc_ref[...].astype(BF16)
        u_pad[0:SUBLANES, :] = jnp.zeros((SUBLANES, u_pad.shape[1]), F32)

    cw = cw_ref[...]
    rc = min(ROW_CHUNK, seq)
    for r0 in range(0, seq, rc):
        h = h_ref[r0:r0 + rc, :]
        u = _dot(h, wc_bf[...]) * _dot(h, wx_bf[...])
        p0 = SUBLANES + r0
        u_pad[p0:p0 + rc, :] = u
        y = (cw[0:1, :] * u_pad[p0 - 2:p0 - 2 + rc, :]
             + cw[1:2, :] * u_pad[p0 - 1:p0 - 1 + rc, :])
        y = y + cw[2:3, :] * u
        o_ref[r0:r0 + rc, :] = (_dot(h, wb_bf[...]) * y).astype(o_ref.dtype)


def _conv_branch(h, w_in, conv_w, batch, seq, width, bn=256):
    m, d = h.shape
    nj = width // bn
    return pl.pallas_call(
        _conv_branch_kernel,
        grid=(nj, batch),
        in_specs=[pl.BlockSpec((seq, d), lambda j, b: (b, 0)),
                  pl.BlockSpec((d, bn), lambda j, b: (0, j)),
                  pl.BlockSpec((d, bn), lambda j, b: (0, j + nj)),
                  pl.BlockSpec((d, bn), lambda j, b: (0, j + 2 * nj)),
                  pl.BlockSpec((CONV_K, bn), lambda j, b: (0, j))],
        out_specs=pl.BlockSpec((seq, bn), lambda j, b: (b, j)),
        out_shape=pltpu.HBM((m, width), BF16),
        scratch_shapes=[pltpu.VMEM((d, bn), BF16)] * 3
        + [pltpu.VMEM((SUBLANES + seq, bn), F32)],
        compiler_params=_compiler_params(
            2,
            _nbytes((seq, d), BF16) + 3 * _nbytes((d, bn), F32) + _nbytes((seq, bn), BF16),
            3 * _nbytes((d, bn), BF16) + _nbytes((seq, bn), F32) + 8 * _nbytes((ROW_CHUNK, bn), F32)),
        name="conv_branch",
    )(h, w_in, w_in, w_in, conv_w)


def _head_norm_rope(x, gain, rope):
    width = x.shape[1]
    cos, sin_hi, sin_lo = (rope[:, i * LANES:(i + 1) * LANES] for i in range(3))
    r = lax.broadcasted_iota(jnp.int32, (MXU_DIM, MXU_DIM), 0) // HEAD_DIM
    c = lax.broadcasted_iota(jnp.int32, (MXU_DIM, MXU_DIM), 1) // HEAD_DIM
    ones_blockdiag = (r == c).astype(BF16)
    out = []
    for t in range(width // MXU_DIM):
        xt = x[:, t * MXU_DIM:(t + 1) * MXU_DIM]
        ss = _dot((xt * xt).astype(BF16), ones_blockdiag)
        y = xt * lax.rsqrt(ss * (1.0 / HEAD_DIM) + RMS_EPS) * gain[:, t * MXU_DIM:(t + 1) * MXU_DIM]
        for s in range(MXU_DIM // LANES):
            ys = y[:, s * LANES:(s + 1) * LANES]
            half = ROT_DIM // 2
            rot = (ys * cos
                   + pltpu.roll(ys, LANES - half, 1) * sin_hi
                   + pltpu.roll(ys, half, 1) * sin_lo)
            out.append(rot)
    return jnp.concatenate(out, axis=1)


def _qkv_kernel(h_ref, w_ref, gain_ref, raw_ref, rope_ref, o_ref, w_bf):
    @pl.when(pl.program_id(1) == 0)
    def _():
        w_bf[...] = w_ref[...].astype(BF16)

    gain = gain_ref[...]
    is_raw = raw_ref[...] != 0.0
    seq = h_ref.shape[0]
    rc = min(ROW_CHUNK, seq)
    for r0 in range(0, seq, rc):
        acc = _dot(h_ref[r0:r0 + rc, :], w_bf[...])
        y = _head_norm_rope(acc, gain, rope_ref[r0:r0 + rc, :])
        o_ref[r0:r0 + rc, :] = jnp.where(is_raw, acc, y).astype(o_ref.dtype)


def _qkv_proj(h, w_in, gains, raw_cols, rope, batch, seq, q_col, bn):
    m, d = h.shape
    assert q_col % bn == 0
    nj = gains.shape[0]
    first = q_col // bn
    table_spec = pl.BlockSpec((seq, 3 * LANES), lambda j, b: (0, 0))
    col_spec = pl.BlockSpec((None, 1, bn), lambda j, b: (j, 0, 0))
    return pl.pallas_call(
        _qkv_kernel,
        grid=(nj, batch),
        in_specs=[pl.BlockSpec((seq, d), lambda j, b: (b, 0)),
                  pl.BlockSpec((d, bn), lambda j, b: (0, first + j)),
                  col_spec, col_spec, table_spec],
        out_specs=pl.BlockSpec((seq, bn), lambda j, b: (b, j)),
        out_shape=pltpu.HBM((m, nj * bn), BF16),
        scratch_shapes=[pltpu.VMEM((d, bn), BF16)],
        compiler_params=_compiler_params(
            2,
            _nbytes((seq, d), BF16) + _nbytes((d, bn), F32) + _nbytes((seq, bn), BF16)
            + _nbytes((seq, 3 * LANES), F32),
            _nbytes((d, bn), BF16) + 10 * _nbytes((ROW_CHUNK, bn), F32)),
        name="qkv_proj",
    )(h, w_in, gains, raw_cols, rope)


ATTN_BLOCKS_PER_ITER = 4


def _attention_kernel(sink_ref, q_ref, kv_ref, o_ref, k_lo, k_hi, v_lo, v_hi, bias, s_scr, p_scr, inv_scr,
                      *, n_kv_heads):
    seq = q_ref.shape[0]
    kw = n_kv_heads * HEAD_DIM
    two_w = 2 * WINDOW
    assert LANES == 2 * HEAD_DIM and GROUP == 4

    lane = lax.broadcasted_iota(jnp.int32, (seq, LANES), 1)
    zero_block = jnp.zeros((WINDOW, LANES), BF16)
    for src0, lo_ref, hi_ref in ((0, k_lo, k_hi), (kw, v_lo, v_hi)):
        for grp in range(kw // LANES):
            x = kv_ref[:, src0 + grp * LANES:src0 + (grp + 1) * LANES].astype(F32)
            swapped = pltpu.roll(x, HEAD_DIM, 1)
            even_lo, odd_lo = jnp.where(lane < HEAD_DIM, x, 0.0), jnp.where(lane < HEAD_DIM, swapped, 0.0)
            even_hi, odd_hi = jnp.where(lane >= HEAD_DIM, swapped, 0.0), jnp.where(lane >= HEAD_DIM, x, 0.0)
            for hk, lo, hi in ((2 * grp, even_lo, even_hi), (2 * grp + 1, odd_lo, odd_hi)):
                lo_ref[hk, 0:WINDOW, :] = zero_block
                hi_ref[hk, 0:WINDOW, :] = zero_block
                lo_ref[hk, WINDOW:WINDOW + seq, :] = lo.astype(BF16)
                hi_ref[hk, WINDOW:WINDOW + seq, :] = hi.astype(BF16)

    @pl.when(pl.program_id(0) == 0)
    def _():
        key = lax.broadcasted_iota(jnp.int32, (2 * two_w, two_w), 0) % two_w
        qry = lax.broadcasted_iota(jnp.int32, (2 * two_w, two_w), 1) % WINDOW
        diff = qry + WINDOW - key
        in_window = (diff >= 0) & (diff < WINDOW)
        bias[0] = jnp.where(in_window & (key >= WINDOW), 0.0, NEG_INF)
        bias[1] = jnp.where(in_window, 0.0, NEG_INF)

    n_slabs = s_scr.shape[0]
    blocks_per_iter = n_slabs // n_kv_heads

    def blocks(i, carry):
        items = [(u * n_kv_heads + hk, hk, i * blocks_per_iter + u)
                 for u in range(blocks_per_iter) for hk in range(n_kv_heads)]
        for slab, hk, n in items:
            r0 = pl.multiple_of(n * WINDOW, WINDOW)
            c0 = hk * GROUP * HEAD_DIM
            q_pairs = jnp.concatenate([q_ref[pl.ds(r0, WINDOW), c0:c0 + LANES],
                                       q_ref[pl.ds(r0, WINDOW), c0 + LANES:c0 + 2 * LANES]], axis=0)
            keys = jnp.concatenate([k_lo[hk, pl.ds(r0, two_w), :], k_hi[hk, pl.ds(r0, two_w), :]], axis=0)
            s_scr[slab] = lax.dot_general(keys, q_pairs, (((1,), (1,)), ((), ())), preferred_element_type=F32)
        for slab, hk, n in items:
            t = jnp.minimum(n, 1)
            for half in range(2):
                for pair in range(2):
                    rows = slice(half * two_w, (half + 1) * two_w)
                    cols = slice(pair * WINDOW, (pair + 1) * WINDOW)
                    s_h = s_scr[slab, rows, cols] + bias[t, rows, cols]
                    sink = LOG2_E * sink_ref[hk * GROUP + 2 * pair + half]
                    mx = jnp.maximum(jnp.max(s_h, axis=0, keepdims=True), sink)
                    p = jnp.exp2(s_h - mx)
                    inv = 1.0 / (jnp.sum(p, axis=0, keepdims=True) + jnp.exp2(sink - mx))
                    inv_scr[2 * slab + half, :, cols] = jnp.broadcast_to(inv, (SUBLANES, WINDOW))
                    p_scr[slab, rows, cols] = p.astype(BF16)
        for slab, hk, n in items:
            r0 = pl.multiple_of(n * WINDOW, WINDOW)
            c0 = hk * GROUP * HEAD_DIM
            vals = jnp.concatenate([v_lo[hk, pl.ds(r0, two_w), :], v_hi[hk, pl.ds(r0, two_w), :]], axis=0)
            o_t = lax.dot_general(vals, p_scr[slab], (((0,), (0,)), ((), ())), preferred_element_type=F32)
            o_t = jnp.concatenate([o_t[0:HEAD_DIM, :] * inv_scr[2 * slab, 0:1, :],
                                   o_t[HEAD_DIM:LANES, :] * inv_scr[2 * slab + 1, 0:1, :]], axis=0)
            for pair in range(2):
                o_ref[pl.ds(r0, WINDOW), c0 + pair * LANES:c0 + (pair + 1) * LANES] = (
                    o_t[:, pair * WINDOW:(pair + 1) * WINDOW].T.astype(o_ref.dtype))
        return carry

    assert (seq // WINDOW) % blocks_per_iter == 0
    lax.fori_loop(0, seq // WINDOW // blocks_per_iter, blocks, 0)


def _attention(qkv, sinks, batch, seq, q_width, kv_width):
    m = qkv.shape[0]
    assert q_width % (2 * kv_width) == 0
    n_kv_heads = kv_width // HEAD_DIM
    n_slabs = ATTN_BLOCKS_PER_ITER * n_kv_heads
    return pl.pallas_call(
        functools.partial(_attention_kernel, n_kv_heads=n_kv_heads),
        grid=(batch,),
        in_specs=[pl.BlockSpec(memory_space=pltpu.SMEM),
                  pl.BlockSpec((seq, q_width), lambda b: (b, 0)),
                  pl.BlockSpec((seq, 2 * kv_width), lambda b: (b, q_width // (2 * kv_width)))],
        out_specs=pl.BlockSpec((seq, q_width), lambda b: (b, 0)),
        out_shape=pltpu.HBM((m, q_width), BF16),
        scratch_shapes=[pltpu.VMEM((n_kv_heads, WINDOW + seq, LANES), BF16)] * 4
        + [pltpu.VMEM((2, 4 * WINDOW, 2 * WINDOW), F32),
           pltpu.VMEM((n_slabs, 4 * WINDOW, 2 * WINDOW), F32),
           pltpu.VMEM((n_slabs, 4 * WINDOW, 2 * WINDOW), BF16),
           pltpu.VMEM((2 * n_slabs, SUBLANES, 2 * WINDOW), F32)],
        compiler_params=_compiler_params(
            1,
            2 * _nbytes((seq, q_width), BF16) + _nbytes((seq, 2 * kv_width), BF16),
            4 * _nbytes((n_kv_heads, WINDOW + seq, LANES), BF16)
            + (8 + 2 * n_slabs) * _nbytes((4 * WINDOW, 2 * WINDOW), F32)
            + 4 * _nbytes((seq, LANES), F32)),
        name="swa_attention",
    )(sinks, qkv, qkv)


MERGE_ROW_CHUNK = 128


def _merge_kernel(h_ref, yc_ref, at_ref, wga_ref, wgb_ref, woc_ref, woa_ref, wnext_ref,
                  o_ref, wnext_bf_ref, wga_bf, wgb_bf, woc_bf, woa_bf):
    @pl.when(pl.program_id(1) == 0)
    def _():
        wga_bf[...] = wga_ref[...].astype(BF16)
        wgb_bf[...] = wgb_ref[...].astype(BF16)
        woc_bf[...] = woc_ref[...].astype(BF16)
        woa_bf[...] = woa_ref[...].astype(BF16)

    wnext_bf_ref[...] = wnext_ref[...].astype(BF16)
    for rows in _row_chunks(h_ref.shape[0], MERGE_ROW_CHUNK):
        h = h_ref[rows, :]
        ya = _dot(yc_ref[rows, :], woc_bf[...])
        yb = _dot(at_ref[rows, :], woa_bf[...])
        ga = _dot(h, wga_bf[...])
        gb = _dot(h, wgb_bf[...])
        o_ref[rows, :] = (jax.nn.sigmoid(ga) * ya + jax.nn.sigmoid(gb) * yb).astype(o_ref.dtype)


def _merge(h, yconv, attn, w_in, w_out_conv, w_out_attn, w_next, gate_col, bm=512, bn=512):
    m, d = h.shape
    kc, ka = yconv.shape[1], attn.shape[1]
    assert gate_col % bn == 0
    ga0 = gate_col // bn
    nj, ni = d // bn, m // bm
    side_in, side_out, side_shape, side_bytes = _side_cast_specs(w_next, nj * ni, lambda j, i: j * ni + i)
    return pl.pallas_call(
        _merge_kernel,
        grid=(nj, ni),
        in_specs=[pl.BlockSpec((bm, d), lambda j, i: (i, 0)),
                  pl.BlockSpec((bm, kc), lambda j, i: (i, 0)),
                  pl.BlockSpec((bm, ka), lambda j, i: (i, 0)),
                  pl.BlockSpec((d, bn), lambda j, i: (0, ga0 + j)),
                  pl.BlockSpec((d, bn), lambda j, i: (0, ga0 + nj + j)),
                  pl.BlockSpec((kc, bn), lambda j, i: (0, j)),
                  pl.BlockSpec((ka, bn), lambda j, i: (0, j)),
                  side_in],
        out_specs=[pl.BlockSpec((bm, bn), lambda j, i: (i, j)), side_out],
        out_shape=[pltpu.HBM((m, d), BF16), side_shape],
        scratch_shapes=[pltpu.VMEM((d, bn), BF16), pltpu.VMEM((d, bn), BF16),
                        pltpu.VMEM((kc, bn), BF16), pltpu.VMEM((ka, bn), BF16)],
        compiler_params=_compiler_params(
            2,
            _nbytes((bm, d + kc + ka), BF16) + _nbytes((2 * d + kc + ka, bn), F32)
            + _nbytes((bm, bn), BF16) + side_bytes,
            _nbytes((2 * d + kc + ka, bn), BF16) + 10 * _nbytes((MERGE_ROW_CHUNK, bn), F32)),
        name="gated_merge",
    )(h, yconv, attn, w_in, w_in, w_out_conv, w_out_attn, w_next)


def _rope_tables(seq):
    half = ROT_DIM // 2
    inv_freq = (1.0 / (np.float32(ROPE_THETA) ** (np.arange(0, ROT_DIM, 2, dtype=np.float32) / ROT_DIM))
                ).astype(np.float32)
    ang = (np.arange(seq, dtype=np.float32)[:, None] * inv_freq[None, :]).astype(np.float64)
    cos, sin = np.cos(ang), np.sin(ang)
    ones = np.ones((seq, HEAD_DIM - ROT_DIM))
    zeros, zeros_tail = np.zeros((seq, half)), np.zeros((seq, HEAD_DIM - ROT_DIM))
    per_head = [np.concatenate([cos, cos, ones], axis=1),
                np.concatenate([-sin, zeros, zeros_tail], axis=1),
                np.concatenate([zeros, sin, zeros_tail], axis=1)]
    table = np.concatenate([np.tile(t, (1, LANES // HEAD_DIM)) for t in per_head], axis=1)
    return jnp.asarray(table.astype(np.float32))


def _swiglu_half_step(x, h, w_gu, w_down, next_gain):
    act, w_down_bf = _gate_up(h, w_gu, w_down)
    return _matmul_residual(act, w_down_bf, x, 0.5, next_gain)


def kernel(x, g_ffn1, w_gu1, w_down1, g_mix, w_in, conv_w, q_norm_g, k_norm_g, sinks,
           w_out_conv, w_out_attn, w_o, g_ffn2, w_gu2, w_down2):
    batch, seq, d = x.shape
    depth = w_in.shape[0]
    conv_width = conv_w.shape[2]
    q_width = w_out_attn.shape[1]
    kv_width = q_width // GROUP
    q_col = 3 * conv_width
    gate_col = q_col + q_width + 2 * kv_width
    rope = _rope_tables(seq)

    xf = x.reshape(batch * seq, d)
    h = _rmsnorm(xf, g_ffn1[0])
    for l in range(depth):
        xf, h = _swiglu_half_step(xf, h, w_gu1[l], w_down1[l], g_mix[l])

        yconv = _conv_branch(h, w_in[l], conv_w[l], batch, seq, conv_width)
        qkv_bn = 2 * kv_width
        assert q_width % qkv_bn == 0
        gains = jnp.concatenate(
            [jnp.tile(q_norm_g[l] * (ATTN_SCALE * LOG2_E), q_width // HEAD_DIM),
             jnp.tile(k_norm_g[l], kv_width // HEAD_DIM), jnp.ones((kv_width,), F32)]
        ).reshape(-1, 1, qkv_bn)
        raw_cols = jnp.concatenate(
            [jnp.zeros((q_width + kv_width,), F32), jnp.ones((kv_width,), F32)]
        ).reshape(-1, 1, qkv_bn)
        qkv = _qkv_proj(h, w_in[l], gains, raw_cols, rope, batch, seq, q_col, bn=qkv_bn)
        attn = _attention(qkv, sinks[l], batch, seq, q_width, kv_width)
        merged, w_o_bf = _merge(h, yconv, attn, w_in[l], w_out_conv[l], w_out_attn[l], w_o[l], gate_col)
        xf, h = _matmul_residual(merged, w_o_bf, xf, 1.0, g_ffn2[l])

        next_gain = g_ffn1[l + 1] if l + 1 < depth else None
        xf, h = _swiglu_half_step(xf, h, w_gu2[l], w_down2[l], next_gain)
    return xf.reshape(batch, seq, d)
```

```python
import functools
import math

import jax
import jax.numpy as jnp
import numpy as np
from jax import lax
from jax.experimental import pallas as pl
from jax.experimental.pallas import tpu as pltpu

F32 = jnp.float32
BF16 = jnp.bfloat16

HEAD_DIM = 64
GROUP = 4
CONV_K = 3
WINDOW = 128
ROT_DIM = HEAD_DIM // 4
ROPE_THETA = 500000.0
RMS_EPS = 1e-6
ATTN_SCALE = HEAD_DIM ** -0.5
LOG2_E = 1.4426950408889634
NEG_INF = -1e30

LANES = 128
SUBLANES = 8
MXU_DIM = 256
VMEM_CAP_V7X = 60 * 1024 * 1024


def _nbytes(shape, dtype):
    n = 1
    for s in shape:
        n *= s
    return n * jnp.dtype(dtype).itemsize


def _compiler_params(n_grid_axes, pipelined_bytes, resident_bytes):
    need = 2 * pipelined_bytes + resident_bytes
    return pltpu.CompilerParams(
        dimension_semantics=("arbitrary",) * n_grid_axes,
        vmem_limit_bytes=min(VMEM_CAP_V7X, need + need // 8),
    )


def _dot(a, b):
    return jnp.dot(a, b, preferred_element_type=F32)


ROW_CHUNK = 512


def _row_chunks(n_rows, chunk=ROW_CHUNK):
    chunk = min(chunk, n_rows)
    assert n_rows % chunk == 0
    return [pl.ds(r * chunk, chunk) for r in range(n_rows // chunk)]


def _tapered_chunks(n_rows, chunk=ROW_CHUNK):
    chunk = min(chunk, n_rows)
    assert n_rows % chunk == 0 and chunk % 2 == 0
    sizes = [chunk] * (n_rows // chunk - 1) + [chunk // 2, chunk // 2]
    starts = [sum(sizes[:i]) for i in range(len(sizes))]
    return list(zip(starts, sizes))


def _rmsnorm_kernel(x_ref, g_ref, o_ref):
    x = x_ref[...]
    ms = jnp.mean(x * x, axis=-1, keepdims=True)
    o_ref[...] = (x * lax.rsqrt(ms + RMS_EPS) * g_ref[...]).astype(o_ref.dtype)


def _rmsnorm(x, g, bm=1024):
    m, d = x.shape
    return pl.pallas_call(
        _rmsnorm_kernel,
        grid=(m // bm,),
        in_specs=[pl.BlockSpec((bm, d), lambda i: (i, 0)),
                  pl.BlockSpec((1, d), lambda i: (0, 0))],
        out_specs=pl.BlockSpec((bm, d), lambda i: (i, 0)),
        out_shape=pltpu.HBM((m, d), BF16),
        compiler_params=_compiler_params(
            1, _nbytes((bm, d), F32) + _nbytes((bm, d), BF16), 2 * _nbytes((bm, d), F32)),
        name="rmsnorm",
    )(x, g.reshape(1, d))


def _side_cast_specs(w_next, n_steps, step_index, col0=0, n_cols=None):
    k = w_next.shape[0]
    n_cols = w_next.shape[1] if n_cols is None else n_cols
    cols = math.gcd(col0, n_cols)
    n_col_blocks = n_cols // cols
    assert n_steps % n_col_blocks == 0
    n_row_blocks = n_steps // n_col_blocks
    assert k % n_row_blocks == 0 and (k // n_row_blocks) % (2 * SUBLANES) == 0 and cols % LANES == 0
    rows = k // n_row_blocks
    first = col0 // cols

    def in_map(*ids):
        t = step_index(*ids)
        return t // n_col_blocks, first + t % n_col_blocks

    def out_map(*ids):
        t = step_index(*ids)
        return t // n_col_blocks, t % n_col_blocks

    return (pl.BlockSpec((rows, cols), in_map), pl.BlockSpec((rows, cols), out_map),
            pltpu.HBM((k, n_cols), BF16), _nbytes((rows, cols), F32) + _nbytes((rows, cols), BF16))


def _gate_up_kernel(h_ref, wg_ref, wu_ref, wnext_ref, o_ref, wnext_bf_ref, wgu_bf):
    bn = wg_ref.shape[1]

    @pl.when(pl.program_id(1) == 0)
    def _():
        wgu_bf[:, :bn] = wg_ref[...].astype(BF16)
        wgu_bf[:, bn:] = wu_ref[...].astype(BF16)

    wnext_bf_ref[...] = wnext_ref[...].astype(BF16)
    for r0, rc in _tapered_chunks(h_ref.shape[0]):
        rows = pl.ds(r0, rc)
        gu = _dot(h_ref[rows, :], wgu_bf[...])
        g, u = gu[:, :bn], gu[:, bn:]
        o_ref[rows, :] = (g * jax.nn.sigmoid(g) * u).astype(o_ref.dtype)


def _gate_up(h, w_gu, w_next, bm=2048, bn=512):
    m, d = h.shape
    f = w_gu.shape[1] // 2
    nj, ni = f // bn, m // bm
    side_in, side_out, side_shape, side_bytes = _side_cast_specs(w_next, nj * ni, lambda j, i: j * ni + i)
    return pl.pallas_call(
        _gate_up_kernel,
        grid=(nj, ni),
        in_specs=[pl.BlockSpec((bm, d), lambda j, i: (i, 0)),
                  pl.BlockSpec((d, bn), lambda j, i: (0, j)),
                  pl.BlockSpec((d, bn), lambda j, i: (0, j + nj)),
                  side_in],
        out_specs=[pl.BlockSpec((bm, bn), lambda j, i: (i, j)), side_out],
        out_shape=[pltpu.HBM((m, f), BF16), side_shape],
        scratch_shapes=[pltpu.VMEM((d, 2 * bn), BF16)],
        compiler_params=_compiler_params(
            2,
            _nbytes((bm, d), BF16) + 2 * _nbytes((d, bn), F32) + _nbytes((bm, bn), BF16) + side_bytes,
            2 * _nbytes((d, bn), BF16) + 6 * _nbytes((ROW_CHUNK, bn), F32)),
        name="gate_up",
    )(h, w_gu, w_gu, w_next)


RESIDENT_CHUNKS = 2


WEIGHT_LOAD_CHUNKS = 4


def _matmul_residual_kernel(*refs, scale, with_norm):
    if with_norm:
        a_ref, w_hbm, r_ref, g_ref, x_ref, h_ref, w_vmem, w_sem = refs
    else:
        a_ref, w_hbm, r_ref, x_ref, w_vmem, w_sem = refs
    row_chunks = _row_chunks(a_ref.shape[0], a_ref.shape[0] // RESIDENT_CHUNKS)
    kc = w_vmem.shape[0] // WEIGHT_LOAD_CHUNKS

    def finish(rows, acc):
        x = r_ref[rows, :] + scale * acc
        x_ref[rows, :] = x
        if with_norm:
            ms = jnp.mean(x * x, axis=-1, keepdims=True)
            h_ref[rows, :] = (x * lax.rsqrt(ms + RMS_EPS) * g_ref[...]).astype(h_ref.dtype)

    def slab_copy(c):
        slab = pl.ds(c * kc, kc)
        return pltpu.make_async_copy(w_hbm.at[slab, :], w_vmem.at[slab, :], w_sem.at[c])

    @pl.when(pl.program_id(0) == 0)
    def _():
        for c in range(WEIGHT_LOAD_CHUNKS):
            slab_copy(c).start()
        for c in range(WEIGHT_LOAD_CHUNKS):
            slab_copy(c).wait()
            for rows in row_chunks:
                part = _dot(a_ref[rows, c * kc:(c + 1) * kc], w_vmem[c * kc:(c + 1) * kc, :])
                x_ref[rows, :] = part if c == 0 else x_ref[rows, :] + part
        for rows in row_chunks:
            finish(rows, x_ref[rows, :])

    @pl.when(pl.program_id(0) > 0)
    def _():
        for rows in row_chunks:
            finish(rows, _dot(a_ref[rows, :], w_vmem[...]))


def _matmul_residual(a, w_bf, res, scale, norm_gain=None):
    m, k = a.shape
    n = w_bf.shape[1]
    with_norm = norm_gain is not None
    assert k % (WEIGHT_LOAD_CHUNKS * LANES) == 0

    def resident_bytes(rows):
        return _nbytes((k, n), BF16) + 6 * _nbytes((rows // RESIDENT_CHUNKS, n), F32)

    def pipelined_bytes(rows):
        return (_nbytes((rows, k), BF16) + 2 * _nbytes((rows, n), F32)
                + (_nbytes((rows, n), BF16) if with_norm else 0))

    bm = next(rows for rows in (1024, 512, 256, 128)
              if m % rows == 0
              and (2 * pipelined_bytes(rows) + resident_bytes(rows)) * 9 // 8 <= VMEM_CAP_V7X)
    row_spec = lambda width: pl.BlockSpec((bm, width), lambda i: (i, 0))
    in_specs = [row_spec(k), pl.BlockSpec(memory_space=pltpu.HBM), row_spec(n)]
    operands = [a, w_bf, res]
    out_specs = [row_spec(n)]
    out_shape = [pltpu.HBM((m, n), F32)]
    if with_norm:
        in_specs.append(pl.BlockSpec((1, n), lambda i: (0, 0)))
        operands.append(norm_gain.reshape(1, n))
        out_specs.append(row_spec(n))
        out_shape.append(pltpu.HBM((m, n), BF16))
    outs = pl.pallas_call(
        functools.partial(_matmul_residual_kernel, scale=scale, with_norm=with_norm),
        grid=(m // bm,),
        in_specs=in_specs,
        out_specs=out_specs,
        out_shape=out_shape,
        scratch_shapes=[pltpu.VMEM((k, n), BF16), pltpu.SemaphoreType.DMA((WEIGHT_LOAD_CHUNKS,))],
        compiler_params=_compiler_params(1, pipelined_bytes(bm), resident_bytes(bm)),
        name="matmul_residual",
    )(*operands)
    return outs if with_norm else (outs[0], None)


def _conv_branch_kernel(h_ref, wx_ref, wb_ref, wc_ref, cw_ref, wnext_ref, o_ref, wnext_bf_ref,
                        wx_bf, wb_bf, wc_bf, u_pad):
    seq = h_ref.shape[0]
    wnext_bf_ref[...] = wnext_ref[...].astype(BF16)

    @pl.when(pl.program_id(1) == 0)
    def _():
        wx_bf[...] = wx_ref[...].astype(BF16)
        wb_bf[...] = wb_ref[...].astype(BF16)
        wc_bf[...] = wc_ref[...].astype(BF16)
        u_pad[0:SUBLANES, :] = jnp.zeros((SUBLANES, u_pad.shape[1]), F32)

    cw = cw_ref[...]
    for r0, rc in _tapered_chunks(seq):
        h = h_ref[r0:r0 + rc, :]
        u = _dot(h, wc_bf[...]) * _dot(h, wx_bf[...])
        p0 = SUBLANES + r0
        u_pad[p0:p0 + rc, :] = u
        y = (cw[0:1, :] * u_pad[p0 - 2:p0 - 2 + rc, :]
             + cw[1:2, :] * u_pad[p0 - 1:p0 - 1 + rc, :])
        y = y + cw[2:3, :] * u
        o_ref[r0:r0 + rc, :] = (_dot(h, wb_bf[...]) * y).astype(o_ref.dtype)


def _conv_branch(h, w_in, conv_w, batch, seq, width, next_col0, next_cols, bn=256):
    m, d = h.shape
    nj = width // bn
    side_in, side_out, side_shape, side_bytes = _side_cast_specs(
        w_in, nj * batch, lambda j, b: j * batch + b, next_col0, next_cols)
    return pl.pallas_call(
        _conv_branch_kernel,
        grid=(nj, batch),
        in_specs=[pl.BlockSpec((seq, d), lambda j, b: (b, 0)),
                  pl.BlockSpec((d, bn), lambda j, b: (0, j)),
                  pl.BlockSpec((d, bn), lambda j, b: (0, j + nj)),
                  pl.BlockSpec((d, bn), lambda j, b: (0, j + 2 * nj)),
                  pl.BlockSpec((CONV_K, bn), lambda j, b: (0, j)),
                  side_in],
        out_specs=[pl.BlockSpec((seq, bn), lambda j, b: (b, j)), side_out],
        out_shape=[pltpu.HBM((m, width), BF16), side_shape],
        scratch_shapes=[pltpu.VMEM((d, bn), BF16)] * 3
        + [pltpu.VMEM((SUBLANES + seq, bn), F32)],
        compiler_params=_compiler_params(
            2,
            _nbytes((seq, d), BF16) + 3 * _nbytes((d, bn), F32) + _nbytes((seq, bn), BF16) + side_bytes,
            3 * _nbytes((d, bn), BF16) + _nbytes((seq, bn), F32) + 8 * _nbytes((ROW_CHUNK, bn), F32)),
        name="conv_branch",
    )(h, w_in, w_in, w_in, conv_w, w_in)


def _head_norm_rope(x, gain, rope):
    width = x.shape[1]
    cos, sin_hi, sin_lo = (rope[:, i * LANES:(i + 1) * LANES] for i in range(3))
    r = lax.broadcasted_iota(jnp.int32, (MXU_DIM, MXU_DIM), 0) // HEAD_DIM
    c = lax.broadcasted_iota(jnp.int32, (MXU_DIM, MXU_DIM), 1) // HEAD_DIM
    ones_blockdiag = (r == c).astype(BF16)
    out = []
    for t in range(width // MXU_DIM):
        xt = x[:, t * MXU_DIM:(t + 1) * MXU_DIM]
        ss = _dot((xt * xt).astype(BF16), ones_blockdiag)
        y = xt * lax.rsqrt(ss * (1.0 / HEAD_DIM) + RMS_EPS) * gain[:, t * MXU_DIM:(t + 1) * MXU_DIM]
        for s in range(MXU_DIM // LANES):
            ys = y[:, s * LANES:(s + 1) * LANES]
            half = ROT_DIM // 2
            rot = (ys * cos
                   + pltpu.roll(ys, LANES - half, 1) * sin_hi
                   + pltpu.roll(ys, half, 1) * sin_lo)
            out.append(rot)
    return jnp.concatenate(out, axis=1)


def _qkv_kernel(h_ref, w_ref, gain_ref, raw_ref, rope_ref, o_ref, w_bf):
    @pl.when(pl.program_id(1) == 0)
    def _():
        w_bf[...] = w_ref[...].astype(BF16)

    gain = gain_ref[...]
    is_raw = raw_ref[...] != 0.0
    seq = h_ref.shape[0]
    for r0, rc in _tapered_chunks(seq):
        acc = _dot(h_ref[r0:r0 + rc, :], w_bf[...])
        y = _head_norm_rope(acc, gain, rope_ref[r0:r0 + rc, :])
        o_ref[r0:r0 + rc, :] = jnp.where(is_raw, acc, y).astype(o_ref.dtype)


def _qkv_proj(h, w_in, gains, raw_cols, rope, batch, seq, q_col, bn):
    m, d = h.shape
    assert q_col % bn == 0
    nj = gains.shape[0]
    first = q_col // bn
    table_spec = pl.BlockSpec((seq, 3 * LANES), lambda j, b: (0, 0))
    col_spec = pl.BlockSpec((None, 1, bn), lambda j, b: (j, 0, 0))
    return pl.pallas_call(
        _qkv_kernel,
        grid=(nj, batch),
        in_specs=[pl.BlockSpec((seq, d), lambda j, b: (b, 0)),
                  pl.BlockSpec((d, bn), lambda j, b: (0, first + j)),
                  col_spec, col_spec, table_spec],
        out_specs=pl.BlockSpec((seq, bn), lambda j, b: (b, j)),
        out_shape=pltpu.HBM((m, nj * bn), BF16),
        scratch_shapes=[pltpu.VMEM((d, bn), BF16)],
        compiler_params=_compiler_params(
            2,
            _nbytes((seq, d), BF16) + _nbytes((d, bn), F32) + _nbytes((seq, bn), BF16)
            + _nbytes((seq, 3 * LANES), F32),
            _nbytes((d, bn), BF16) + 10 * _nbytes((ROW_CHUNK, bn), F32)),
        name="qkv_proj",
    )(h, w_in, gains, raw_cols, rope)


ATTN_BLOCKS_PER_ITER = 4


def _attention_kernel(sink_ref, q_ref, kv_ref, o_ref, k_lo, k_hi, v_lo, v_hi, bias, s_scr, p_scr, inv_scr,
                      *, n_kv_heads):
    seq = q_ref.shape[0]
    kw = n_kv_heads * HEAD_DIM
    two_w = 2 * WINDOW
    assert LANES == 2 * HEAD_DIM and GROUP == 4

    lane = lax.broadcasted_iota(jnp.int32, (seq, LANES), 1)
    zero_block = jnp.zeros((WINDOW, LANES), BF16)
    for src0, lo_ref, hi_ref in ((0, k_lo, k_hi), (kw, v_lo, v_hi)):
        for grp in range(kw // LANES):
            x = kv_ref[:, src0 + grp * LANES:src0 + (grp + 1) * LANES]
            swapped = pltpu.roll(x, HEAD_DIM, 1)
            zero = jnp.zeros_like(x)
            even_lo, odd_lo = jnp.where(lane < HEAD_DIM, x, zero), jnp.where(lane < HEAD_DIM, swapped, zero)
            even_hi, odd_hi = jnp.where(lane >= HEAD_DIM, swapped, zero), jnp.where(lane >= HEAD_DIM, x, zero)
            for hk, lo, hi in ((2 * grp, even_lo, even_hi), (2 * grp + 1, odd_lo, odd_hi)):
                lo_ref[hk, 0:WINDOW, :] = zero_block
                hi_ref[hk, 0:WINDOW, :] = zero_block
                lo_ref[hk, WINDOW:WINDOW + seq, :] = lo
                hi_ref[hk, WINDOW:WINDOW + seq, :] = hi

    @pl.when(pl.program_id(0) == 0)
    def _():
        key = lax.broadcasted_iota(jnp.int32, (2 * two_w, two_w), 0) % two_w
        qry = lax.broadcasted_iota(jnp.int32, (2 * two_w, two_w), 1) % WINDOW
        diff = qry + WINDOW - key
        in_window = (diff >= 0) & (diff < WINDOW)
        bias[0] = jnp.where(in_window & (key >= WINDOW), 0.0, NEG_INF)
        bias[1] = jnp.where(in_window, 0.0, NEG_INF)

    n_slabs = s_scr.shape[0]
    blocks_per_iter = n_slabs // n_kv_heads

    def blocks(i, carry):
        items = [(u * n_kv_heads + hk, hk, i * blocks_per_iter + u)
                 for u in range(blocks_per_iter) for hk in range(n_kv_heads)]
        for slab, hk, n in items:
            r0 = pl.multiple_of(n * WINDOW, WINDOW)
            c0 = hk * GROUP * HEAD_DIM
            q_pairs = jnp.concatenate([q_ref[pl.ds(r0, WINDOW), c0:c0 + LANES],
                                       q_ref[pl.ds(r0, WINDOW), c0 + LANES:c0 + 2 * LANES]], axis=0)
            keys = jnp.concatenate([k_lo[hk, pl.ds(r0, two_w), :], k_hi[hk, pl.ds(r0, two_w), :]], axis=0)
            s_scr[slab] = lax.dot_general(keys, q_pairs, (((1,), (1,)), ((), ())), preferred_element_type=F32)
        for slab, hk, n in items:
            t = jnp.minimum(n, 1)
            for half in range(2):
                for pair in range(2):
                    rows = slice(half * two_w, (half + 1) * two_w)
                    cols = slice(pair * WINDOW, (pair + 1) * WINDOW)
                    s_h = s_scr[slab, rows, cols] + bias[t, rows, cols]
                    sink = LOG2_E * sink_ref[hk * GROUP + 2 * pair + half]
                    mx = jnp.maximum(jnp.max(s_h, axis=0, keepdims=True), sink)
                    p = jnp.exp2(s_h - mx)
                    inv = 1.0 / (jnp.sum(p, axis=0, keepdims=True) + jnp.exp2(sink - mx))
                    inv_scr[2 * slab + half, :, cols] = jnp.broadcast_to(inv, (SUBLANES, WINDOW))
                    p_scr[slab, rows, cols] = p.astype(BF16)
        for slab, hk, n in items:
            r0 = pl.multiple_of(n * WINDOW, WINDOW)
            c0 = hk * GROUP * HEAD_DIM
            vals = jnp.concatenate([v_lo[hk, pl.ds(r0, two_w), :], v_hi[hk, pl.ds(r0, two_w), :]], axis=0)
            o_t = lax.dot_general(vals, p_scr[slab], (((0,), (0,)), ((), ())), preferred_element_type=F32)
            o_t = jnp.concatenate([o_t[0:HEAD_DIM, :] * inv_scr[2 * slab, 0:1, :],
                                   o_t[HEAD_DIM:LANES, :] * inv_scr[2 * slab + 1, 0:1, :]], axis=0)
            for pair in range(2):
                o_ref[pl.ds(r0, WINDOW), c0 + pair * LANES:c0 + (pair + 1) * LANES] = (
                    o_t[:, pair * WINDOW:(pair + 1) * WINDOW].T.astype(o_ref.dtype))
        return carry

    assert (seq // WINDOW) % blocks_per_iter == 0
    lax.fori_loop(0, seq // WINDOW // blocks_per_iter, blocks, 0)


def _attention(qkv, sinks, batch, seq, q_width, kv_width):
    m = qkv.shape[0]
    assert q_width % (2 * kv_width) == 0
    n_kv_heads = kv_width // HEAD_DIM
    n_slabs = ATTN_BLOCKS_PER_ITER * n_kv_heads
    return pl.pallas_call(
        functools.partial(_attention_kernel, n_kv_heads=n_kv_heads),
        grid=(batch,),
        in_specs=[pl.BlockSpec(memory_space=pltpu.SMEM),
                  pl.BlockSpec((seq, q_width), lambda b: (b, 0)),
                  pl.BlockSpec((seq, 2 * kv_width), lambda b: (b, q_width // (2 * kv_width)))],
        out_specs=pl.BlockSpec((seq, q_width), lambda b: (b, 0)),
        out_shape=pltpu.HBM((m, q_width), BF16),
        scratch_shapes=[pltpu.VMEM((n_kv_heads, WINDOW + seq, LANES), BF16)] * 4
        + [pltpu.VMEM((2, 4 * WINDOW, 2 * WINDOW), F32),
           pltpu.VMEM((n_slabs, 4 * WINDOW, 2 * WINDOW), F32),
           pltpu.VMEM((n_slabs, 4 * WINDOW, 2 * WINDOW), BF16),
           pltpu.VMEM((2 * n_slabs, SUBLANES, 2 * WINDOW), F32)],
        compiler_params=_compiler_params(
            1,
            2 * _nbytes((seq, q_width), BF16) + _nbytes((seq, 2 * kv_width), BF16),
            4 * _nbytes((n_kv_heads, WINDOW + seq, LANES), BF16)
            + (8 + 2 * n_slabs) * _nbytes((4 * WINDOW, 2 * WINDOW), F32)
            + 4 * _nbytes((seq, LANES), F32)),
        name="swa_attention",
    )(sinks, qkv, qkv)


MERGE_ROW_CHUNK = 128


def _merge_kernel(h_ref, yc_ref, at_ref, wga_ref, wgb_ref, woc_ref, woa_ref, wnext_ref,
                  o_ref, wnext_bf_ref, woc_bf, woa_bf):
    @pl.when(pl.program_id(1) == 0)
    def _():
        woc_bf[...] = woc_ref[...].astype(BF16)
        woa_bf[...] = woa_ref[...].astype(BF16)

    wnext_bf_ref[...] = wnext_ref[...].astype(BF16)
    for rows in _row_chunks(h_ref.shape[0], MERGE_ROW_CHUNK):
        h = h_ref[rows, :]
        ya = _dot(yc_ref[rows, :], woc_bf[...])
        yb = _dot(at_ref[rows, :], woa_bf[...])
        ga = _dot(h, wga_ref[...])
        gb = _dot(h, wgb_ref[...])
        o_ref[rows, :] = (jax.nn.sigmoid(ga) * ya + jax.nn.sigmoid(gb) * yb).astype(o_ref.dtype)


def _merge(h, yconv, attn, w_gates_bf, w_out_conv, w_out_attn, w_next, bm=2048, bn=512):
    m, d = h.shape
    kc, ka = yconv.shape[1], attn.shape[1]
    nj, ni = d // bn, m // bm
    side_in, side_out, side_shape, side_bytes = _side_cast_specs(w_next, nj * ni, lambda j, i: j * ni + i)
    return pl.pallas_call(
        _merge_kernel,
        grid=(nj, ni),
        in_specs=[pl.BlockSpec((bm, d), lambda j, i: (i, 0)),
                  pl.BlockSpec((bm, kc), lambda j, i: (i, 0)),
                  pl.BlockSpec((bm, ka), lambda j, i: (i, 0)),
                  pl.BlockSpec((d, bn), lambda j, i: (0, j)),
                  pl.BlockSpec((d, bn), lambda j, i: (0, nj + j)),
                  pl.BlockSpec((kc, bn), lambda j, i: (0, j)),
                  pl.BlockSpec((ka, bn), lambda j, i: (0, j)),
                  side_in],
        out_specs=[pl.BlockSpec((bm, bn), lambda j, i: (i, j)), side_out],
        out_shape=[pltpu.HBM((m, d), BF16), side_shape],
        scratch_shapes=[pltpu.VMEM((kc, bn), BF16), pltpu.VMEM((ka, bn), BF16)],
        compiler_params=_compiler_params(
            2,
            _nbytes((bm, d + kc + ka), BF16) + _nbytes((2 * d, bn), BF16) + _nbytes((kc + ka, bn), F32)
            + _nbytes((bm, bn), BF16) + side_bytes,
            _nbytes((kc + ka, bn), BF16) + 10 * _nbytes((MERGE_ROW_CHUNK, bn), F32)),
        name="gated_merge",
    )(h, yconv, attn, w_gates_bf, w_gates_bf, w_out_conv, w_out_attn, w_next)


def _rope_tables(seq):
    half = ROT_DIM // 2
    inv_freq = (1.0 / (np.float32(ROPE_THETA) ** (np.arange(0, ROT_DIM, 2, dtype=np.float32) / ROT_DIM))
                ).astype(np.float32)
    ang = (np.arange(seq, dtype=np.float32)[:, None] * inv_freq[None, :]).astype(np.float64)
    cos, sin = np.cos(ang), np.sin(ang)
    ones = np.ones((seq, HEAD_DIM - ROT_DIM))
    zeros, zeros_tail = np.zeros((seq, half)), np.zeros((seq, HEAD_DIM - ROT_DIM))
    per_head = [np.concatenate([cos, cos, ones], axis=1),
                np.concatenate([-sin, zeros, zeros_tail], axis=1),
                np.concatenate([zeros, sin, zeros_tail], axis=1)]
    table = np.concatenate([np.tile(t, (1, LANES // HEAD_DIM)) for t in per_head], axis=1)
    return jnp.asarray(table.astype(np.float32))


def _swiglu_half_step(x, h, w_gu, w_down, next_gain):
    act, w_down_bf = _gate_up(h, w_gu, w_down)
    return _matmul_residual(act, w_down_bf, x, 0.5, next_gain)


def kernel(x, g_ffn1, w_gu1, w_down1, g_mix, w_in, conv_w, q_norm_g, k_norm_g, sinks,
           w_out_conv, w_out_attn, w_o, g_ffn2, w_gu2, w_down2):
    batch, seq, d = x.shape
    depth = w_in.shape[0]
    conv_width = conv_w.shape[2]
    q_width = w_out_attn.shape[1]
    kv_width = q_width // GROUP
    q_col = 3 * conv_width
    gate_col = q_col + q_width + 2 * kv_width
    rope = _rope_tables(seq)

    xf = x.reshape(batch * seq, d)
    h = _rmsnorm(xf, g_ffn1[0])
    for l in range(depth):
        xf, h = _swiglu_half_step(xf, h, w_gu1[l], w_down1[l], g_mix[l])

        yconv, w_gates_bf = _conv_branch(h, w_in[l], conv_w[l], batch, seq, conv_width, gate_col, 2 * d)
        qkv_bn = 2 * kv_width
        assert q_width % qkv_bn == 0
        gains = jnp.concatenate(
            [jnp.tile(q_norm_g[l] * (ATTN_SCALE * LOG2_E), q_width // HEAD_DIM),
             jnp.tile(k_norm_g[l], kv_width // HEAD_DIM), jnp.ones((kv_width,), F32)]
        ).reshape(-1, 1, qkv_bn)
        raw_cols = jnp.concatenate(
            [jnp.zeros((q_width + kv_width,), F32), jnp.ones((kv_width,), F32)]
        ).reshape(-1, 1, qkv_bn)
        qkv = _qkv_proj(h, w_in[l], gains, raw_cols, rope, batch, seq, q_col, bn=qkv_bn)
        attn = _attention(qkv, sinks[l], batch, seq, q_width, kv_width)
        merged, w_o_bf = _merge(h, yconv, attn, w_gates_bf, w_out_conv[l], w_out_attn[l], w_o[l])
        xf, h = _matmul_residual(merged, w_o_bf, xf, 1.0, g_ffn2[l])

        next_gain = g_ffn1[l + 1] if l + 1 < depth else None
        xf, h = _swiglu_half_step(xf, h, w_gu2[l], w_down2[l], next_gain)
    return xf.reshape(batch, seq, d)
```

```python
import functools
import math

import jax
import jax.numpy as jnp
import numpy as np
from jax import lax
from jax.experimental import pallas as pl
from jax.experimental.pallas import tpu as pltpu

F32 = jnp.float32
BF16 = jnp.bfloat16

HEAD_DIM = 64
GROUP = 4
CONV_K = 3
WINDOW = 128
ROT_DIM = HEAD_DIM // 4
ROPE_THETA = 500000.0
RMS_EPS = 1e-6
ATTN_SCALE = HEAD_DIM ** -0.5
LOG2_E = 1.4426950408889634
NEG_INF = -1e30

LANES = 128
SUBLANES = 8
MXU_DIM = 256
VMEM_CAP_V7X = 60 * 1024 * 1024


def _nbytes(shape, dtype):
    n = 1
    for s in shape:
        n *= s
    return n * jnp.dtype(dtype).itemsize


def _compiler_params(n_grid_axes, pipelined_bytes, resident_bytes):
    need = 2 * pipelined_bytes + resident_bytes
    return pltpu.CompilerParams(
        dimension_semantics=("arbitrary",) * n_grid_axes,
        vmem_limit_bytes=min(VMEM_CAP_V7X, need + need // 8),
    )


def _dot(a, b):
    return jnp.dot(a, b, preferred_element_type=F32)


ROW_CHUNK = 512


def _row_chunks(n_rows, chunk=ROW_CHUNK):
    chunk = min(chunk, n_rows)
    assert n_rows % chunk == 0
    return [pl.ds(r * chunk, chunk) for r in range(n_rows // chunk)]


def _tapered_chunks(n_rows, chunk=ROW_CHUNK):
    chunk = min(chunk, n_rows)
    assert n_rows % chunk == 0 and chunk % 2 == 0
    sizes = [chunk] * (n_rows // chunk - 1) + [chunk // 2, chunk // 2]
    starts = [sum(sizes[:i]) for i in range(len(sizes))]
    return list(zip(starts, sizes))


def _rmsnorm_kernel(x_ref, g_ref, o_ref):
    x = x_ref[...]
    ms = jnp.mean(x * x, axis=-1, keepdims=True)
    o_ref[...] = (x * lax.rsqrt(ms + RMS_EPS) * g_ref[...]).astype(o_ref.dtype)


def _rmsnorm(x, g, bm=1024):
    m, d = x.shape
    return pl.pallas_call(
        _rmsnorm_kernel,
        grid=(m // bm,),
        in_specs=[pl.BlockSpec((bm, d), lambda i: (i, 0)),
                  pl.BlockSpec((1, d), lambda i: (0, 0))],
        out_specs=pl.BlockSpec((bm, d), lambda i: (i, 0)),
        out_shape=pltpu.HBM((m, d), BF16),
        compiler_params=_compiler_params(
            1, _nbytes((bm, d), F32) + _nbytes((bm, d), BF16), 2 * _nbytes((bm, d), F32)),
        name="rmsnorm",
    )(x, g.reshape(1, d))


def _side_cast_specs(w_next, n_steps, step_index, col0=0, n_cols=None):
    k = w_next.shape[0]
    n_cols = w_next.shape[1] if n_cols is None else n_cols
    cols = math.gcd(col0, n_cols)
    n_col_blocks = n_cols // cols
    assert n_steps % n_col_blocks == 0
    n_row_blocks = n_steps // n_col_blocks
    assert k % n_row_blocks == 0 and (k // n_row_blocks) % (2 * SUBLANES) == 0 and cols % LANES == 0
    rows = k // n_row_blocks
    first = col0 // cols

    def in_map(*ids):
        t = step_index(*ids)
        return t // n_col_blocks, first + t % n_col_blocks

    def out_map(*ids):
        t = step_index(*ids)
        return t // n_col_blocks, t % n_col_blocks

    return (pl.BlockSpec((rows, cols), in_map), pl.BlockSpec((rows, cols), out_map),
            pltpu.HBM((k, n_cols), BF16), _nbytes((rows, cols), F32) + _nbytes((rows, cols), BF16))


def _gate_up_kernel(h_ref, wg_ref, wu_ref, wnext_ref, o_ref, wnext_bf_ref, wgu_bf):
    bn = wg_ref.shape[1]

    @pl.when(pl.program_id(1) == 0)
    def _():
        wgu_bf[:, :bn] = wg_ref[...].astype(BF16)
        wgu_bf[:, bn:] = wu_ref[...].astype(BF16)

    wnext_bf_ref[...] = wnext_ref[...].astype(BF16)
    for r0, rc in _tapered_chunks(h_ref.shape[0]):
        rows = pl.ds(r0, rc)
        gu = _dot(h_ref[rows, :], wgu_bf[...])
        g, u = gu[:, :bn], gu[:, bn:]
        o_ref[rows, :] = (g * jax.nn.sigmoid(g) * u).astype(o_ref.dtype)


def _gate_up(h, w_gu, w_next, bm=2048, bn=512):
    m, d = h.shape
    f = w_gu.shape[1] // 2
    nj, ni = f // bn, m // bm
    side_in, side_out, side_shape, side_bytes = _side_cast_specs(w_next, nj * ni, lambda j, i: j * ni + i)
    return pl.pallas_call(
        _gate_up_kernel,
        grid=(nj, ni),
        in_specs=[pl.BlockSpec((bm, d), lambda j, i: (i, 0)),
                  pl.BlockSpec((d, bn), lambda j, i: (0, j)),
                  pl.BlockSpec((d, bn), lambda j, i: (0, j + nj)),
                  side_in],
        out_specs=[pl.BlockSpec((bm, bn), lambda j, i: (i, j)), side_out],
        out_shape=[pltpu.HBM((m, f), BF16), side_shape],
        scratch_shapes=[pltpu.VMEM((d, 2 * bn), BF16)],
        compiler_params=_compiler_params(
            2,
            _nbytes((bm, d), BF16) + 2 * _nbytes((d, bn), F32) + _nbytes((bm, bn), BF16) + side_bytes,
            2 * _nbytes((d, bn), BF16) + 6 * _nbytes((ROW_CHUNK, bn), F32)),
        name="gate_up",
    )(h, w_gu, w_gu, w_next)


RESIDENT_CHUNKS = 2


def _matmul_residual_kernel(*refs, scale, with_norm):
    if with_norm:
        a_ref, w_ref, r_ref, g_ref, x_ref, h_ref = refs
    else:
        a_ref, w_ref, r_ref, x_ref = refs
    for rows in _row_chunks(a_ref.shape[0], a_ref.shape[0] // RESIDENT_CHUNKS):
        x = r_ref[rows, :] + scale * _dot(a_ref[rows, :], w_ref[...])
        x_ref[rows, :] = x
        if with_norm:
            ms = jnp.mean(x * x, axis=-1, keepdims=True)
            h_ref[rows, :] = (x * lax.rsqrt(ms + RMS_EPS) * g_ref[...]).astype(h_ref.dtype)


def _matmul_residual(a, w_bf, res, scale, norm_gain=None):
    m, k = a.shape
    n = w_bf.shape[1]
    with_norm = norm_gain is not None

    def resident_bytes(rows):
        return _nbytes((k, n), BF16) + 6 * _nbytes((rows // RESIDENT_CHUNKS, n), F32)

    def pipelined_bytes(rows):
        return (_nbytes((rows, k), BF16) + 2 * _nbytes((rows, n), F32)
                + (_nbytes((rows, n), BF16) if with_norm else 0))

    bm = next(rows for rows in (1024, 512, 256, 128)
              if m % rows == 0
              and (2 * pipelined_bytes(rows) + resident_bytes(rows)) * 9 // 8 <= VMEM_CAP_V7X)
    row_spec = lambda width: pl.BlockSpec((bm, width), lambda i: (i, 0))
    in_specs = [row_spec(k),
                pl.BlockSpec((k, n), lambda i: (0, 0), pipeline_mode=pl.Buffered(1)),
                row_spec(n)]
    operands = [a, w_bf, res]
    out_specs = [row_spec(n)]
    out_shape = [pltpu.HBM((m, n), F32)]
    if with_norm:
        in_specs.append(pl.BlockSpec((1, n), lambda i: (0, 0)))
        operands.append(norm_gain.reshape(1, n))
        out_specs.append(row_spec(n))
        out_shape.append(pltpu.HBM((m, n), BF16))
    outs = pl.pallas_call(
        functools.partial(_matmul_residual_kernel, scale=scale, with_norm=with_norm),
        grid=(m // bm,),
        in_specs=in_specs,
        out_specs=out_specs,
        out_shape=out_shape,
        compiler_params=_compiler_params(1, pipelined_bytes(bm), resident_bytes(bm)),
        name="matmul_residual",
    )(*operands)
    return outs if with_norm else (outs[0], None)


def _conv_branch_kernel(h_ref, wx_ref, wb_ref, wc_ref, cw_ref, wnext_ref, o_ref, wnext_bf_ref,
                        wx_bf, wb_bf, wc_bf, u_pad):
    seq = h_ref.shape[0]
    wnext_bf_ref[...] = wnext_ref[...].astype(BF16)

    @pl.when(pl.program_id(1) == 0)
    def _():
        wx_bf[...] = wx_ref[...].astype(BF16)
        wb_bf[...] = wb_ref[...].astype(BF16)
        wc_bf[...] = wc_ref[...].astype(BF16)
        u_pad[0:SUBLANES, :] = jnp.zeros((SUBLANES, u_pad.shape[1]), F32)

    cw = cw_ref[...]
    for r0, rc in _tapered_chunks(seq):
        h = h_ref[r0:r0 + rc, :]
        u = _dot(h, wc_bf[...]) * _dot(h, wx_bf[...])
        p0 = SUBLANES + r0
        u_pad[p0:p0 + rc, :] = u
        y = (cw[0:1, :] * u_pad[p0 - 2:p0 - 2 + rc, :]
             + cw[1:2, :] * u_pad[p0 - 1:p0 - 1 + rc, :])
        y = y + cw[2:3, :] * u
        o_ref[r0:r0 + rc, :] = (_dot(h, wb_bf[...]) * y).astype(o_ref.dtype)


def _conv_branch(h, w_in, conv_w, batch, seq, width, next_col0, next_cols, bn=256):
    m, d = h.shape
    nj = width // bn
    side_in, side_out, side_shape, side_bytes = _side_cast_specs(
        w_in, nj * batch, lambda j, b: j * batch + b, next_col0, next_cols)
    return pl.pallas_call(
        _conv_branch_kernel,
        grid=(nj, batch),
        in_specs=[pl.BlockSpec((seq, d), lambda j, b: (b, 0)),
                  pl.BlockSpec((d, bn), lambda j, b: (0, j)),
                  pl.BlockSpec((d, bn), lambda j, b: (0, j + nj)),
                  pl.BlockSpec((d, bn), lambda j, b: (0, j + 2 * nj)),
                  pl.BlockSpec((CONV_K, bn), lambda j, b: (0, j)),
                  side_in],
        out_specs=[pl.BlockSpec((seq, bn), lambda j, b: (b, j)), side_out],
        out_shape=[pltpu.HBM((m, width), BF16), side_shape],
        scratch_shapes=[pltpu.VMEM((d, bn), BF16)] * 3
        + [pltpu.VMEM((SUBLANES + seq, bn), F32)],
        compiler_params=_compiler_params(
            2,
            _nbytes((seq, d), BF16) + 3 * _nbytes((d, bn), F32) + _nbytes((seq, bn), BF16) + side_bytes,
            3 * _nbytes((d, bn), BF16) + _nbytes((seq, bn), F32) + 8 * _nbytes((ROW_CHUNK, bn), F32)),
        name="conv_branch",
    )(h, w_in, w_in, w_in, conv_w, w_in)


def _head_norm_rope(x, gain, rope):
    width = x.shape[1]
    cos, sin_hi, sin_lo = (rope[:, i * LANES:(i + 1) * LANES] for i in range(3))
    r = lax.broadcasted_iota(jnp.int32, (MXU_DIM, MXU_DIM), 0) // HEAD_DIM
    c = lax.broadcasted_iota(jnp.int32, (MXU_DIM, MXU_DIM), 1) // HEAD_DIM
    ones_blockdiag = (r == c).astype(BF16)
    out = []
    for t in range(width // MXU_DIM):
        xt = x[:, t * MXU_DIM:(t + 1) * MXU_DIM]
        ss = _dot((xt * xt).astype(BF16), ones_blockdiag)
        y = xt * lax.rsqrt(ss * (1.0 / HEAD_DIM) + RMS_EPS) * gain[:, t * MXU_DIM:(t + 1) * MXU_DIM]
        for s in range(MXU_DIM // LANES):
            ys = y[:, s * LANES:(s + 1) * LANES]
            half = ROT_DIM // 2
            rot = (ys * cos
                   + pltpu.roll(ys, LANES - half, 1) * sin_hi
                   + pltpu.roll(ys, half, 1) * sin_lo)
            out.append(rot)
    return jnp.concatenate(out, axis=1)


def _qkv_kernel(h_ref, w_ref, gain_ref, raw_ref, rope_ref, o_ref, w_bf):
    @pl.when(pl.program_id(1) == 0)
    def _():
        w_bf[...] = w_ref[...].astype(BF16)

    gain = gain_ref[...]
    is_raw = raw_ref[...] != 0.0
    seq = h_ref.shape[0]
    for r0, rc in _tapered_chunks(seq):
        acc = _dot(h_ref[r0:r0 + rc, :], w_bf[...])
        y = _head_norm_rope(acc, gain, rope_ref[r0:r0 + rc, :])
        o_ref[r0:r0 + rc, :] = jnp.where(is_raw, acc, y).astype(o_ref.dtype)


def _qkv_proj(h, w_in, gains, raw_cols, rope, batch, seq, q_col, bn):
    m, d = h.shape
    assert q_col % bn == 0
    nj = gains.shape[0]
    first = q_col // bn
    table_spec = pl.BlockSpec((seq, 3 * LANES), lambda j, b: (0, 0))
    col_spec = pl.BlockSpec((None, 1, bn), lambda j, b: (j, 0, 0))
    return pl.pallas_call(
        _qkv_kernel,
        grid=(nj, batch),
        in_specs=[pl.BlockSpec((seq, d), lambda j, b: (b, 0)),
                  pl.BlockSpec((d, bn), lambda j, b: (0, first + j)),
                  col_spec, col_spec, table_spec],
        out_specs=pl.BlockSpec((seq, bn), lambda j, b: (b, j)),
        out_shape=pltpu.HBM((m, nj * bn), BF16),
        scratch_shapes=[pltpu.VMEM((d, bn), BF16)],
        compiler_params=_compiler_params(
            2,
            _nbytes((seq, d), BF16) + _nbytes((d, bn), F32) + _nbytes((seq, bn), BF16)
            + _nbytes((seq, 3 * LANES), F32),
            _nbytes((d, bn), BF16) + 10 * _nbytes((ROW_CHUNK, bn), F32)),
        name="qkv_proj",
    )(h, w_in, gains, raw_cols, rope)


ATTN_BLOCKS_PER_ITER = 4


def _attention_kernel(sink_ref, q_ref, kv_ref, o_ref, k_lo, k_hi, v_lo, v_hi, bias, s_scr, p_scr, inv_scr,
                      *, n_kv_heads):
    seq = q_ref.shape[0]
    kw = n_kv_heads * HEAD_DIM
    two_w = 2 * WINDOW
    assert LANES == 2 * HEAD_DIM and GROUP == 4

    lane = lax.broadcasted_iota(jnp.int32, (seq, LANES), 1)
    zero_block = jnp.zeros((WINDOW, LANES), BF16)
    for src0, lo_ref, hi_ref in ((0, k_lo, k_hi), (kw, v_lo, v_hi)):
        for grp in range(kw // LANES):
            x = kv_ref[:, src0 + grp * LANES:src0 + (grp + 1) * LANES]
            swapped = pltpu.roll(x, HEAD_DIM, 1)
            zero = jnp.zeros_like(x)
            even_lo, odd_lo = jnp.where(lane < HEAD_DIM, x, zero), jnp.where(lane < HEAD_DIM, swapped, zero)
            even_hi, odd_hi = jnp.where(lane >= HEAD_DIM, swapped, zero), jnp.where(lane >= HEAD_DIM, x, zero)
            for hk, lo, hi in ((2 * grp, even_lo, even_hi), (2 * grp + 1, odd_lo, odd_hi)):
                lo_ref[hk, 0:WINDOW, :] = zero_block
                hi_ref[hk, 0:WINDOW, :] = zero_block
                lo_ref[hk, WINDOW:WINDOW + seq, :] = lo
                hi_ref[hk, WINDOW:WINDOW + seq, :] = hi

    @pl.when(pl.program_id(0) == 0)
    def _():
        key = lax.broadcasted_iota(jnp.int32, (2 * two_w, two_w), 0) % two_w
        qry = lax.broadcasted_iota(jnp.int32, (2 * two_w, two_w), 1) % WINDOW
        diff = qry + WINDOW - key
        in_window = (diff >= 0) & (diff < WINDOW)
        bias[0] = jnp.where(in_window & (key >= WINDOW), 0.0, NEG_INF)
        bias[1] = jnp.where(in_window, 0.0, NEG_INF)

    n_slabs = s_scr.shape[0]
    blocks_per_iter = n_slabs // n_kv_heads

    def blocks(i, carry):
        items = [(u * n_kv_heads + hk, hk, i * blocks_per_iter + u)
                 for u in range(blocks_per_iter) for hk in range(n_kv_heads)]
        for slab, hk, n in items:
            r0 = pl.multiple_of(n * WINDOW, WINDOW)
            c0 = hk * GROUP * HEAD_DIM
            q_pairs = jnp.concatenate([q_ref[pl.ds(r0, WINDOW), c0:c0 + LANES],
                                       q_ref[pl.ds(r0, WINDOW), c0 + LANES:c0 + 2 * LANES]], axis=0)
            keys = jnp.concatenate([k_lo[hk, pl.ds(r0, two_w), :], k_hi[hk, pl.ds(r0, two_w), :]], axis=0)
            s_scr[slab] = lax.dot_general(keys, q_pairs, (((1,), (1,)), ((), ())), preferred_element_type=F32)
        for slab, hk, n in items:
            t = jnp.minimum(n, 1)
            for half in range(2):
                for pair in range(2):
                    rows = slice(half * two_w, (half + 1) * two_w)
                    cols = slice(pair * WINDOW, (pair + 1) * WINDOW)
                    s_h = s_scr[slab, rows, cols] + bias[t, rows, cols]
                    sink = LOG2_E * sink_ref[hk * GROUP + 2 * pair + half]
                    mx = jnp.maximum(jnp.max(s_h, axis=0, keepdims=True), sink)
                    p = jnp.exp2(s_h - mx)
                    inv = 1.0 / (jnp.sum(p, axis=0, keepdims=True) + jnp.exp2(sink - mx))
                    inv_scr[2 * slab + half, :, cols] = jnp.broadcast_to(inv, (SUBLANES, WINDOW))
                    p_scr[slab, rows, cols] = p.astype(BF16)
        for slab, hk, n in items:
            r0 = pl.multiple_of(n * WINDOW, WINDOW)
            c0 = hk * GROUP * HEAD_DIM
            vals = jnp.concatenate([v_lo[hk, pl.ds(r0, two_w), :], v_hi[hk, pl.ds(r0, two_w), :]], axis=0)
            o_t = lax.dot_general(vals, p_scr[slab], (((0,), (0,)), ((), ())), preferred_element_type=F32)
            o_t = jnp.concatenate([o_t[0:HEAD_DIM, :] * inv_scr[2 * slab, 0:1, :],
                                   o_t[HEAD_DIM:LANES, :] * inv_scr[2 * slab + 1, 0:1, :]], axis=0)
            for pair in range(2):
                o_ref[pl.ds(r0, WINDOW), c0 + pair * LANES:c0 + (pair + 1) * LANES] = (
                    o_t[:, pair * WINDOW:(pair + 1) * WINDOW].T.astype(o_ref.dtype))
        return carry

    assert (seq // WINDOW) % blocks_per_iter == 0
    lax.fori_loop(0, seq // WINDOW // blocks_per_iter, blocks, 0)


def _attention(qkv, sinks, batch, seq, q_width, kv_width):
    m = qkv.shape[0]
    assert q_width % (2 * kv_width) == 0
    n_kv_heads = kv_width // HEAD_DIM
    n_slabs = ATTN_BLOCKS_PER_ITER * n_kv_heads
    return pl.pallas_call(
        functools.partial(_attention_kernel, n_kv_heads=n_kv_heads),
        grid=(batch,),
        in_specs=[pl.BlockSpec(memory_space=pltpu.SMEM),
                  pl.BlockSpec((seq, q_width), lambda b: (b, 0)),
                  pl.BlockSpec((seq, 2 * kv_width), lambda b: (b, q_width // (2 * kv_width)))],
        out_specs=pl.BlockSpec((seq, q_width), lambda b: (b, 0)),
        out_shape=pltpu.HBM((m, q_width), BF16),
        scratch_shapes=[pltpu.VMEM((n_kv_heads, WINDOW + seq, LANES), BF16)] * 4
        + [pltpu.VMEM((2, 4 * WINDOW, 2 * WINDOW), F32),
           pltpu.VMEM((n_slabs, 4 * WINDOW, 2 * WINDOW), F32),
           pltpu.VMEM((n_slabs, 4 * WINDOW, 2 * WINDOW), BF16),
           pltpu.VMEM((2 * n_slabs, SUBLANES, 2 * WINDOW), F32)],
        compiler_params=_compiler_params(
            1,
            2 * _nbytes((seq, q_width), BF16) + _nbytes((seq, 2 * kv_width), BF16),
            4 * _nbytes((n_kv_heads, WINDOW + seq, LANES), BF16)
            + (8 + 2 * n_slabs) * _nbytes((4 * WINDOW, 2 * WINDOW), F32)
            + 4 * _nbytes((seq, LANES), F32)),
        name="swa_attention",
    )(sinks, qkv, qkv)


MERGE_ROW_CHUNK = 128


def _merge_kernel(h_ref, yc_ref, at_ref, wga_ref, wgb_ref, woc_ref, woa_ref, wnext_ref,
                  o_ref, wnext_bf_ref, woc_bf, woa_bf):
    @pl.when(pl.program_id(1) == 0)
    def _():
        woc_bf[...] = woc_ref[...].astype(BF16)
        woa_bf[...] = woa_ref[...].astype(BF16)

    wnext_bf_ref[...] = wnext_ref[...].astype(BF16)
    for rows in _row_chunks(h_ref.shape[0], MERGE_ROW_CHUNK):
        h = h_ref[rows, :]
        ya = _dot(yc_ref[rows, :], woc_bf[...])
        yb = _dot(at_ref[rows, :], woa_bf[...])
        ga = _dot(h, wga_ref[...])
        gb = _dot(h, wgb_ref[...])
        o_ref[rows, :] = (jax.nn.sigmoid(ga) * ya + jax.nn.sigmoid(gb) * yb).astype(o_ref.dtype)


def _merge(h, yconv, attn, w_gates_bf, w_out_conv, w_out_attn, w_next, bm=2048, bn=512):
    m, d = h.shape
    kc, ka = yconv.shape[1], attn.shape[1]
    nj, ni = d // bn, m // bm
    side_in, side_out, side_shape, side_bytes = _side_cast_specs(w_next, nj * ni, lambda j, i: j * ni + i)
    return pl.pallas_call(
        _merge_kernel,
        grid=(nj, ni),
        in_specs=[pl.BlockSpec((bm, d), lambda j, i: (i, 0)),
                  pl.BlockSpec((bm, kc), lambda j, i: (i, 0)),
                  pl.BlockSpec((bm, ka), lambda j, i: (i, 0)),
                  pl.BlockSpec((d, bn), lambda j, i: (0, j)),
                  pl.BlockSpec((d, bn), lambda j, i: (0, nj + j)),
                  pl.BlockSpec((kc, bn), lambda j, i: (0, j)),
                  pl.BlockSpec((ka, bn), lambda j, i: (0, j)),
                  side_in],
        out_specs=[pl.BlockSpec((bm, bn), lambda j, i: (i, j)), side_out],
        out_shape=[pltpu.HBM((m, d), BF16), side_shape],
        scratch_shapes=[pltpu.VMEM((kc, bn), BF16), pltpu.VMEM((ka, bn), BF16)],
        compiler_params=_compiler_params(
            2,
            _nbytes((bm, d + kc + ka), BF16) + _nbytes((2 * d, bn), BF16) + _nbytes((kc + ka, bn), F32)
            + _nbytes((bm, bn), BF16) + side_bytes,
            _nbytes((kc + ka, bn), BF16) + 10 * _nbytes((MERGE_ROW_CHUNK, bn), F32)),
        name="gated_merge",
    )(h, yconv, attn, w_gates_bf, w_gates_bf, w_out_conv, w_out_attn, w_next)


def _rope_tables(seq):
    half = ROT_DIM // 2
    inv_freq = (1.0 / (np.float32(ROPE_THETA) ** (np.arange(0, ROT_DIM, 2, dtype=np.float32) / ROT_DIM))
                ).astype(np.float32)
    ang = (np.arange(seq, dtype=np.float32)[:, None] * inv_freq[None, :]).astype(np.float64)
    cos, sin = np.cos(ang), np.sin(ang)
    ones = np.ones((seq, HEAD_DIM - ROT_DIM))
    zeros, zeros_tail = np.zeros((seq, half)), np.zeros((seq, HEAD_DIM - ROT_DIM))
    per_head = [np.concatenate([cos, cos, ones], axis=1),
                np.concatenate([-sin, zeros, zeros_tail], axis=1),
                np.concatenate([zeros, sin, zeros_tail], axis=1)]
    table = np.concatenate([np.tile(t, (1, LANES // HEAD_DIM)) for t in per_head], axis=1)
    return jnp.asarray(table.astype(np.float32))


def _swiglu_half_step(x, h, w_gu, w_down, next_gain):
    act, w_down_bf = _gate_up(h, w_gu, w_down)
    return _matmul_residual(act, w_down_bf, x, 0.5, next_gain)


def kernel(x, g_ffn1, w_gu1, w_down1, g_mix, w_in, conv_w, q_norm_g, k_norm_g, sinks,
           w_out_conv, w_out_attn, w_o, g_ffn2, w_gu2, w_down2):
    batch, seq, d = x.shape
    depth = w_in.shape[0]
    conv_width = conv_w.shape[2]
    q_width = w_out_attn.shape[1]
    kv_width = q_width // GROUP
    q_col = 3 * conv_width
    gate_col = q_col + q_width + 2 * kv_width
    rope = _rope_tables(seq)

    xf = x.reshape(batch * seq, d)
    h = _rmsnorm(xf, g_ffn1[0])
    for l in range(depth):
        xf, h = _swiglu_half_step(xf, h, w_gu1[l], w_down1[l], g_mix[l])

        yconv, w_gates_bf = _conv_branch(h, w_in[l], conv_w[l], batch, seq, conv_width, gate_col, 2 * d)
        qkv_bn = 2 * kv_width
        assert q_width % qkv_bn == 0
        gains = jnp.concatenate(
            [jnp.tile(q_norm_g[l] * (ATTN_SCALE * LOG2_E), q_width // HEAD_DIM),
             jnp.tile(k_norm_g[l], kv_width // HEAD_DIM), jnp.ones((kv_width,), F32)]
        ).reshape(-1, 1, qkv_bn)
        raw_cols = jnp.concatenate(
            [jnp.zeros((q_width + kv_width,), F32), jnp.ones((kv_width,), F32)]
        ).reshape(-1, 1, qkv_bn)
        qkv = _qkv_proj(h, w_in[l], gains, raw_cols, rope, batch, seq, q_col, bn=qkv_bn)
        attn = _attention(qkv, sinks[l], batch, seq, q_width, kv_width)
        merged, w_o_bf = _merge(h, yconv, attn, w_gates_bf, w_out_conv[l], w_out_attn[l], w_o[l])
        xf, h = _matmul_residual(merged, w_o_bf, xf, 1.0, g_ffn2[l])

        next_gain = g_ffn1[l + 1] if l + 1 < depth else None
        xf, h = _swiglu_half_step(xf, h, w_gu2[l], w_down2[l], next_gain)
    return xf.reshape(batch, seq, d)
```

```python
import functools
import math

import jax
import jax.numpy as jnp
import numpy as np
from jax import lax
from jax.experimental import pallas as pl
from jax.experimental.pallas import tpu as pltpu

F32 = jnp.float32
BF16 = jnp.bfloat16

HEAD_DIM = 64
GROUP = 4
CONV_K = 3
WINDOW = 128
ROT_DIM = HEAD_DIM // 4
ROPE_THETA = 500000.0
RMS_EPS = 1e-6
ATTN_SCALE = HEAD_DIM ** -0.5
LOG2_E = 1.4426950408889634
NEG_INF = -1e30

LANES = 128
SUBLANES = 8
MXU_DIM = 256
VMEM_CAP_V7X = 60 * 1024 * 1024


def _nbytes(shape, dtype):
    n = 1
    for s in shape:
        n *= s
    return n * jnp.dtype(dtype).itemsize


def _compiler_params(n_grid_axes, pipelined_bytes, resident_bytes):
    need = 2 * pipelined_bytes + resident_bytes
    return pltpu.CompilerParams(
        dimension_semantics=("arbitrary",) * n_grid_axes,
        vmem_limit_bytes=min(VMEM_CAP_V7X, need + need // 8),
    )


def _dot(a, b):
    return jnp.dot(a, b, preferred_element_type=F32)


ROW_CHUNK = 512


def _row_chunks(n_rows, chunk=ROW_CHUNK):
    chunk = min(chunk, n_rows)
    assert n_rows % chunk == 0
    return [pl.ds(r * chunk, chunk) for r in range(n_rows // chunk)]


def _tapered_chunks(n_rows, chunk=ROW_CHUNK):
    chunk = min(chunk, n_rows)
    assert n_rows % chunk == 0 and chunk % 2 == 0
    sizes = [chunk] * (n_rows // chunk - 1) + [chunk // 2, chunk // 2]
    starts = [sum(sizes[:i]) for i in range(len(sizes))]
    return list(zip(starts, sizes))


def _rmsnorm_kernel(x_ref, g_ref, o_ref):
    x = x_ref[...]
    ms = jnp.mean(x * x, axis=-1, keepdims=True)
    o_ref[...] = (x * lax.rsqrt(ms + RMS_EPS) * g_ref[...]).astype(o_ref.dtype)


def _rmsnorm(x, g, bm=1024):
    m, d = x.shape
    return pl.pallas_call(
        _rmsnorm_kernel,
        grid=(m // bm,),
        in_specs=[pl.BlockSpec((bm, d), lambda i: (i, 0)),
                  pl.BlockSpec((1, d), lambda i: (0, 0))],
        out_specs=pl.BlockSpec((bm, d), lambda i: (i, 0)),
        out_shape=pltpu.HBM((m, d), BF16),
        compiler_params=_compiler_params(
            1, _nbytes((bm, d), F32) + _nbytes((bm, d), BF16), 2 * _nbytes((bm, d), F32)),
        name="rmsnorm",
    )(x, g.reshape(1, d))


def _side_cast_specs(w_next, n_steps, step_index, col0=0, n_cols=None):
    k = w_next.shape[0]
    n_cols = w_next.shape[1] if n_cols is None else n_cols
    cols = math.gcd(col0, n_cols)
    n_col_blocks = n_cols // cols
    assert n_steps % n_col_blocks == 0
    n_row_blocks = n_steps // n_col_blocks
    assert k % n_row_blocks == 0 and (k // n_row_blocks) % (2 * SUBLANES) == 0 and cols % LANES == 0
    rows = k // n_row_blocks
    first = col0 // cols

    def in_map(*ids):
        t = step_index(*ids)
        return t // n_col_blocks, first + t % n_col_blocks

    def out_map(*ids):
        t = step_index(*ids)
        return t // n_col_blocks, t % n_col_blocks

    return (pl.BlockSpec((rows, cols), in_map), pl.BlockSpec((rows, cols), out_map),
            pltpu.HBM((k, n_cols), BF16), _nbytes((rows, cols), F32) + _nbytes((rows, cols), BF16))


GATE_UP_ROW_CHUNK = 256


def _gate_up_kernel(h_ref, wg_ref, wu_ref, wnext_ref, o_ref, wnext_bf_ref, wgu_bf):
    bn = wg_ref.shape[1]

    @pl.when(pl.program_id(1) == 0)
    def _():
        wgu_bf[:, :bn] = wg_ref[...].astype(BF16)
        wgu_bf[:, bn:] = wu_ref[...].astype(BF16)

    wnext_bf_ref[...] = wnext_ref[...].astype(BF16)
    for r0, rc in _tapered_chunks(h_ref.shape[0], GATE_UP_ROW_CHUNK):
        rows = pl.ds(r0, rc)
        gu = _dot(h_ref[rows, :], wgu_bf[...])
        g, u = gu[:, :bn], gu[:, bn:]
        o_ref[rows, :] = (g * jax.nn.sigmoid(g) * u).astype(o_ref.dtype)


def _gate_up(h, w_gu, w_next, bm=2048, bn=512):
    m, d = h.shape
    f = w_gu.shape[1] // 2
    nj, ni = f // bn, m // bm
    side_in, side_out, side_shape, side_bytes = _side_cast_specs(w_next, nj * ni, lambda j, i: j * ni + i)
    return pl.pallas_call(
        _gate_up_kernel,
        grid=(nj, ni),
        in_specs=[pl.BlockSpec((bm, d), lambda j, i: (i, 0)),
                  pl.BlockSpec((d, bn), lambda j, i: (0, j)),
                  pl.BlockSpec((d, bn), lambda j, i: (0, j + nj)),
                  side_in],
        out_specs=[pl.BlockSpec((bm, bn), lambda j, i: (i, j)), side_out],
        out_shape=[pltpu.HBM((m, f), BF16), side_shape],
        scratch_shapes=[pltpu.VMEM((d, 2 * bn), BF16)],
        compiler_params=_compiler_params(
            2,
            _nbytes((bm, d), BF16) + 2 * _nbytes((d, bn), F32) + _nbytes((bm, bn), BF16) + side_bytes,
            2 * _nbytes((d, bn), BF16) + 6 * _nbytes((ROW_CHUNK, bn), F32)),
        name="gate_up",
    )(h, w_gu, w_gu, w_next)


RESIDENT_CHUNKS = 2


def _matmul_residual_kernel(*refs, scale, with_norm):
    if with_norm:
        a_ref, w_ref, r_ref, g_ref, x_ref, h_ref = refs
    else:
        a_ref, w_ref, r_ref, x_ref = refs
    for rows in _row_chunks(a_ref.shape[0], a_ref.shape[0] // RESIDENT_CHUNKS):
        x = r_ref[rows, :] + scale * _dot(a_ref[rows, :], w_ref[...])
        x_ref[rows, :] = x
        if with_norm:
            ms = jnp.mean(x * x, axis=-1, keepdims=True)
            h_ref[rows, :] = (x * lax.rsqrt(ms + RMS_EPS) * g_ref[...]).astype(h_ref.dtype)


def _matmul_residual(a, w_bf, res, scale, norm_gain=None):
    m, k = a.shape
    n = w_bf.shape[1]
    with_norm = norm_gain is not None

    def resident_bytes(rows):
        return _nbytes((k, n), BF16) + 6 * _nbytes((rows // RESIDENT_CHUNKS, n), F32)

    def pipelined_bytes(rows):
        return (_nbytes((rows, k), BF16) + 2 * _nbytes((rows, n), F32)
                + (_nbytes((rows, n), BF16) if with_norm else 0))

    bm = next(rows for rows in (1024, 512, 256, 128)
              if m % rows == 0
              and (2 * pipelined_bytes(rows) + resident_bytes(rows)) * 9 // 8 <= VMEM_CAP_V7X)
    row_spec = lambda width: pl.BlockSpec((bm, width), lambda i: (i, 0))
    in_specs = [row_spec(k),
                pl.BlockSpec((k, n), lambda i: (0, 0), pipeline_mode=pl.Buffered(1)),
                row_spec(n)]
    operands = [a, w_bf, res]
    out_specs = [row_spec(n)]
    out_shape = [pltpu.HBM((m, n), F32)]
    if with_norm:
        in_specs.append(pl.BlockSpec((1, n), lambda i: (0, 0)))
        operands.append(norm_gain.reshape(1, n))
        out_specs.append(row_spec(n))
        out_shape.append(pltpu.HBM((m, n), BF16))
    outs = pl.pallas_call(
        functools.partial(_matmul_residual_kernel, scale=scale, with_norm=with_norm),
        grid=(m // bm,),
        in_specs=in_specs,
        out_specs=out_specs,
        out_shape=out_shape,
        compiler_params=_compiler_params(1, pipelined_bytes(bm), resident_bytes(bm)),
        name="matmul_residual",
    )(*operands)
    return outs if with_norm else (outs[0], None)


def _conv_branch_kernel(h_ref, wx_ref, wb_ref, wc_ref, cw_ref, wnext_ref, o_ref, wnext_bf_ref,
                        wx_bf, wb_bf, wc_bf, u_pad):
    seq = h_ref.shape[0]
    wnext_bf_ref[...] = wnext_ref[...].astype(BF16)

    @pl.when(pl.program_id(1) == 0)
    def _():
        wx_bf[...] = wx_ref[...].astype(BF16)
        wb_bf[...] = wb_ref[...].astype(BF16)
        wc_bf[...] = wc_ref[...].astype(BF16)
        u_pad[0:SUBLANES, :] = jnp.zeros((SUBLANES, u_pad.shape[1]), F32)

    cw = cw_ref[...]
    for r0, rc in _tapered_chunks(seq):
        h = h_ref[r0:r0 + rc, :]
        u = _dot(h, wc_bf[...]) * _dot(h, wx_bf[...])
        p0 = SUBLANES + r0
        u_pad[p0:p0 + rc, :] = u
        y = (cw[0:1, :] * u_pad[p0 - 2:p0 - 2 + rc, :]
             + cw[1:2, :] * u_pad[p0 - 1:p0 - 1 + rc, :])
        y = y + cw[2:3, :] * u
        o_ref[r0:r0 + rc, :] = (_dot(h, wb_bf[...]) * y).astype(o_ref.dtype)


def _conv_branch(h, w_in, conv_w, batch, seq, width, next_col0, next_cols, bn=256):
    m, d = h.shape
    nj = width // bn
    side_in, side_out, side_shape, side_bytes = _side_cast_specs(
        w_in, nj * batch, lambda j, b: j * batch + b, next_col0, next_cols)
    return pl.pallas_call(
        _conv_branch_kernel,
        grid=(nj, batch),
        in_specs=[pl.BlockSpec((seq, d), lambda j, b: (b, 0)),
                  pl.BlockSpec((d, bn), lambda j, b: (0, j)),
                  pl.BlockSpec((d, bn), lambda j, b: (0, j + nj)),
                  pl.BlockSpec((d, bn), lambda j, b: (0, j + 2 * nj)),
                  pl.BlockSpec((CONV_K, bn), lambda j, b: (0, j)),
                  side_in],
        out_specs=[pl.BlockSpec((seq, bn), lambda j, b: (b, j)), side_out],
        out_shape=[pltpu.HBM((m, width), BF16), side_shape],
        scratch_shapes=[pltpu.VMEM((d, bn), BF16)] * 3
        + [pltpu.VMEM((SUBLANES + seq, bn), F32)],
        compiler_params=_compiler_params(
            2,
            _nbytes((seq, d), BF16) + 3 * _nbytes((d, bn), F32) + _nbytes((seq, bn), BF16) + side_bytes,
            3 * _nbytes((d, bn), BF16) + _nbytes((seq, bn), F32) + 8 * _nbytes((ROW_CHUNK, bn), F32)),
        name="conv_branch",
    )(h, w_in, w_in, w_in, conv_w, w_in)


def _head_norm_rope(x, gain, rope):
    width = x.shape[1]
    cos, sin_hi, sin_lo = (rope[:, i * LANES:(i + 1) * LANES] for i in range(3))
    r = lax.broadcasted_iota(jnp.int32, (MXU_DIM, MXU_DIM), 0) // HEAD_DIM
    c = lax.broadcasted_iota(jnp.int32, (MXU_DIM, MXU_DIM), 1) // HEAD_DIM
    ones_blockdiag = (r == c).astype(BF16)
    out = []
    for t in range(width // MXU_DIM):
        xt = x[:, t * MXU_DIM:(t + 1) * MXU_DIM]
        ss = _dot((xt * xt).astype(BF16), ones_blockdiag)
        y = xt * lax.rsqrt(ss * (1.0 / HEAD_DIM) + RMS_EPS) * gain[:, t * MXU_DIM:(t + 1) * MXU_DIM]
        for s in range(MXU_DIM // LANES):
            ys = y[:, s * LANES:(s + 1) * LANES]
            half = ROT_DIM // 2
            rot = (ys * cos
                   + pltpu.roll(ys, LANES - half, 1) * sin_hi
                   + pltpu.roll(ys, half, 1) * sin_lo)
            out.append(rot)
    return jnp.concatenate(out, axis=1)


def _qkv_kernel(h_ref, w_ref, gain_ref, raw_ref, rope_ref, o_ref, w_bf):
    @pl.when(pl.program_id(1) == 0)
    def _():
        w_bf[...] = w_ref[...].astype(BF16)

    gain = gain_ref[...]
    is_raw = raw_ref[...] != 0.0
    seq = h_ref.shape[0]
    for r0, rc in _tapered_chunks(seq):
        acc = _dot(h_ref[r0:r0 + rc, :], w_bf[...])
        y = _head_norm_rope(acc, gain, rope_ref[r0:r0 + rc, :])
        o_ref[r0:r0 + rc, :] = jnp.where(is_raw, acc, y).astype(o_ref.dtype)


def _qkv_proj(h, w_in, gains, raw_cols, rope, batch, seq, q_col, bn):
    m, d = h.shape
    assert q_col % bn == 0
    nj = gains.shape[0]
    first = q_col // bn
    table_spec = pl.BlockSpec((seq, 3 * LANES), lambda j, b: (0, 0))
    col_spec = pl.BlockSpec((None, 1, bn), lambda j, b: (j, 0, 0))
    return pl.pallas_call(
        _qkv_kernel,
        grid=(nj, batch),
        in_specs=[pl.BlockSpec((seq, d), lambda j, b: (b, 0)),
                  pl.BlockSpec((d, bn), lambda j, b: (0, first + j)),
                  col_spec, col_spec, table_spec],
        out_specs=pl.BlockSpec((seq, bn), lambda j, b: (b, j)),
        out_shape=pltpu.HBM((m, nj * bn), BF16),
        scratch_shapes=[pltpu.VMEM((d, bn), BF16)],
        compiler_params=_compiler_params(
            2,
            _nbytes((seq, d), BF16) + _nbytes((d, bn), F32) + _nbytes((seq, bn), BF16)
            + _nbytes((seq, 3 * LANES), F32),
            _nbytes((d, bn), BF16) + 10 * _nbytes((ROW_CHUNK, bn), F32)),
        name="qkv_proj",
    )(h, w_in, gains, raw_cols, rope)


ATTN_BLOCKS_PER_ITER = 4


def _attention_kernel(sink_ref, q_ref, kv_ref, o_ref, k_lo, k_hi, v_lo, v_hi, bias, s_scr, p_scr, inv_scr,
                      *, n_kv_heads):
    seq = q_ref.shape[0]
    kw = n_kv_heads * HEAD_DIM
    two_w = 2 * WINDOW
    assert LANES == 2 * HEAD_DIM and GROUP == 4

    lane = lax.broadcasted_iota(jnp.int32, (seq, LANES), 1)
    zero_block = jnp.zeros((WINDOW, LANES), BF16)
    for src0, lo_ref, hi_ref in ((0, k_lo, k_hi), (kw, v_lo, v_hi)):
        for grp in range(kw // LANES):
            x = kv_ref[:, src0 + grp * LANES:src0 + (grp + 1) * LANES]
            swapped = pltpu.roll(x, HEAD_DIM, 1)
            zero = jnp.zeros_like(x)
            even_lo, odd_lo = jnp.where(lane < HEAD_DIM, x, zero), jnp.where(lane < HEAD_DIM, swapped, zero)
            even_hi, odd_hi = jnp.where(lane >= HEAD_DIM, swapped, zero), jnp.where(lane >= HEAD_DIM, x, zero)
            for hk, lo, hi in ((2 * grp, even_lo, even_hi), (2 * grp + 1, odd_lo, odd_hi)):
                lo_ref[hk, 0:WINDOW, :] = zero_block
                hi_ref[hk, 0:WINDOW, :] = zero_block
                lo_ref[hk, WINDOW:WINDOW + seq, :] = lo
                hi_ref[hk, WINDOW:WINDOW + seq, :] = hi

    @pl.when(pl.program_id(0) == 0)
    def _():
        key = lax.broadcasted_iota(jnp.int32, (2 * two_w, two_w), 0) % two_w
        qry = lax.broadcasted_iota(jnp.int32, (2 * two_w, two_w), 1) % WINDOW
        diff = qry + WINDOW - key
        in_window = (diff >= 0) & (diff < WINDOW)
        bias[0] = jnp.where(in_window & (key >= WINDOW), 0.0, NEG_INF)
        bias[1] = jnp.where(in_window, 0.0, NEG_INF)

    n_slabs = s_scr.shape[0]
    blocks_per_iter = n_slabs // n_kv_heads

    def blocks(i, carry):
        items = [(u * n_kv_heads + hk, hk, i * blocks_per_iter + u)
                 for u in range(blocks_per_iter) for hk in range(n_kv_heads)]
        for slab, hk, n in items:
            r0 = pl.multiple_of(n * WINDOW, WINDOW)
            c0 = hk * GROUP * HEAD_DIM
            q_pairs = jnp.concatenate([q_ref[pl.ds(r0, WINDOW), c0:c0 + LANES],
                                       q_ref[pl.ds(r0, WINDOW), c0 + LANES:c0 + 2 * LANES]], axis=0)
            keys = jnp.concatenate([k_lo[hk, pl.ds(r0, two_w), :], k_hi[hk, pl.ds(r0, two_w), :]], axis=0)
            s_scr[slab] = lax.dot_general(keys, q_pairs, (((1,), (1,)), ((), ())), preferred_element_type=F32)
        for slab, hk, n in items:
            t = jnp.minimum(n, 1)
            for half in range(2):
                for pair in range(2):
                    rows = slice(half * two_w, (half + 1) * two_w)
                    cols = slice(pair * WINDOW, (pair + 1) * WINDOW)
                    s_h = s_scr[slab, rows, cols] + bias[t, rows, cols]
                    sink = LOG2_E * sink_ref[hk * GROUP + 2 * pair + half]
                    mx = jnp.maximum(jnp.max(s_h, axis=0, keepdims=True), sink)
                    p = jnp.exp2(s_h - mx)
                    inv = 1.0 / (jnp.sum(p, axis=0, keepdims=True) + jnp.exp2(sink - mx))
                    inv_scr[2 * slab + half, :, cols] = jnp.broadcast_to(inv, (SUBLANES, WINDOW))
                    p_scr[slab, rows, cols] = p.astype(BF16)
        for slab, hk, n in items:
            r0 = pl.multiple_of(n * WINDOW, WINDOW)
            c0 = hk * GROUP * HEAD_DIM
            vals = jnp.concatenate([v_lo[hk, pl.ds(r0, two_w), :], v_hi[hk, pl.ds(r0, two_w), :]], axis=0)
            o_t = lax.dot_general(vals, p_scr[slab], (((0,), (0,)), ((), ())), preferred_element_type=F32)
            o_t = jnp.concatenate([o_t[0:HEAD_DIM, :] * inv_scr[2 * slab, 0:1, :],
                                   o_t[HEAD_DIM:LANES, :] * inv_scr[2 * slab + 1, 0:1, :]], axis=0)
            for pair in range(2):
                o_ref[pl.ds(r0, WINDOW), c0 + pair * LANES:c0 + (pair + 1) * LANES] = (
                    o_t[:, pair * WINDOW:(pair + 1) * WINDOW].T.astype(o_ref.dtype))
        return carry

    assert (seq // WINDOW) % blocks_per_iter == 0
    lax.fori_loop(0, seq // WINDOW // blocks_per_iter, blocks, 0)


def _attention(qkv, sinks, batch, seq, q_width, kv_width):
    m = qkv.shape[0]
    assert q_width % (2 * kv_width) == 0
    n_kv_heads = kv_width // HEAD_DIM
    n_slabs = ATTN_BLOCKS_PER_ITER * n_kv_heads
    return pl.pallas_call(
        functools.partial(_attention_kernel, n_kv_heads=n_kv_heads),
        grid=(batch,),
        in_specs=[pl.BlockSpec(memory_space=pltpu.SMEM),
                  pl.BlockSpec((seq, q_width), lambda b: (b, 0)),
                  pl.BlockSpec((seq, 2 * kv_width), lambda b: (b, q_width // (2 * kv_width)))],
        out_specs=pl.BlockSpec((seq, q_width), lambda b: (b, 0)),
        out_shape=pltpu.HBM((m, q_width), BF16),
        scratch_shapes=[pltpu.VMEM((n_kv_heads, WINDOW + seq, LANES), BF16)] * 4
        + [pltpu.VMEM((2, 4 * WINDOW, 2 * WINDOW), F32),
           pltpu.VMEM((n_slabs, 4 * WINDOW, 2 * WINDOW), F32),
           pltpu.VMEM((n_slabs, 4 * WINDOW, 2 * WINDOW), BF16),
           pltpu.VMEM((2 * n_slabs, SUBLANES, 2 * WINDOW), F32)],
        compiler_params=_compiler_params(
            1,
            2 * _nbytes((seq, q_width), BF16) + _nbytes((seq, 2 * kv_width), BF16),
            4 * _nbytes((n_kv_heads, WINDOW + seq, LANES), BF16)
            + (8 + 2 * n_slabs) * _nbytes((4 * WINDOW, 2 * WINDOW), F32)
            + 4 * _nbytes((seq, LANES), F32)),
        name="swa_attention",
    )(sinks, qkv, qkv)


MERGE_ROW_CHUNK = 128


def _merge_kernel(h_ref, yc_ref, at_ref, wga_ref, wgb_ref, woc_ref, woa_ref, wnext_ref,
                  o_ref, wnext_bf_ref, woc_bf, woa_bf):
    @pl.when(pl.program_id(1) == 0)
    def _():
        woc_bf[...] = woc_ref[...].astype(BF16)
        woa_bf[...] = woa_ref[...].astype(BF16)

    wnext_bf_ref[...] = wnext_ref[...].astype(BF16)
    for rows in _row_chunks(h_ref.shape[0], MERGE_ROW_CHUNK):
        h = h_ref[rows, :]
        ya = _dot(yc_ref[rows, :], woc_bf[...])
        yb = _dot(at_ref[rows, :], woa_bf[...])
        ga = _dot(h, wga_ref[...])
        gb = _dot(h, wgb_ref[...])
        o_ref[rows, :] = (jax.nn.sigmoid(ga) * ya + jax.nn.sigmoid(gb) * yb).astype(o_ref.dtype)


def _merge(h, yconv, attn, w_gates_bf, w_out_conv, w_out_attn, w_next, bm=2048, bn=512):
    m, d = h.shape
    kc, ka = yconv.shape[1], attn.shape[1]
    nj, ni = d // bn, m // bm
    side_in, side_out, side_shape, side_bytes = _side_cast_specs(w_next, nj * ni, lambda j, i: j * ni + i)
    return pl.pallas_call(
        _merge_kernel,
        grid=(nj, ni),
        in_specs=[pl.BlockSpec((bm, d), lambda j, i: (i, 0)),
                  pl.BlockSpec((bm, kc), lambda j, i: (i, 0)),
                  pl.BlockSpec((bm, ka), lambda j, i: (i, 0)),
                  pl.BlockSpec((d, bn), lambda j, i: (0, j)),
                  pl.BlockSpec((d, bn), lambda j, i: (0, nj + j)),
                  pl.BlockSpec((kc, bn), lambda j, i: (0, j)),
                  pl.BlockSpec((ka, bn), lambda j, i: (0, j)),
                  side_in],
        out_specs=[pl.BlockSpec((bm, bn), lambda j, i: (i, j)), side_out],
        out_shape=[pltpu.HBM((m, d), BF16), side_shape],
        scratch_shapes=[pltpu.VMEM((kc, bn), BF16), pltpu.VMEM((ka, bn), BF16)],
        compiler_params=_compiler_params(
            2,
            _nbytes((bm, d + kc + ka), BF16) + _nbytes((2 * d, bn), BF16) + _nbytes((kc + ka, bn), F32)
            + _nbytes((bm, bn), BF16) + side_bytes,
            _nbytes((kc + ka, bn), BF16) + 10 * _nbytes((MERGE_ROW_CHUNK, bn), F32)),
        name="gated_merge",
    )(h, yconv, attn, w_gates_bf, w_gates_bf, w_out_conv, w_out_attn, w_next)


def _rope_tables(seq):
    half = ROT_DIM // 2
    inv_freq = (1.0 / (np.float32(ROPE_THETA) ** (np.arange(0, ROT_DIM, 2, dtype=np.float32) / ROT_DIM))
                ).astype(np.float32)
    ang = (np.arange(seq, dtype=np.float32)[:, None] * inv_freq[None, :]).astype(np.float64)
    cos, sin = np.cos(ang), np.sin(ang)
    ones = np.ones((seq, HEAD_DIM - ROT_DIM))
    zeros, zeros_tail = np.zeros((seq, half)), np.zeros((seq, HEAD_DIM - ROT_DIM))
    per_head = [np.concatenate([cos, cos, ones], axis=1),
                np.concatenate([-sin, zeros, zeros_tail], axis=1),
                np.concatenate([zeros, sin, zeros_tail], axis=1)]
    table = np.concatenate([np.tile(t, (1, LANES // HEAD_DIM)) for t in per_head], axis=1)
    return jnp.asarray(table.astype(np.float32))


def _swiglu_half_step(x, h, w_gu, w_down, next_gain):
    act, w_down_bf = _gate_up(h, w_gu, w_down)
    return _matmul_residual(act, w_down_bf, x, 0.5, next_gain)


def kernel(x, g_ffn1, w_gu1, w_down1, g_mix, w_in, conv_w, q_norm_g, k_norm_g, sinks,
           w_out_conv, w_out_attn, w_o, g_ffn2, w_gu2, w_down2):
    batch, seq, d = x.shape
    depth = w_in.shape[0]
    conv_width = conv_w.shape[2]
    q_width = w_out_attn.shape[1]
    kv_width = q_width // GROUP
    q_col = 3 * conv_width
    gate_col = q_col + q_width + 2 * kv_width
    rope = _rope_tables(seq)

    xf = x.reshape(batch * seq, d)
    h = _rmsnorm(xf, g_ffn1[0])
    for l in range(depth):
        xf, h = _swiglu_half_step(xf, h, w_gu1[l], w_down1[l], g_mix[l])

        yconv, w_gates_bf = _conv_branch(h, w_in[l], conv_w[l], batch, seq, conv_width, gate_col, 2 * d)
        qkv_bn = 2 * kv_width
        assert q_width % qkv_bn == 0
        gains = jnp.concatenate(
            [jnp.tile(q_norm_g[l] * (ATTN_SCALE * LOG2_E), q_width // HEAD_DIM),
             jnp.tile(k_norm_g[l], kv_width // HEAD_DIM), jnp.ones((kv_width,), F32)]
        ).reshape(-1, 1, qkv_bn)
        raw_cols = jnp.concatenate(
            [jnp.zeros((q_width + kv_width,), F32), jnp.ones((kv_width,), F32)]
        ).reshape(-1, 1, qkv_bn)
        qkv = _qkv_proj(h, w_in[l], gains, raw_cols, rope, batch, seq, q_col, bn=qkv_bn)
        attn = _attention(qkv, sinks[l], batch, seq, q_width, kv_width)
        merged, w_o_bf = _merge(h, yconv, attn, w_gates_bf, w_out_conv[l], w_out_attn[l], w_o[l])
        xf, h = _matmul_residual(merged, w_o_bf, xf, 1.0, g_ffn2[l])

        next_gain = g_ffn1[l + 1] if l + 1 < depth else None
        xf, h = _swiglu_half_step(xf, h, w_gu2[l], w_down2[l], next_gain)
    return xf.reshape(batch, seq, d)
```

```python
import functools
import math

import jax
import jax.numpy as jnp
import numpy as np
from jax import lax
from jax.experimental import pallas as pl
from jax.experimental.pallas import tpu as pltpu

F32 = jnp.float32
BF16 = jnp.bfloat16

HEAD_DIM = 64
GROUP = 4
CONV_K = 3
WINDOW = 128
ROT_DIM = HEAD_DIM // 4
ROPE_THETA = 500000.0
RMS_EPS = 1e-6
ATTN_SCALE = HEAD_DIM ** -0.5
LOG2_E = 1.4426950408889634
NEG_INF = -1e30

LANES = 128
SUBLANES = 8
MXU_DIM = 256
VMEM_CAP_V7X = 60 * 1024 * 1024


def _nbytes(shape, dtype):
    n = 1
    for s in shape:
        n *= s
    return n * jnp.dtype(dtype).itemsize


def _compiler_params(n_grid_axes, pipelined_bytes, resident_bytes):
    need = 2 * pipelined_bytes + resident_bytes
    return pltpu.CompilerParams(
        dimension_semantics=("arbitrary",) * n_grid_axes,
        vmem_limit_bytes=min(VMEM_CAP_V7X, need + need // 8),
    )


def _dot(a, b):
    return jnp.dot(a, b, preferred_element_type=F32)


ROW_CHUNK = 1024


def _row_chunks(n_rows, chunk=ROW_CHUNK):
    chunk = min(chunk, n_rows)
    assert n_rows % chunk == 0
    return [pl.ds(r * chunk, chunk) for r in range(n_rows // chunk)]


def _tapered_chunks(n_rows, chunk=ROW_CHUNK):
    chunk = min(chunk, n_rows)
    assert n_rows % chunk == 0 and chunk % 2 == 0
    sizes = [chunk] * (n_rows // chunk - 1) + [chunk // 2, chunk // 2]
    starts = [sum(sizes[:i]) for i in range(len(sizes))]
    return list(zip(starts, sizes))


def _rmsnorm_kernel(x_ref, g_ref, o_ref):
    x = x_ref[...]
    ms = jnp.mean(x * x, axis=-1, keepdims=True)
    o_ref[...] = (x * lax.rsqrt(ms + RMS_EPS) * g_ref[...]).astype(o_ref.dtype)


def _rmsnorm(x, g, bm=1024):
    m, d = x.shape
    return pl.pallas_call(
        _rmsnorm_kernel,
        grid=(m // bm,),
        in_specs=[pl.BlockSpec((bm, d), lambda i: (i, 0)),
                  pl.BlockSpec((1, d), lambda i: (0, 0))],
        out_specs=pl.BlockSpec((bm, d), lambda i: (i, 0)),
        out_shape=pltpu.HBM((m, d), BF16),
        compiler_params=_compiler_params(
            1, _nbytes((bm, d), F32) + _nbytes((bm, d), BF16), 2 * _nbytes((bm, d), F32)),
        name="rmsnorm",
    )(x, g.reshape(1, d))


def _side_cast_specs(w_next, n_steps, step_index, col0=0, n_cols=None):
    k = w_next.shape[0]
    n_cols = w_next.shape[1] if n_cols is None else n_cols
    cols = math.gcd(col0, n_cols)
    n_col_blocks = n_cols // cols
    assert n_steps % n_col_blocks == 0
    n_row_blocks = n_steps // n_col_blocks
    assert k % n_row_blocks == 0 and (k // n_row_blocks) % (2 * SUBLANES) == 0 and cols % LANES == 0
    rows = k // n_row_blocks
    first = col0 // cols

    def in_map(*ids):
        t = step_index(*ids)
        return t // n_col_blocks, first + t % n_col_blocks

    def out_map(*ids):
        t = step_index(*ids)
        return t // n_col_blocks, t % n_col_blocks

    return (pl.BlockSpec((rows, cols), in_map), pl.BlockSpec((rows, cols), out_map),
            pltpu.HBM((k, n_cols), BF16), _nbytes((rows, cols), F32) + _nbytes((rows, cols), BF16))


GATE_UP_ROW_CHUNK = 256


def _gate_up_kernel(h_ref, wg_ref, wu_ref, wnext_ref, o_ref, wnext_bf_ref, wgu_bf):
    bn = wg_ref.shape[1]

    @pl.when(pl.program_id(1) == 0)
    def _():
        wgu_bf[:, :bn] = wg_ref[...].astype(BF16)
        wgu_bf[:, bn:] = wu_ref[...].astype(BF16)

    wnext_bf_ref[...] = wnext_ref[...].astype(BF16)
    for r0, rc in _tapered_chunks(h_ref.shape[0], GATE_UP_ROW_CHUNK):
        rows = pl.ds(r0, rc)
        gu = _dot(h_ref[rows, :], wgu_bf[...])
        g, u = gu[:, :bn], gu[:, bn:]
        o_ref[rows, :] = (g * jax.nn.sigmoid(g) * u).astype(o_ref.dtype)


def _gate_up(h, w_gu, w_next, bm=2048, bn=512):
    m, d = h.shape
    f = w_gu.shape[1] // 2
    nj, ni = f // bn, m // bm
    side_in, side_out, side_shape, side_bytes = _side_cast_specs(w_next, nj * ni, lambda j, i: j * ni + i)
    return pl.pallas_call(
        _gate_up_kernel,
        grid=(nj, ni),
        in_specs=[pl.BlockSpec((bm, d), lambda j, i: (i, 0)),
                  pl.BlockSpec((d, bn), lambda j, i: (0, j)),
                  pl.BlockSpec((d, bn), lambda j, i: (0, j + nj)),
                  side_in],
        out_specs=[pl.BlockSpec((bm, bn), lambda j, i: (i, j)), side_out],
        out_shape=[pltpu.HBM((m, f), BF16), side_shape],
        scratch_shapes=[pltpu.VMEM((d, 2 * bn), BF16)],
        compiler_params=_compiler_params(
            2,
            _nbytes((bm, d), BF16) + 2 * _nbytes((d, bn), F32) + _nbytes((bm, bn), BF16) + side_bytes,
            2 * _nbytes((d, bn), BF16) + 6 * _nbytes((ROW_CHUNK, bn), F32)),
        name="gate_up",
    )(h, w_gu, w_gu, w_next)


RESIDENT_CHUNKS = 2


def _matmul_residual_kernel(*refs, scale, with_norm):
    if with_norm:
        a_ref, w_ref, r_ref, g_ref, x_ref, h_ref = refs
    else:
        a_ref, w_ref, r_ref, x_ref = refs
    for rows in _row_chunks(a_ref.shape[0], a_ref.shape[0] // RESIDENT_CHUNKS):
        x = r_ref[rows, :] + scale * _dot(a_ref[rows, :], w_ref[...])
        x_ref[rows, :] = x
        if with_norm:
            ms = jnp.mean(x * x, axis=-1, keepdims=True)
            h_ref[rows, :] = (x * lax.rsqrt(ms + RMS_EPS) * g_ref[...]).astype(h_ref.dtype)


def _matmul_residual(a, w_bf, res, scale, norm_gain=None):
    m, k = a.shape
    n = w_bf.shape[1]
    with_norm = norm_gain is not None

    def resident_bytes(rows):
        return _nbytes((k, n), BF16) + 6 * _nbytes((rows // RESIDENT_CHUNKS, n), F32)

    def pipelined_bytes(rows):
        return (_nbytes((rows, k), BF16) + 2 * _nbytes((rows, n), F32)
                + (_nbytes((rows, n), BF16) if with_norm else 0))

    bm = next(rows for rows in (1024, 512, 256, 128)
              if m % rows == 0
              and (2 * pipelined_bytes(rows) + resident_bytes(rows)) * 9 // 8 <= VMEM_CAP_V7X)
    row_spec = lambda width: pl.BlockSpec((bm, width), lambda i: (i, 0))
    in_specs = [row_spec(k),
                pl.BlockSpec((k, n), lambda i: (0, 0), pipeline_mode=pl.Buffered(1)),
                row_spec(n)]
    operands = [a, w_bf, res]
    out_specs = [row_spec(n)]
    out_shape = [pltpu.HBM((m, n), F32)]
    if with_norm:
        in_specs.append(pl.BlockSpec((1, n), lambda i: (0, 0)))
        operands.append(norm_gain.reshape(1, n))
        out_specs.append(row_spec(n))
        out_shape.append(pltpu.HBM((m, n), BF16))
    outs = pl.pallas_call(
        functools.partial(_matmul_residual_kernel, scale=scale, with_norm=with_norm),
        grid=(m // bm,),
        in_specs=in_specs,
        out_specs=out_specs,
        out_shape=out_shape,
        compiler_params=_compiler_params(1, pipelined_bytes(bm), resident_bytes(bm)),
        name="matmul_residual",
    )(*operands)
    return outs if with_norm else (outs[0], None)


def _conv_branch_kernel(h_ref, wx_ref, wb_ref, wc_ref, cw_ref, wnext_ref, o_ref, wnext_bf_ref,
                        wx_bf, wb_bf, wc_bf, u_pad):
    seq = h_ref.shape[0]
    wnext_bf_ref[...] = wnext_ref[...].astype(BF16)

    @pl.when(pl.program_id(1) == 0)
    def _():
        wx_bf[...] = wx_ref[...].astype(BF16)
        wb_bf[...] = wb_ref[...].astype(BF16)
        wc_bf[...] = wc_ref[...].astype(BF16)
        u_pad[0:SUBLANES, :] = jnp.zeros((SUBLANES, u_pad.shape[1]), F32)

    cw = cw_ref[...]
    for r0, rc in _tapered_chunks(seq):
        h = h_ref[r0:r0 + rc, :]
        u = _dot(h, wc_bf[...]) * _dot(h, wx_bf[...])
        p0 = SUBLANES + r0
        u_pad[p0:p0 + rc, :] = u
        y = (cw[0:1, :] * u_pad[p0 - 2:p0 - 2 + rc, :]
             + cw[1:2, :] * u_pad[p0 - 1:p0 - 1 + rc, :])
        y = y + cw[2:3, :] * u
        o_ref[r0:r0 + rc, :] = (_dot(h, wb_bf[...]) * y).astype(o_ref.dtype)


def _conv_branch(h, w_in, conv_w, batch, seq, width, next_col0, next_cols, bn=256):
    m, d = h.shape
    nj = width // bn
    side_in, side_out, side_shape, side_bytes = _side_cast_specs(
        w_in, nj * batch, lambda j, b: j * batch + b, next_col0, next_cols)
    return pl.pallas_call(
        _conv_branch_kernel,
        grid=(nj, batch),
        in_specs=[pl.BlockSpec((seq, d), lambda j, b: (b, 0)),
                  pl.BlockSpec((d, bn), lambda j, b: (0, j)),
                  pl.BlockSpec((d, bn), lambda j, b: (0, j + nj)),
                  pl.BlockSpec((d, bn), lambda j, b: (0, j + 2 * nj)),
                  pl.BlockSpec((CONV_K, bn), lambda j, b: (0, j)),
                  side_in],
        out_specs=[pl.BlockSpec((seq, bn), lambda j, b: (b, j)), side_out],
        out_shape=[pltpu.HBM((m, width), BF16), side_shape],
        scratch_shapes=[pltpu.VMEM((d, bn), BF16)] * 3
        + [pltpu.VMEM((SUBLANES + seq, bn), F32)],
        compiler_params=_compiler_params(
            2,
            _nbytes((seq, d), BF16) + 3 * _nbytes((d, bn), F32) + _nbytes((seq, bn), BF16) + side_bytes,
            3 * _nbytes((d, bn), BF16) + _nbytes((seq, bn), F32) + 8 * _nbytes((ROW_CHUNK, bn), F32)),
        name="conv_branch",
    )(h, w_in, w_in, w_in, conv_w, w_in)


def _head_norm_rope(x, gain, rope):
    width = x.shape[1]
    cos, sin_hi, sin_lo = (rope[:, i * LANES:(i + 1) * LANES] for i in range(3))
    r = lax.broadcasted_iota(jnp.int32, (MXU_DIM, MXU_DIM), 0) // HEAD_DIM
    c = lax.broadcasted_iota(jnp.int32, (MXU_DIM, MXU_DIM), 1) // HEAD_DIM
    ones_blockdiag = (r == c).astype(BF16)
    out = []
    for t in range(width // MXU_DIM):
        xt = x[:, t * MXU_DIM:(t + 1) * MXU_DIM]
        ss = _dot((xt * xt).astype(BF16), ones_blockdiag)
        y = xt * lax.rsqrt(ss * (1.0 / HEAD_DIM) + RMS_EPS) * gain[:, t * MXU_DIM:(t + 1) * MXU_DIM]
        for s in range(MXU_DIM // LANES):
            ys = y[:, s * LANES:(s + 1) * LANES]
            half = ROT_DIM // 2
            rot = (ys * cos
                   + pltpu.roll(ys, LANES - half, 1) * sin_hi
                   + pltpu.roll(ys, half, 1) * sin_lo)
            out.append(rot)
    return jnp.concatenate(out, axis=1)


def _qkv_kernel(h_ref, w_ref, gain_ref, raw_ref, rope_ref, o_ref, w_bf):
    @pl.when(pl.program_id(1) == 0)
    def _():
        w_bf[...] = w_ref[...].astype(BF16)

    gain = gain_ref[...]
    is_raw = raw_ref[...] != 0.0
    seq = h_ref.shape[0]
    for r0, rc in _tapered_chunks(seq):
        acc = _dot(h_ref[r0:r0 + rc, :], w_bf[...])
        y = _head_norm_rope(acc, gain, rope_ref[r0:r0 + rc, :])
        o_ref[r0:r0 + rc, :] = jnp.where(is_raw, acc, y).astype(o_ref.dtype)


def _qkv_proj(h, w_in, gains, raw_cols, rope, batch, seq, q_col, bn):
    m, d = h.shape
    assert q_col % bn == 0
    nj = gains.shape[0]
    first = q_col // bn
    table_spec = pl.BlockSpec((seq, 3 * LANES), lambda j, b: (0, 0))
    col_spec = pl.BlockSpec((None, 1, bn), lambda j, b: (j, 0, 0))
    return pl.pallas_call(
        _qkv_kernel,
        grid=(nj, batch),
        in_specs=[pl.BlockSpec((seq, d), lambda j, b: (b, 0)),
                  pl.BlockSpec((d, bn), lambda j, b: (0, first + j)),
                  col_spec, col_spec, table_spec],
        out_specs=pl.BlockSpec((seq, bn), lambda j, b: (b, j)),
        out_shape=pltpu.HBM((m, nj * bn), BF16),
        scratch_shapes=[pltpu.VMEM((d, bn), BF16)],
        compiler_params=_compiler_params(
            2,
            _nbytes((seq, d), BF16) + _nbytes((d, bn), F32) + _nbytes((seq, bn), BF16)
            + _nbytes((seq, 3 * LANES), F32),
            _nbytes((d, bn), BF16) + 10 * _nbytes((ROW_CHUNK, bn), F32)),
        name="qkv_proj",
    )(h, w_in, gains, raw_cols, rope)


ATTN_BLOCKS_PER_ITER = 4


def _attention_kernel(sink_ref, q_ref, kv_ref, o_ref, k_lo, k_hi, v_lo, v_hi, bias, s_scr, p_scr, inv_scr,
                      *, n_kv_heads):
    seq = q_ref.shape[0]
    kw = n_kv_heads * HEAD_DIM
    two_w = 2 * WINDOW
    assert LANES == 2 * HEAD_DIM and GROUP == 4

    lane = lax.broadcasted_iota(jnp.int32, (seq, LANES), 1)
    zero_block = jnp.zeros((WINDOW, LANES), BF16)
    for src0, lo_ref, hi_ref in ((0, k_lo, k_hi), (kw, v_lo, v_hi)):
        for grp in range(kw // LANES):
            x = kv_ref[:, src0 + grp * LANES:src0 + (grp + 1) * LANES]
            swapped = pltpu.roll(x, HEAD_DIM, 1)
            zero = jnp.zeros_like(x)
            even_lo, odd_lo = jnp.where(lane < HEAD_DIM, x, zero), jnp.where(lane < HEAD_DIM, swapped, zero)
            even_hi, odd_hi = jnp.where(lane >= HEAD_DIM, swapped, zero), jnp.where(lane >= HEAD_DIM, x, zero)
            for hk, lo, hi in ((2 * grp, even_lo, even_hi), (2 * grp + 1, odd_lo, odd_hi)):
                lo_ref[hk, 0:WINDOW, :] = zero_block
                hi_ref[hk, 0:WINDOW, :] = zero_block
                lo_ref[hk, WINDOW:WINDOW + seq, :] = lo
                hi_ref[hk, WINDOW:WINDOW + seq, :] = hi

    @pl.when(pl.program_id(0) == 0)
    def _():
        key = lax.broadcasted_iota(jnp.int32, (2 * two_w, two_w), 0) % two_w
        qry = lax.broadcasted_iota(jnp.int32, (2 * two_w, two_w), 1) % WINDOW
        diff = qry + WINDOW - key
        in_window = (diff >= 0) & (diff < WINDOW)
        bias[0] = jnp.where(in_window & (key >= WINDOW), 0.0, NEG_INF)
        bias[1] = jnp.where(in_window, 0.0, NEG_INF)

    n_slabs = s_scr.shape[0]
    blocks_per_iter = n_slabs // n_kv_heads

    def blocks(i, carry):
        items = [(u * n_kv_heads + hk, hk, i * blocks_per_iter + u)
                 for u in range(blocks_per_iter) for hk in range(n_kv_heads)]
        for slab, hk, n in items:
            r0 = pl.multiple_of(n * WINDOW, WINDOW)
            c0 = hk * GROUP * HEAD_DIM
            q_pairs = jnp.concatenate([q_ref[pl.ds(r0, WINDOW), c0:c0 + LANES],
                                       q_ref[pl.ds(r0, WINDOW), c0 + LANES:c0 + 2 * LANES]], axis=0)
            keys = jnp.concatenate([k_lo[hk, pl.ds(r0, two_w), :], k_hi[hk, pl.ds(r0, two_w), :]], axis=0)
            s_scr[slab] = lax.dot_general(keys, q_pairs, (((1,), (1,)), ((), ())), preferred_element_type=F32)
        for slab, hk, n in items:
            t = jnp.minimum(n, 1)
            for half in range(2):
                for pair in range(2):
                    rows = slice(half * two_w, (half + 1) * two_w)
                    cols = slice(pair * WINDOW, (pair + 1) * WINDOW)
                    s_h = s_scr[slab, rows, cols] + bias[t, rows, cols]
                    sink = LOG2_E * sink_ref[hk * GROUP + 2 * pair + half]
                    mx = jnp.maximum(jnp.max(s_h, axis=0, keepdims=True), sink)
                    p = jnp.exp2(s_h - mx)
                    inv = 1.0 / (jnp.sum(p, axis=0, keepdims=True) + jnp.exp2(sink - mx))
                    inv_scr[2 * slab + half, :, cols] = jnp.broadcast_to(inv, (SUBLANES, WINDOW))
                    p_scr[slab, rows, cols] = p.astype(BF16)
        for slab, hk, n in items:
            r0 = pl.multiple_of(n * WINDOW, WINDOW)
            c0 = hk * GROUP * HEAD_DIM
            vals = jnp.concatenate([v_lo[hk, pl.ds(r0, two_w), :], v_hi[hk, pl.ds(r0, two_w), :]], axis=0)
            o_t = lax.dot_general(vals, p_scr[slab], (((0,), (0,)), ((), ())), preferred_element_type=F32)
            o_t = jnp.concatenate([o_t[0:HEAD_DIM, :] * inv_scr[2 * slab, 0:1, :],
                                   o_t[HEAD_DIM:LANES, :] * inv_scr[2 * slab + 1, 0:1, :]], axis=0)
            for pair in range(2):
                o_ref[pl.ds(r0, WINDOW), c0 + pair * LANES:c0 + (pair + 1) * LANES] = (
                    o_t[:, pair * WINDOW:(pair + 1) * WINDOW].T.astype(o_ref.dtype))
        return carry

    assert (seq // WINDOW) % blocks_per_iter == 0
    lax.fori_loop(0, seq // WINDOW // blocks_per_iter, blocks, 0)


def _attention(qkv, sinks, batch, seq, q_width, kv_width):
    m = qkv.shape[0]
    assert q_width % (2 * kv_width) == 0
    n_kv_heads = kv_width // HEAD_DIM
    n_slabs = ATTN_BLOCKS_PER_ITER * n_kv_heads
    return pl.pallas_call(
        functools.partial(_attention_kernel, n_kv_heads=n_kv_heads),
        grid=(batch,),
        in_specs=[pl.BlockSpec(memory_space=pltpu.SMEM),
                  pl.BlockSpec((seq, q_width), lambda b: (b, 0)),
                  pl.BlockSpec((seq, 2 * kv_width), lambda b: (b, q_width // (2 * kv_width)))],
        out_specs=pl.BlockSpec((seq, q_width), lambda b: (b, 0)),
        out_shape=pltpu.HBM((m, q_width), BF16),
        scratch_shapes=[pltpu.VMEM((n_kv_heads, WINDOW + seq, LANES), BF16)] * 4
        + [pltpu.VMEM((2, 4 * WINDOW, 2 * WINDOW), F32),
           pltpu.VMEM((n_slabs, 4 * WINDOW, 2 * WINDOW), F32),
           pltpu.VMEM((n_slabs, 4 * WINDOW, 2 * WINDOW), BF16),
           pltpu.VMEM((2 * n_slabs, SUBLANES, 2 * WINDOW), F32)],
        compiler_params=_compiler_params(
            1,
            2 * _nbytes((seq, q_width), BF16) + _nbytes((seq, 2 * kv_width), BF16),
            4 * _nbytes((n_kv_heads, WINDOW + seq, LANES), BF16)
            + (8 + 2 * n_slabs) * _nbytes((4 * WINDOW, 2 * WINDOW), F32)
            + 4 * _nbytes((seq, LANES), F32)),
        name="swa_attention",
    )(sinks, qkv, qkv)


MERGE_ROW_CHUNK = 128


def _merge_kernel(h_ref, yc_ref, at_ref, wga_ref, wgb_ref, woc_ref, woa_ref, wnext_ref,
                  o_ref, wnext_bf_ref, woc_bf, woa_bf):
    @pl.when(pl.program_id(1) == 0)
    def _():
        woc_bf[...] = woc_ref[...].astype(BF16)
        woa_bf[...] = woa_ref[...].astype(BF16)

    wnext_bf_ref[...] = wnext_ref[...].astype(BF16)
    for rows in _row_chunks(h_ref.shape[0], MERGE_ROW_CHUNK):
        h = h_ref[rows, :]
        ya = _dot(yc_ref[rows, :], woc_bf[...])
        yb = _dot(at_ref[rows, :], woa_bf[...])
        ga = _dot(h, wga_ref[...])
        gb = _dot(h, wgb_ref[...])
        o_ref[rows, :] = (jax.nn.sigmoid(ga) * ya + jax.nn.sigmoid(gb) * yb).astype(o_ref.dtype)


def _merge(h, yconv, attn, w_gates_bf, w_out_conv, w_out_attn, w_next, bm=2048, bn=512):
    m, d = h.shape
    kc, ka = yconv.shape[1], attn.shape[1]
    nj, ni = d // bn, m // bm
    side_in, side_out, side_shape, side_bytes = _side_cast_specs(w_next, nj * ni, lambda j, i: j * ni + i)
    return pl.pallas_call(
        _merge_kernel,
        grid=(nj, ni),
        in_specs=[pl.BlockSpec((bm, d), lambda j, i: (i, 0)),
                  pl.BlockSpec((bm, kc), lambda j, i: (i, 0)),
                  pl.BlockSpec((bm, ka), lambda j, i: (i, 0)),
                  pl.BlockSpec((d, bn), lambda j, i: (0, j)),
                  pl.BlockSpec((d, bn), lambda j, i: (0, nj + j)),
                  pl.BlockSpec((kc, bn), lambda j, i: (0, j)),
                  pl.BlockSpec((ka, bn), lambda j, i: (0, j)),
                  side_in],
        out_specs=[pl.BlockSpec((bm, bn), lambda j, i: (i, j)), side_out],
        out_shape=[pltpu.HBM((m, d), BF16), side_shape],
        scratch_shapes=[pltpu.VMEM((kc, bn), BF16), pltpu.VMEM((ka, bn), BF16)],
        compiler_params=_compiler_params(
            2,
            _nbytes((bm, d + kc + ka), BF16) + _nbytes((2 * d, bn), BF16) + _nbytes((kc + ka, bn), F32)
            + _nbytes((bm, bn), BF16) + side_bytes,
            _nbytes((kc + ka, bn), BF16) + 10 * _nbytes((MERGE_ROW_CHUNK, bn), F32)),
        name="gated_merge",
    )(h, yconv, attn, w_gates_bf, w_gates_bf, w_out_conv, w_out_attn, w_next)


def _rope_tables(seq):
    half = ROT_DIM // 2
    inv_freq = (1.0 / (np.float32(ROPE_THETA) ** (np.arange(0, ROT_DIM, 2, dtype=np.float32) / ROT_DIM))
                ).astype(np.float32)
    ang = (np.arange(seq, dtype=np.float32)[:, None] * inv_freq[None, :]).astype(np.float64)
    cos, sin = np.cos(ang), np.sin(ang)
    ones = np.ones((seq, HEAD_DIM - ROT_DIM))
    zeros, zeros_tail = np.zeros((seq, half)), np.zeros((seq, HEAD_DIM - ROT_DIM))
    per_head = [np.concatenate([cos, cos, ones], axis=1),
                np.concatenate([-sin, zeros, zeros_tail], axis=1),
                np.concatenate([zeros, sin, zeros_tail], axis=1)]
    table = np.concatenate([np.tile(t, (1, LANES // HEAD_DIM)) for t in per_head], axis=1)
    return jnp.asarray(table.astype(np.float32))


def _swiglu_half_step(x, h, w_gu, w_down, next_gain):
    act, w_down_bf = _gate_up(h, w_gu, w_down)
    return _matmul_residual(act, w_down_bf, x, 0.5, next_gain)


def kernel(x, g_ffn1, w_gu1, w_down1, g_mix, w_in, conv_w, q_norm_g, k_norm_g, sinks,
           w_out_conv, w_out_attn, w_o, g_ffn2, w_gu2, w_down2):
    batch, seq, d = x.shape
    depth = w_in.shape[0]
    conv_width = conv_w.shape[2]
    q_width = w_out_attn.shape[1]
    kv_width = q_width // GROUP
    q_col = 3 * conv_width
    gate_col = q_col + q_width + 2 * kv_width
    rope = _rope_tables(seq)

    xf = x.reshape(batch * seq, d)
    h = _rmsnorm(xf, g_ffn1[0])
    for l in range(depth):
        xf, h = _swiglu_half_step(xf, h, w_gu1[l], w_down1[l], g_mix[l])

        yconv, w_gates_bf = _conv_branch(h, w_in[l], conv_w[l], batch, seq, conv_width, gate_col, 2 * d)
        qkv_bn = 2 * kv_width
        assert q_width % qkv_bn == 0
        gains = jnp.concatenate(
            [jnp.tile(q_norm_g[l] * (ATTN_SCALE * LOG2_E), q_width // HEAD_DIM),
             jnp.tile(k_norm_g[l], kv_width // HEAD_DIM), jnp.ones((kv_width,), F32)]
        ).reshape(-1, 1, qkv_bn)
        raw_cols = jnp.concatenate(
            [jnp.zeros((q_width + kv_width,), F32), jnp.ones((kv_width,), F32)]
        ).reshape(-1, 1, qkv_bn)
        qkv = _qkv_proj(h, w_in[l], gains, raw_cols, rope, batch, seq, q_col, bn=qkv_bn)
        attn = _attention(qkv, sinks[l], batch, seq, q_width, kv_width)
        merged, w_o_bf = _merge(h, yconv, attn, w_gates_bf, w_out_conv[l], w_out_attn[l], w_o[l])
        xf, h = _matmul_residual(merged, w_o_bf, xf, 1.0, g_ffn2[l])

        next_gain = g_ffn1[l + 1] if l + 1 < depth else None
        xf, h = _swiglu_half_step(xf, h, w_gu2[l], w_down2[l], next_gain)
    return xf.reshape(batch, seq, d)
```

```python
import functools
import math

import jax
import jax.numpy as jnp
import numpy as np
from jax import lax
from jax.experimental import pallas as pl
from jax.experimental.pallas import tpu as pltpu

F32 = jnp.float32
BF16 = jnp.bfloat16

HEAD_DIM = 64
GROUP = 4
CONV_K = 3
WINDOW = 128
ROT_DIM = HEAD_DIM // 4
ROPE_THETA = 500000.0
RMS_EPS = 1e-6
ATTN_SCALE = HEAD_DIM ** -0.5
LOG2_E = 1.4426950408889634
NEG_INF = -1e30

LANES = 128
SUBLANES = 8
MXU_DIM = 256
VMEM_CAP_V7X = 60 * 1024 * 1024


def _nbytes(shape, dtype):
    n = 1
    for s in shape:
        n *= s
    return n * jnp.dtype(dtype).itemsize


def _compiler_params(n_grid_axes, pipelined_bytes, resident_bytes):
    need = 2 * pipelined_bytes + resident_bytes
    return pltpu.CompilerParams(
        dimension_semantics=("arbitrary",) * n_grid_axes,
        vmem_limit_bytes=min(VMEM_CAP_V7X, need + need // 8),
    )


def _dot(a, b):
    return jnp.dot(a, b, preferred_element_type=F32)


ROW_CHUNK = 1024


def _row_chunks(n_rows, chunk=ROW_CHUNK):
    chunk = min(chunk, n_rows)
    assert n_rows % chunk == 0
    return [pl.ds(r * chunk, chunk) for r in range(n_rows // chunk)]


def _tapered_chunks(n_rows, chunk=ROW_CHUNK):
    chunk = min(chunk, n_rows)
    assert n_rows % chunk == 0 and chunk % 2 == 0
    sizes = [chunk] * (n_rows // chunk - 1) + [chunk // 2, chunk // 2]
    starts = [sum(sizes[:i]) for i in range(len(sizes))]
    return list(zip(starts, sizes))


def _rmsnorm_kernel(x_ref, g_ref, o_ref):
    x = x_ref[...]
    ms = jnp.mean(x * x, axis=-1, keepdims=True)
    o_ref[...] = (x * lax.rsqrt(ms + RMS_EPS) * g_ref[...]).astype(o_ref.dtype)


def _rmsnorm(x, g, bm=1024):
    m, d = x.shape
    return pl.pallas_call(
        _rmsnorm_kernel,
        grid=(m // bm,),
        in_specs=[pl.BlockSpec((bm, d), lambda i: (i, 0)),
                  pl.BlockSpec((1, d), lambda i: (0, 0))],
        out_specs=pl.BlockSpec((bm, d), lambda i: (i, 0)),
        out_shape=pltpu.HBM((m, d), BF16),
        compiler_params=_compiler_params(
            1, _nbytes((bm, d), F32) + _nbytes((bm, d), BF16), 2 * _nbytes((bm, d), F32)),
        name="rmsnorm",
    )(x, g.reshape(1, d))


def _side_cast_specs(w_next, n_steps, step_index, col0=0, n_cols=None):
    k = w_next.shape[0]
    n_cols = w_next.shape[1] if n_cols is None else n_cols
    cols = math.gcd(col0, n_cols)
    n_col_blocks = n_cols // cols
    assert n_steps % n_col_blocks == 0
    n_row_blocks = n_steps // n_col_blocks
    assert k % n_row_blocks == 0 and (k // n_row_blocks) % (2 * SUBLANES) == 0 and cols % LANES == 0
    rows = k // n_row_blocks
    first = col0 // cols

    def in_map(*ids):
        t = step_index(*ids)
        return t // n_col_blocks, first + t % n_col_blocks

    def out_map(*ids):
        t = step_index(*ids)
        return t // n_col_blocks, t % n_col_blocks

    return (pl.BlockSpec((rows, cols), in_map), pl.BlockSpec((rows, cols), out_map),
            pltpu.HBM((k, n_cols), BF16), _nbytes((rows, cols), F32) + _nbytes((rows, cols), BF16))


GATE_UP_ROW_CHUNK = 256


def _gate_up_kernel(h_ref, wg_ref, wu_ref, wnext_ref, o_ref, wnext_bf_ref, wgu_bf):
    bn = wg_ref.shape[1]

    @pl.when(pl.program_id(1) == 0)
    def _():
        wgu_bf[:, :bn] = wg_ref[...].astype(BF16)
        wgu_bf[:, bn:] = wu_ref[...].astype(BF16)

    wnext_bf_ref[...] = wnext_ref[...].astype(BF16)
    for r0, rc in _tapered_chunks(h_ref.shape[0], GATE_UP_ROW_CHUNK):
        rows = pl.ds(r0, rc)
        gu = _dot(h_ref[rows, :], wgu_bf[...])
        g, u = gu[:, :bn], gu[:, bn:]
        o_ref[rows, :] = (g * jax.nn.sigmoid(g) * u).astype(o_ref.dtype)


def _gate_up(h, w_gu, w_next, bm=2048, bn=512):
    m, d = h.shape
    f = w_gu.shape[1] // 2
    nj, ni = f // bn, m // bm
    side_in, side_out, side_shape, side_bytes = _side_cast_specs(w_next, nj * ni, lambda j, i: j * ni + i)
    return pl.pallas_call(
        _gate_up_kernel,
        grid=(nj, ni),
        in_specs=[pl.BlockSpec((bm, d), lambda j, i: (i, 0)),
                  pl.BlockSpec((d, bn), lambda j, i: (0, j)),
                  pl.BlockSpec((d, bn), lambda j, i: (0, j + nj)),
                  side_in],
        out_specs=[pl.BlockSpec((bm, bn), lambda j, i: (i, j)), side_out],
        out_shape=[pltpu.HBM((m, f), BF16), side_shape],
        scratch_shapes=[pltpu.VMEM((d, 2 * bn), BF16)],
        compiler_params=_compiler_params(
            2,
            _nbytes((bm, d), BF16) + 2 * _nbytes((d, bn), F32) + _nbytes((bm, bn), BF16) + side_bytes,
            2 * _nbytes((d, bn), BF16) + 6 * _nbytes((ROW_CHUNK, bn), F32)),
        name="gate_up",
    )(h, w_gu, w_gu, w_next)


RESIDENT_ROW_CHUNK = 128


def _matmul_residual_kernel(*refs, scale, with_norm):
    if with_norm:
        a_ref, w_ref, r_ref, g_ref, x_ref, h_ref = refs
    else:
        a_ref, w_ref, r_ref, x_ref = refs
    for rows in _row_chunks(a_ref.shape[0], RESIDENT_ROW_CHUNK):
        x = r_ref[rows, :] + scale * _dot(a_ref[rows, :], w_ref[...])
        x_ref[rows, :] = x
        if with_norm:
            ms = jnp.mean(x * x, axis=-1, keepdims=True)
            h_ref[rows, :] = (x * lax.rsqrt(ms + RMS_EPS) * g_ref[...]).astype(h_ref.dtype)


def _matmul_residual(a, w_bf, res, scale, norm_gain=None):
    m, k = a.shape
    n = w_bf.shape[1]
    with_norm = norm_gain is not None

    def resident_bytes(rows):
        return _nbytes((k, n), BF16) + 6 * _nbytes((RESIDENT_ROW_CHUNK, n), F32)

    def pipelined_bytes(rows):
        return (_nbytes((rows, k), BF16) + 2 * _nbytes((rows, n), F32)
                + (_nbytes((rows, n), BF16) if with_norm else 0))

    bm = next(rows for rows in (1024, 512, 256, 128)
              if m % rows == 0
              and (2 * pipelined_bytes(rows) + resident_bytes(rows)) * 9 // 8 <= VMEM_CAP_V7X)
    row_spec = lambda width: pl.BlockSpec((bm, width), lambda i: (i, 0))
    in_specs = [row_spec(k),
                pl.BlockSpec((k, n), lambda i: (0, 0), pipeline_mode=pl.Buffered(1)),
                row_spec(n)]
    operands = [a, w_bf, res]
    out_specs = [row_spec(n)]
    out_shape = [pltpu.HBM((m, n), F32)]
    if with_norm:
        in_specs.append(pl.BlockSpec((1, n), lambda i: (0, 0)))
        operands.append(norm_gain.reshape(1, n))
        out_specs.append(row_spec(n))
        out_shape.append(pltpu.HBM((m, n), BF16))
    outs = pl.pallas_call(
        functools.partial(_matmul_residual_kernel, scale=scale, with_norm=with_norm),
        grid=(m // bm,),
        in_specs=in_specs,
        out_specs=out_specs,
        out_shape=out_shape,
        compiler_params=_compiler_params(1, pipelined_bytes(bm), resident_bytes(bm)),
        name="matmul_residual",
    )(*operands)
    return outs if with_norm else (outs[0], None)


def _conv_branch_kernel(h_ref, wx_ref, wb_ref, wc_ref, cw_ref, wnext_ref, o_ref, wnext_bf_ref,
                        wx_bf, wb_bf, wc_bf, u_pad):
    seq = h_ref.shape[0]
    wnext_bf_ref[...] = wnext_ref[...].astype(BF16)

    @pl.when(pl.program_id(1) == 0)
    def _():
        wx_bf[...] = wx_ref[...].astype(BF16)
        wb_bf[...] = wb_ref[...].astype(BF16)
        wc_bf[...] = wc_ref[...].astype(BF16)
        u_pad[0:SUBLANES, :] = jnp.zeros((SUBLANES, u_pad.shape[1]), F32)

    cw = cw_ref[...]
    for r0, rc in _tapered_chunks(seq):
        h = h_ref[r0:r0 + rc, :]
        u = _dot(h, wc_bf[...]) * _dot(h, wx_bf[...])
        p0 = SUBLANES + r0
        u_pad[p0:p0 + rc, :] = u
        y = (cw[0:1, :] * u_pad[p0 - 2:p0 - 2 + rc, :]
             + cw[1:2, :] * u_pad[p0 - 1:p0 - 1 + rc, :])
        y = y + cw[2:3, :] * u
        o_ref[r0:r0 + rc, :] = (_dot(h, wb_bf[...]) * y).astype(o_ref.dtype)


def _conv_branch(h, w_in, conv_w, batch, seq, width, next_col0, next_cols, bn=256):
    m, d = h.shape
    nj = width // bn
    side_in, side_out, side_shape, side_bytes = _side_cast_specs(
        w_in, nj * batch, lambda j, b: j * batch + b, next_col0, next_cols)
    return pl.pallas_call(
        _conv_branch_kernel,
        grid=(nj, batch),
        in_specs=[pl.BlockSpec((seq, d), lambda j, b: (b, 0)),
                  pl.BlockSpec((d, bn), lambda j, b: (0, j)),
                  pl.BlockSpec((d, bn), lambda j, b: (0, j + nj)),
                  pl.BlockSpec((d, bn), lambda j, b: (0, j + 2 * nj)),
                  pl.BlockSpec((CONV_K, bn), lambda j, b: (0, j)),
                  side_in],
        out_specs=[pl.BlockSpec((seq, bn), lambda j, b: (b, j)), side_out],
        out_shape=[pltpu.HBM((m, width), BF16), side_shape],
        scratch_shapes=[pltpu.VMEM((d, bn), BF16)] * 3
        + [pltpu.VMEM((SUBLANES + seq, bn), F32)],
        compiler_params=_compiler_params(
            2,
            _nbytes((seq, d), BF16) + 3 * _nbytes((d, bn), F32) + _nbytes((seq, bn), BF16) + side_bytes,
            3 * _nbytes((d, bn), BF16) + _nbytes((seq, bn), F32) + 8 * _nbytes((ROW_CHUNK, bn), F32)),
        name="conv_branch",
    )(h, w_in, w_in, w_in, conv_w, w_in)


def _head_norm_rope(x, gain, rope):
    width = x.shape[1]
    cos, sin_hi, sin_lo = (rope[:, i * LANES:(i + 1) * LANES] for i in range(3))
    r = lax.broadcasted_iota(jnp.int32, (MXU_DIM, MXU_DIM), 0) // HEAD_DIM
    c = lax.broadcasted_iota(jnp.int32, (MXU_DIM, MXU_DIM), 1) // HEAD_DIM
    ones_blockdiag = (r == c).astype(BF16)
    out = []
    for t in range(width // MXU_DIM):
        xt = x[:, t * MXU_DIM:(t + 1) * MXU_DIM]
        ss = _dot((xt * xt).astype(BF16), ones_blockdiag)
        y = xt * lax.rsqrt(ss * (1.0 / HEAD_DIM) + RMS_EPS) * gain[:, t * MXU_DIM:(t + 1) * MXU_DIM]
        for s in range(MXU_DIM // LANES):
            ys = y[:, s * LANES:(s + 1) * LANES]
            half = ROT_DIM // 2
            rot = (ys * cos
                   + pltpu.roll(ys, LANES - half, 1) * sin_hi
                   + pltpu.roll(ys, half, 1) * sin_lo)
            out.append(rot)
    return jnp.concatenate(out, axis=1)


def _qkv_kernel(h_ref, w_ref, gain_ref, raw_ref, rope_ref, o_ref, w_bf):
    @pl.when(pl.program_id(1) == 0)
    def _():
        w_bf[...] = w_ref[...].astype(BF16)

    gain = gain_ref[...]
    is_raw = raw_ref[...] != 0.0
    seq = h_ref.shape[0]
    for r0, rc in _tapered_chunks(seq):
        acc = _dot(h_ref[r0:r0 + rc, :], w_bf[...])
        y = _head_norm_rope(acc, gain, rope_ref[r0:r0 + rc, :])
        o_ref[r0:r0 + rc, :] = jnp.where(is_raw, acc, y).astype(o_ref.dtype)


def _qkv_proj(h, w_in, gains, raw_cols, rope, batch, seq, q_col, bn):
    m, d = h.shape
    assert q_col % bn == 0
    nj = gains.shape[0]
    first = q_col // bn
    table_spec = pl.BlockSpec((seq, 3 * LANES), lambda j, b: (0, 0))
    col_spec = pl.BlockSpec((None, 1, bn), lambda j, b: (j, 0, 0))
    return pl.pallas_call(
        _qkv_kernel,
        grid=(nj, batch),
        in_specs=[pl.BlockSpec((seq, d), lambda j, b: (b, 0)),
                  pl.BlockSpec((d, bn), lambda j, b: (0, first + j)),
                  col_spec, col_spec, table_spec],
        out_specs=pl.BlockSpec((seq, bn), lambda j, b: (b, j)),
        out_shape=pltpu.HBM((m, nj * bn), BF16),
        scratch_shapes=[pltpu.VMEM((d, bn), BF16)],
        compiler_params=_compiler_params(
            2,
            _nbytes((seq, d), BF16) + _nbytes((d, bn), F32) + _nbytes((seq, bn), BF16)
            + _nbytes((seq, 3 * LANES), F32),
            _nbytes((d, bn), BF16) + 10 * _nbytes((ROW_CHUNK, bn), F32)),
        name="qkv_proj",
    )(h, w_in, gains, raw_cols, rope)


ATTN_BLOCKS_PER_ITER = 4


def _attention_kernel(sink_ref, q_ref, kv_ref, o_ref, k_lo, k_hi, v_lo, v_hi, bias, s_scr, p_scr, inv_scr,
                      *, n_kv_heads):
    seq = q_ref.shape[0]
    kw = n_kv_heads * HEAD_DIM
    two_w = 2 * WINDOW
    assert LANES == 2 * HEAD_DIM and GROUP == 4

    lane = lax.broadcasted_iota(jnp.int32, (seq, LANES), 1)
    zero_block = jnp.zeros((WINDOW, LANES), BF16)
    for src0, lo_ref, hi_ref in ((0, k_lo, k_hi), (kw, v_lo, v_hi)):
        for grp in range(kw // LANES):
            x = kv_ref[:, src0 + grp * LANES:src0 + (grp + 1) * LANES]
            swapped = pltpu.roll(x, HEAD_DIM, 1)
            zero = jnp.zeros_like(x)
            even_lo, odd_lo = jnp.where(lane < HEAD_DIM, x, zero), jnp.where(lane < HEAD_DIM, swapped, zero)
            even_hi, odd_hi = jnp.where(lane >= HEAD_DIM, swapped, zero), jnp.where(lane >= HEAD_DIM, x, zero)
            for hk, lo, hi in ((2 * grp, even_lo, even_hi), (2 * grp + 1, odd_lo, odd_hi)):
                lo_ref[hk, 0:WINDOW, :] = zero_block
                hi_ref[hk, 0:WINDOW, :] = zero_block
                lo_ref[hk, WINDOW:WINDOW + seq, :] = lo
                hi_ref[hk, WINDOW:WINDOW + seq, :] = hi

    @pl.when(pl.program_id(0) == 0)
    def _():
        key = lax.broadcasted_iota(jnp.int32, (2 * two_w, two_w), 0) % two_w
        qry = lax.broadcasted_iota(jnp.int32, (2 * two_w, two_w), 1) % WINDOW
        diff = qry + WINDOW - key
        in_window = (diff >= 0) & (diff < WINDOW)
        bias[0] = jnp.where(in_window & (key >= WINDOW), 0.0, NEG_INF)
        bias[1] = jnp.where(in_window, 0.0, NEG_INF)

    n_slabs = s_scr.shape[0]
    blocks_per_iter = n_slabs // n_kv_heads

    def blocks(i, carry):
        items = [(u * n_kv_heads + hk, hk, i * blocks_per_iter + u)
                 for u in range(blocks_per_iter) for hk in range(n_kv_heads)]
        for slab, hk, n in items:
            r0 = pl.multiple_of(n * WINDOW, WINDOW)
            c0 = hk * GROUP * HEAD_DIM
            q_pairs = jnp.concatenate([q_ref[pl.ds(r0, WINDOW), c0:c0 + LANES],
                                       q_ref[pl.ds(r0, WINDOW), c0 + LANES:c0 + 2 * LANES]], axis=0)
            keys = jnp.concatenate([k_lo[hk, pl.ds(r0, two_w), :], k_hi[hk, pl.ds(r0, two_w), :]], axis=0)
            s_scr[slab] = lax.dot_general(keys, q_pairs, (((1,), (1,)), ((), ())), preferred_element_type=F32)
        for slab, hk, n in items:
            t = jnp.minimum(n, 1)
            for half in range(2):
                for pair in range(2):
                    rows = slice(half * two_w, (half + 1) * two_w)
                    cols = slice(pair * WINDOW, (pair + 1) * WINDOW)
                    s_h = s_scr[slab, rows, cols] + bias[t, rows, cols]
                    sink = LOG2_E * sink_ref[hk * GROUP + 2 * pair + half]
                    mx = jnp.maximum(jnp.max(s_h, axis=0, keepdims=True), sink)
                    p = jnp.exp2(s_h - mx)
                    inv = 1.0 / (jnp.sum(p, axis=0, keepdims=True) + jnp.exp2(sink - mx))
                    inv_scr[2 * slab + half, :, cols] = jnp.broadcast_to(inv, (SUBLANES, WINDOW))
                    p_scr[slab, rows, cols] = p.astype(BF16)
        for slab, hk, n in items:
            r0 = pl.multiple_of(n * WINDOW, WINDOW)
            c0 = hk * GROUP * HEAD_DIM
            vals = jnp.concatenate([v_lo[hk, pl.ds(r0, two_w), :], v_hi[hk, pl.ds(r0, two_w), :]], axis=0)
            o_t = lax.dot_general(vals, p_scr[slab], (((0,), (0,)), ((), ())), preferred_element_type=F32)
            o_t = jnp.concatenate([o_t[0:HEAD_DIM, :] * inv_scr[2 * slab, 0:1, :],
                                   o_t[HEAD_DIM:LANES, :] * inv_scr[2 * slab + 1, 0:1, :]], axis=0)
            for pair in range(2):
                o_ref[pl.ds(r0, WINDOW), c0 + pair * LANES:c0 + (pair + 1) * LANES] = (
                    o_t[:, pair * WINDOW:(pair + 1) * WINDOW].T.astype(o_ref.dtype))
        return carry

    assert (seq // WINDOW) % blocks_per_iter == 0
    lax.fori_loop(0, seq // WINDOW // blocks_per_iter, blocks, 0)


def _attention(qkv, sinks, batch, seq, q_width, kv_width):
    m = qkv.shape[0]
    assert q_width % (2 * kv_width) == 0
    n_kv_heads = kv_width // HEAD_DIM
    n_slabs = ATTN_BLOCKS_PER_ITER * n_kv_heads
    return pl.pallas_call(
        functools.partial(_attention_kernel, n_kv_heads=n_kv_heads),
        grid=(batch,),
        in_specs=[pl.BlockSpec(memory_space=pltpu.SMEM),
                  pl.BlockSpec((seq, q_width), lambda b: (b, 0)),
                  pl.BlockSpec((seq, 2 * kv_width), lambda b: (b, q_width // (2 * kv_width)))],
        out_specs=pl.BlockSpec((seq, q_width), lambda b: (b, 0)),
        out_shape=pltpu.HBM((m, q_width), BF16),
        scratch_shapes=[pltpu.VMEM((n_kv_heads, WINDOW + seq, LANES), BF16)] * 4
        + [pltpu.VMEM((2, 4 * WINDOW, 2 * WINDOW), F32),
           pltpu.VMEM((n_slabs, 4 * WINDOW, 2 * WINDOW), F32),
           pltpu.VMEM((n_slabs, 4 * WINDOW, 2 * WINDOW), BF16),
           pltpu.VMEM((2 * n_slabs, SUBLANES, 2 * WINDOW), F32)],
        compiler_params=_compiler_params(
            1,
            2 * _nbytes((seq, q_width), BF16) + _nbytes((seq, 2 * kv_width), BF16),
            4 * _nbytes((n_kv_heads, WINDOW + seq, LANES), BF16)
            + (8 + 2 * n_slabs) * _nbytes((4 * WINDOW, 2 * WINDOW), F32)
            + 4 * _nbytes((seq, LANES), F32)),
        name="swa_attention",
    )(sinks, qkv, qkv)


MERGE_ROW_CHUNK = 256


def _merge_kernel(h_ref, yc_ref, at_ref, wga_ref, wgb_ref, woc_ref, woa_ref, wnext_ref,
                  o_ref, wnext_bf_ref, woc_bf, woa_bf):
    @pl.when(pl.program_id(1) == 0)
    def _():
        woc_bf[...] = woc_ref[...].astype(BF16)
        woa_bf[...] = woa_ref[...].astype(BF16)

    wnext_bf_ref[...] = wnext_ref[...].astype(BF16)
    for rows in _row_chunks(h_ref.shape[0], MERGE_ROW_CHUNK):
        h = h_ref[rows, :]
        ya = _dot(yc_ref[rows, :], woc_bf[...])
        yb = _dot(at_ref[rows, :], woa_bf[...])
        ga = _dot(h, wga_ref[...])
        gb = _dot(h, wgb_ref[...])
        o_ref[rows, :] = (jax.nn.sigmoid(ga) * ya + jax.nn.sigmoid(gb) * yb).astype(o_ref.dtype)


def _merge(h, yconv, attn, w_gates_bf, w_out_conv, w_out_attn, w_next, bm=2048, bn=512):
    m, d = h.shape
    kc, ka = yconv.shape[1], attn.shape[1]
    nj, ni = d // bn, m // bm
    side_in, side_out, side_shape, side_bytes = _side_cast_specs(w_next, nj * ni, lambda j, i: j * ni + i)
    return pl.pallas_call(
        _merge_kernel,
        grid=(nj, ni),
        in_specs=[pl.BlockSpec((bm, d), lambda j, i: (i, 0)),
                  pl.BlockSpec((bm, kc), lambda j, i: (i, 0)),
                  pl.BlockSpec((bm, ka), lambda j, i: (i, 0)),
                  pl.BlockSpec((d, bn), lambda j, i: (0, j)),
                  pl.BlockSpec((d, bn), lambda j, i: (0, nj + j)),
                  pl.BlockSpec((kc, bn), lambda j, i: (0, j)),
                  pl.BlockSpec((ka, bn), lambda j, i: (0, j)),
                  side_in],
        out_specs=[pl.BlockSpec((bm, bn), lambda j, i: (i, j)), side_out],
        out_shape=[pltpu.HBM((m, d), BF16), side_shape],
        scratch_shapes=[pltpu.VMEM((kc, bn), BF16), pltpu.VMEM((ka, bn), BF16)],
        compiler_params=_compiler_params(
            2,
            _nbytes((bm, d + kc + ka), BF16) + _nbytes((2 * d, bn), BF16) + _nbytes((kc + ka, bn), F32)
            + _nbytes((bm, bn), BF16) + side_bytes,
            _nbytes((kc + ka, bn), BF16) + 10 * _nbytes((MERGE_ROW_CHUNK, bn), F32)),
        name="gated_merge",
    )(h, yconv, attn, w_gates_bf, w_gates_bf, w_out_conv, w_out_attn, w_next)


def _rope_tables(seq):
    half = ROT_DIM // 2
    inv_freq = (1.0 / (np.float32(ROPE_THETA) ** (np.arange(0, ROT_DIM, 2, dtype=np.float32) / ROT_DIM))
                ).astype(np.float32)
    ang = (np.arange(seq, dtype=np.float32)[:, None] * inv_freq[None, :]).astype(np.float64)
    cos, sin = np.cos(ang), np.sin(ang)
    ones = np.ones((seq, HEAD_DIM - ROT_DIM))
    zeros, zeros_tail = np.zeros((seq, half)), np.zeros((seq, HEAD_DIM - ROT_DIM))
    per_head = [np.concatenate([cos, cos, ones], axis=1),
                np.concatenate([-sin, zeros, zeros_tail], axis=1),
                np.concatenate([zeros, sin, zeros_tail], axis=1)]
    table = np.concatenate([np.tile(t, (1, LANES // HEAD_DIM)) for t in per_head], axis=1)
    return jnp.asarray(table.astype(np.float32))


def _swiglu_half_step(x, h, w_gu, w_down, next_gain):
    act, w_down_bf = _gate_up(h, w_gu, w_down)
    return _matmul_residual(act, w_down_bf, x, 0.5, next_gain)


def kernel(x, g_ffn1, w_gu1, w_down1, g_mix, w_in, conv_w, q_norm_g, k_norm_g, sinks,
           w_out_conv, w_out_attn, w_o, g_ffn2, w_gu2, w_down2):
    batch, seq, d = x.shape
    depth = w_in.shape[0]
    conv_width = conv_w.shape[2]
    q_width = w_out_attn.shape[1]
    kv_width = q_width // GROUP
    q_col = 3 * conv_width
    gate_col = q_col + q_width + 2 * kv_width
    rope = _rope_tables(seq)

    xf = x.reshape(batch * seq, d)
    h = _rmsnorm(xf, g_ffn1[0])
    for l in range(depth):
        xf, h = _swiglu_half_step(xf, h, w_gu1[l], w_down1[l], g_mix[l])

        yconv, w_gates_bf = _conv_branch(h, w_in[l], conv_w[l], batch, seq, conv_width, gate_col, 2 * d)
        qkv_bn = 2 * kv_width
        assert q_width % qkv_bn == 0
        gains = jnp.concatenate(
            [jnp.tile(q_norm_g[l] * (ATTN_SCALE * LOG2_E), q_width // HEAD_DIM),
             jnp.tile(k_norm_g[l], kv_width // HEAD_DIM), jnp.ones((kv_width,), F32)]
        ).reshape(-1, 1, qkv_bn)
        raw_cols = jnp.concatenate(
            [jnp.zeros((q_width + kv_width,), F32), jnp.ones((kv_width,), F32)]
        ).reshape(-1, 1, qkv_bn)
        qkv = _qkv_proj(h, w_in[l], gains, raw_cols, rope, batch, seq, q_col, bn=qkv_bn)
        attn = _attention(qkv, sinks[l], batch, seq, q_width, kv_width)
        merged, w_o_bf = _merge(h, yconv, attn, w_gates_bf, w_out_conv[l], w_out_attn[l], w_o[l])
        xf, h = _matmul_residual(merged, w_o_bf, xf, 1.0, g_ffn2[l])

        next_gain = g_ffn1[l + 1] if l + 1 < depth else None
        xf, h = _swiglu_half_step(xf, h, w_gu2[l], w_down2[l], next_gain)
    return xf.reshape(batch, seq, d)
```

```python
import functools
import math

import jax
import jax.numpy as jnp
import numpy as np
from jax import lax
from jax.experimental import pallas as pl
from jax.experimental.pallas import tpu as pltpu

F32 = jnp.float32
BF16 = jnp.bfloat16

HEAD_DIM = 64
GROUP = 4
CONV_K = 3
WINDOW = 128
ROT_DIM = HEAD_DIM // 4
ROPE_THETA = 500000.0
RMS_EPS = 1e-6
ATTN_SCALE = HEAD_DIM ** -0.5
LOG2_E = 1.4426950408889634
NEG_INF = -1e30

LANES = 128
SUBLANES = 8
MXU_DIM = 256
VMEM_CAP_V7X = 60 * 1024 * 1024


def _nbytes(shape, dtype):
    n = 1
    for s in shape:
        n *= s
    return n * jnp.dtype(dtype).itemsize


def _compiler_params(n_grid_axes, pipelined_bytes, resident_bytes):
    need = 2 * pipelined_bytes + resident_bytes
    return pltpu.CompilerParams(
        dimension_semantics=("arbitrary",) * n_grid_axes,
        vmem_limit_bytes=min(VMEM_CAP_V7X, need + need // 8),
    )


def _dot(a, b):
    return jnp.dot(a, b, preferred_element_type=F32)


ROW_CHUNK = 1024


def _row_chunks(n_rows, chunk=ROW_CHUNK):
    chunk = min(chunk, n_rows)
    assert n_rows % chunk == 0
    return [pl.ds(r * chunk, chunk) for r in range(n_rows // chunk)]


def _tapered_chunks(n_rows, chunk=ROW_CHUNK):
    chunk = min(chunk, n_rows)
    assert n_rows % chunk == 0 and chunk % 2 == 0
    sizes = [chunk] * (n_rows // chunk - 1) + [chunk // 2, chunk // 2]
    starts = [sum(sizes[:i]) for i in range(len(sizes))]
    return list(zip(starts, sizes))


def _rmsnorm_kernel(x_ref, g_ref, o_ref):
    x = x_ref[...]
    ms = jnp.mean(x * x, axis=-1, keepdims=True)
    o_ref[...] = (x * lax.rsqrt(ms + RMS_EPS) * g_ref[...]).astype(o_ref.dtype)


def _rmsnorm(x, g, bm=1024):
    m, d = x.shape
    return pl.pallas_call(
        _rmsnorm_kernel,
        grid=(m // bm,),
        in_specs=[pl.BlockSpec((bm, d), lambda i: (i, 0)),
                  pl.BlockSpec((1, d), lambda i: (0, 0))],
        out_specs=pl.BlockSpec((bm, d), lambda i: (i, 0)),
        out_shape=pltpu.HBM((m, d), BF16),
        compiler_params=_compiler_params(
            1, _nbytes((bm, d), F32) + _nbytes((bm, d), BF16), 2 * _nbytes((bm, d), F32)),
        name="rmsnorm",
    )(x, g.reshape(1, d))


def _side_cast_specs(w_next, n_steps, step_index, col0=0, n_cols=None):
    k = w_next.shape[0]
    n_cols = w_next.shape[1] if n_cols is None else n_cols
    cols = math.gcd(col0, n_cols)
    n_col_blocks = n_cols // cols
    assert n_steps % n_col_blocks == 0
    n_row_blocks = n_steps // n_col_blocks
    assert k % n_row_blocks == 0 and (k // n_row_blocks) % (2 * SUBLANES) == 0 and cols % LANES == 0
    rows = k // n_row_blocks
    first = col0 // cols

    def in_map(*ids):
        t = step_index(*ids)
        return t // n_col_blocks, first + t % n_col_blocks

    def out_map(*ids):
        t = step_index(*ids)
        return t // n_col_blocks, t % n_col_blocks

    return (pl.BlockSpec((rows, cols), in_map), pl.BlockSpec((rows, cols), out_map),
            pltpu.HBM((k, n_cols), BF16), _nbytes((rows, cols), F32) + _nbytes((rows, cols), BF16))


GATE_UP_ROW_CHUNK = 256


def _gate_up_kernel(h_ref, wg_ref, wu_ref, wnext_ref, o_ref, wnext_bf_ref, wgu_bf):
    bn = wg_ref.shape[1]

    @pl.when(pl.program_id(1) == 0)
    def _():
        wgu_bf[:, :bn] = wg_ref[...].astype(BF16)
        wgu_bf[:, bn:] = wu_ref[...].astype(BF16)

    wnext_bf_ref[...] = wnext_ref[...].astype(BF16)
    for r0, rc in _tapered_chunks(h_ref.shape[0], GATE_UP_ROW_CHUNK):
        rows = pl.ds(r0, rc)
        gu = _dot(h_ref[rows, :], wgu_bf[...])
        g, u = gu[:, :bn], gu[:, bn:]
        o_ref[rows, :] = (g * jax.nn.sigmoid(g) * u).astype(o_ref.dtype)


def _gate_up(h, w_gu, w_next, bm=2048, bn=512):
    m, d = h.shape
    f = w_gu.shape[1] // 2
    nj, ni = f // bn, m // bm
    side_in, side_out, side_shape, side_bytes = _side_cast_specs(w_next, nj * ni, lambda j, i: j * ni + i)
    return pl.pallas_call(
        _gate_up_kernel,
        grid=(nj, ni),
        in_specs=[pl.BlockSpec((bm, d), lambda j, i: (i, 0)),
                  pl.BlockSpec((d, bn), lambda j, i: (0, j)),
                  pl.BlockSpec((d, bn), lambda j, i: (0, j + nj)),
                  side_in],
        out_specs=[pl.BlockSpec((bm, bn), lambda j, i: (i, j)), side_out],
        out_shape=[pltpu.HBM((m, f), BF16), side_shape],
        scratch_shapes=[pltpu.VMEM((d, 2 * bn), BF16)],
        compiler_params=_compiler_params(
            2,
            _nbytes((bm, d), BF16) + 2 * _nbytes((d, bn), F32) + _nbytes((bm, bn), BF16) + side_bytes,
            2 * _nbytes((d, bn), BF16) + 6 * _nbytes((ROW_CHUNK, bn), F32)),
        name="gate_up",
    )(h, w_gu, w_gu, w_next)


RESIDENT_CHUNKS = 2


def _matmul_residual_kernel(*refs, scale, with_norm):
    if with_norm:
        a_ref, w_ref, r_ref, g_ref, x_ref, h_ref = refs
    else:
        a_ref, w_ref, r_ref, x_ref = refs
    for rows in _row_chunks(a_ref.shape[0], a_ref.shape[0] // RESIDENT_CHUNKS):
        x = r_ref[rows, :] + scale * _dot(a_ref[rows, :], w_ref[...])
        x_ref[rows, :] = x
        if with_norm:
            ms = jnp.mean(x * x, axis=-1, keepdims=True)
            h_ref[rows, :] = (x * lax.rsqrt(ms + RMS_EPS) * g_ref[...]).astype(h_ref.dtype)


def _matmul_residual(a, w_bf, res, scale, norm_gain=None):
    m, k = a.shape
    n = w_bf.shape[1]
    with_norm = norm_gain is not None

    def resident_bytes(rows):
        return _nbytes((k, n), BF16) + 6 * _nbytes((rows // RESIDENT_CHUNKS, n), F32)

    def pipelined_bytes(rows):
        return (_nbytes((rows, k), BF16) + 2 * _nbytes((rows, n), F32)
                + (_nbytes((rows, n), BF16) if with_norm else 0))

    bm = next(rows for rows in (1024, 512, 256, 128)
              if m % rows == 0
              and (2 * pipelined_bytes(rows) + resident_bytes(rows)) * 9 // 8 <= VMEM_CAP_V7X)
    row_spec = lambda width: pl.BlockSpec((bm, width), lambda i: (i, 0))
    in_specs = [row_spec(k),
                pl.BlockSpec((k, n), lambda i: (0, 0), pipeline_mode=pl.Buffered(1)),
                row_spec(n)]
    operands = [a, w_bf, res]
    out_specs = [row_spec(n)]
    out_shape = [pltpu.HBM((m, n), F32)]
    if with_norm:
        in_specs.append(pl.BlockSpec((1, n), lambda i: (0, 0)))
        operands.append(norm_gain.reshape(1, n))
        out_specs.append(row_spec(n))
        out_shape.append(pltpu.HBM((m, n), BF16))
    outs = pl.pallas_call(
        functools.partial(_matmul_residual_kernel, scale=scale, with_norm=with_norm),
        grid=(m // bm,),
        in_specs=in_specs,
        out_specs=out_specs,
        out_shape=out_shape,
        compiler_params=_compiler_params(1, pipelined_bytes(bm), resident_bytes(bm)),
        name="matmul_residual",
    )(*operands)
    return outs if with_norm else (outs[0], None)


def _conv_branch_kernel(h_ref, wx_ref, wb_ref, wc_ref, cw_ref, wnext_ref, o_ref, wnext_bf_ref,
                        wx_bf, wb_bf, wc_bf, u_pad):
    seq = h_ref.shape[0]
    wnext_bf_ref[...] = wnext_ref[...].astype(BF16)

    @pl.when(pl.program_id(1) == 0)
    def _():
        wx_bf[...] = wx_ref[...].astype(BF16)
        wb_bf[...] = wb_ref[...].astype(BF16)
        wc_bf[...] = wc_ref[...].astype(BF16)
        u_pad[0:SUBLANES, :] = jnp.zeros((SUBLANES, u_pad.shape[1]), F32)

    cw = cw_ref[...]
    for r0, rc in _tapered_chunks(seq):
        h = h_ref[r0:r0 + rc, :]
        u = _dot(h, wc_bf[...]) * _dot(h, wx_bf[...])
        p0 = SUBLANES + r0
        u_pad[p0:p0 + rc, :] = u
        y = (cw[0:1, :] * u_pad[p0 - 2:p0 - 2 + rc, :]
             + cw[1:2, :] * u_pad[p0 - 1:p0 - 1 + rc, :])
        y = y + cw[2:3, :] * u
        o_ref[r0:r0 + rc, :] = (_dot(h, wb_bf[...]) * y).astype(o_ref.dtype)


def _conv_branch(h, w_in, conv_w, batch, seq, width, next_col0, next_cols, bn=256):
    m, d = h.shape
    nj = width // bn
    side_in, side_out, side_shape, side_bytes = _side_cast_specs(
        w_in, nj * batch, lambda j, b: j * batch + b, next_col0, next_cols)
    return pl.pallas_call(
        _conv_branch_kernel,
        grid=(nj, batch),
        in_specs=[pl.BlockSpec((seq, d), lambda j, b: (b, 0)),
                  pl.BlockSpec((d, bn), lambda j, b: (0, j)),
                  pl.BlockSpec((d, bn), lambda j, b: (0, j + nj)),
                  pl.BlockSpec((d, bn), lambda j, b: (0, j + 2 * nj)),
                  pl.BlockSpec((CONV_K, bn), lambda j, b: (0, j)),
                  side_in],
        out_specs=[pl.BlockSpec((seq, bn), lambda j, b: (b, j)), side_out],
        out_shape=[pltpu.HBM((m, width), BF16), side_shape],
        scratch_shapes=[pltpu.VMEM((d, bn), BF16)] * 3
        + [pltpu.VMEM((SUBLANES + seq, bn), F32)],
        compiler_params=_compiler_params(
            2,
            _nbytes((seq, d), BF16) + 3 * _nbytes((d, bn), F32) + _nbytes((seq, bn), BF16) + side_bytes,
            3 * _nbytes((d, bn), BF16) + _nbytes((seq, bn), F32) + 8 * _nbytes((ROW_CHUNK, bn), F32)),
        name="conv_branch",
    )(h, w_in, w_in, w_in, conv_w, w_in)


def _head_norm_rope(x, gain, rope):
    width = x.shape[1]
    cos, sin_hi, sin_lo = (rope[:, i * LANES:(i + 1) * LANES] for i in range(3))
    r = lax.broadcasted_iota(jnp.int32, (MXU_DIM, MXU_DIM), 0) // HEAD_DIM
    c = lax.broadcasted_iota(jnp.int32, (MXU_DIM, MXU_DIM), 1) // HEAD_DIM
    ones_blockdiag = (r == c).astype(BF16)
    out = []
    for t in range(width // MXU_DIM):
        xt = x[:, t * MXU_DIM:(t + 1) * MXU_DIM]
        ss = _dot((xt * xt).astype(BF16), ones_blockdiag)
        y = xt * lax.rsqrt(ss * (1.0 / HEAD_DIM) + RMS_EPS) * gain[:, t * MXU_DIM:(t + 1) * MXU_DIM]
        for s in range(MXU_DIM // LANES):
            ys = y[:, s * LANES:(s + 1) * LANES]
            half = ROT_DIM // 2
            rot = (ys * cos
                   + pltpu.roll(ys, LANES - half, 1) * sin_hi
                   + pltpu.roll(ys, half, 1) * sin_lo)
            out.append(rot)
    return jnp.concatenate(out, axis=1)


def _qkv_kernel(h_ref, w_ref, gain_ref, raw_ref, rope_ref, o_ref, w_bf):
    @pl.when(pl.program_id(1) == 0)
    def _():
        w_bf[...] = w_ref[...].astype(BF16)

    gain = gain_ref[...]
    is_raw = raw_ref[...] != 0.0
    seq = h_ref.shape[0]
    for r0, rc in _tapered_chunks(seq):
        acc = _dot(h_ref[r0:r0 + rc, :], w_bf[...])
        y = _head_norm_rope(acc, gain, rope_ref[r0:r0 + rc, :])
        o_ref[r0:r0 + rc, :] = jnp.where(is_raw, acc, y).astype(o_ref.dtype)


def _qkv_proj(h, w_in, gains, raw_cols, rope, batch, seq, q_col, bn):
    m, d = h.shape
    assert q_col % bn == 0
    nj = gains.shape[0]
    first = q_col // bn
    table_spec = pl.BlockSpec((seq, 3 * LANES), lambda j, b: (0, 0))
    col_spec = pl.BlockSpec((None, 1, bn), lambda j, b: (j, 0, 0))
    return pl.pallas_call(
        _qkv_kernel,
        grid=(nj, batch),
        in_specs=[pl.BlockSpec((seq, d), lambda j, b: (b, 0)),
                  pl.BlockSpec((d, bn), lambda j, b: (0, first + j)),
                  col_spec, col_spec, table_spec],
        out_specs=pl.BlockSpec((seq, bn), lambda j, b: (b, j)),
        out_shape=pltpu.HBM((m, nj * bn), BF16),
        scratch_shapes=[pltpu.VMEM((d, bn), BF16)],
        compiler_params=_compiler_params(
            2,
            _nbytes((seq, d), BF16) + _nbytes((d, bn), F32) + _nbytes((seq, bn), BF16)
            + _nbytes((seq, 3 * LANES), F32),
            _nbytes((d, bn), BF16) + 10 * _nbytes((ROW_CHUNK, bn), F32)),
        name="qkv_proj",
    )(h, w_in, gains, raw_cols, rope)


ATTN_BLOCKS_PER_ITER = 8


def _attention_kernel(sink_ref, q_ref, kv_ref, o_ref, k_lo, k_hi, v_lo, v_hi, bias, s_scr, p_scr, inv_scr,
                      *, n_kv_heads):
    seq = q_ref.shape[0]
    kw = n_kv_heads * HEAD_DIM
    two_w = 2 * WINDOW
    assert LANES == 2 * HEAD_DIM and GROUP == 4

    lane = lax.broadcasted_iota(jnp.int32, (seq, LANES), 1)
    zero_block = jnp.zeros((WINDOW, LANES), BF16)
    for src0, lo_ref, hi_ref in ((0, k_lo, k_hi), (kw, v_lo, v_hi)):
        for grp in range(kw // LANES):
            x = kv_ref[:, src0 + grp * LANES:src0 + (grp + 1) * LANES]
            swapped = pltpu.roll(x, HEAD_DIM, 1)
            zero = jnp.zeros_like(x)
            even_lo, odd_lo = jnp.where(lane < HEAD_DIM, x, zero), jnp.where(lane < HEAD_DIM, swapped, zero)
            even_hi, odd_hi = jnp.where(lane >= HEAD_DIM, swapped, zero), jnp.where(lane >= HEAD_DIM, x, zero)
            for hk, lo, hi in ((2 * grp, even_lo, even_hi), (2 * grp + 1, odd_lo, odd_hi)):
                lo_ref[hk, 0:WINDOW, :] = zero_block
                hi_ref[hk, 0:WINDOW, :] = zero_block
                lo_ref[hk, WINDOW:WINDOW + seq, :] = lo
                hi_ref[hk, WINDOW:WINDOW + seq, :] = hi

    @pl.when(pl.program_id(0) == 0)
    def _():
        key = lax.broadcasted_iota(jnp.int32, (2 * two_w, two_w), 0) % two_w
        qry = lax.broadcasted_iota(jnp.int32, (2 * two_w, two_w), 1) % WINDOW
        diff = qry + WINDOW - key
        in_window = (diff >= 0) & (diff < WINDOW)
        bias[0] = jnp.where(in_window & (key >= WINDOW), 0.0, NEG_INF)
        bias[1] = jnp.where(in_window, 0.0, NEG_INF)

    n_slabs = s_scr.shape[0]
    blocks_per_iter = n_slabs // n_kv_heads

    def blocks(i, carry):
        items = [(u * n_kv_heads + hk, hk, i * blocks_per_iter + u)
                 for u in range(blocks_per_iter) for hk in range(n_kv_heads)]
        for slab, hk, n in items:
            r0 = pl.multiple_of(n * WINDOW, WINDOW)
            c0 = hk * GROUP * HEAD_DIM
            q_pairs = jnp.concatenate([q_ref[pl.ds(r0, WINDOW), c0:c0 + LANES],
                                       q_ref[pl.ds(r0, WINDOW), c0 + LANES:c0 + 2 * LANES]], axis=0)
            keys = jnp.concatenate([k_lo[hk, pl.ds(r0, two_w), :], k_hi[hk, pl.ds(r0, two_w), :]], axis=0)
            s_scr[slab] = lax.dot_general(keys, q_pairs, (((1,), (1,)), ((), ())), preferred_element_type=F32)
        for slab, hk, n in items:
            t = jnp.minimum(n, 1)
            for half in range(2):
                for pair in range(2):
                    rows = slice(half * two_w, (half + 1) * two_w)
                    cols = slice(pair * WINDOW, (pair + 1) * WINDOW)
                    s_h = s_scr[slab, rows, cols] + bias[t, rows, cols]
                    sink = LOG2_E * sink_ref[hk * GROUP + 2 * pair + half]
                    mx = jnp.maximum(jnp.max(s_h, axis=0, keepdims=True), sink)
                    p = jnp.exp2(s_h - mx)
                    inv = 1.0 / (jnp.sum(p, axis=0, keepdims=True) + jnp.exp2(sink - mx))
                    inv_scr[2 * slab + half, :, cols] = jnp.broadcast_to(inv, (SUBLANES, WINDOW))
                    p_scr[slab, rows, cols] = p.astype(BF16)
        for slab, hk, n in items:
            r0 = pl.multiple_of(n * WINDOW, WINDOW)
            c0 = hk * GROUP * HEAD_DIM
            vals = jnp.concatenate([v_lo[hk, pl.ds(r0, two_w), :], v_hi[hk, pl.ds(r0, two_w), :]], axis=0)
            o_t = lax.dot_general(vals, p_scr[slab], (((0,), (0,)), ((), ())), preferred_element_type=F32)
            o_t = jnp.concatenate([o_t[0:HEAD_DIM, :] * inv_scr[2 * slab, 0:1, :],
                                   o_t[HEAD_DIM:LANES, :] * inv_scr[2 * slab + 1, 0:1, :]], axis=0)
            for pair in range(2):
                o_ref[pl.ds(r0, WINDOW), c0 + pair * LANES:c0 + (pair + 1) * LANES] = (
                    o_t[:, pair * WINDOW:(pair + 1) * WINDOW].T.astype(o_ref.dtype))
        return carry

    assert (seq // WINDOW) % blocks_per_iter == 0
    lax.fori_loop(0, seq // WINDOW // blocks_per_iter, blocks, 0)


def _attention(qkv, sinks, batch, seq, q_width, kv_width):
    m = qkv.shape[0]
    assert q_width % (2 * kv_width) == 0
    n_kv_heads = kv_width // HEAD_DIM
    n_slabs = ATTN_BLOCKS_PER_ITER * n_kv_heads
    return pl.pallas_call(
        functools.partial(_attention_kernel, n_kv_heads=n_kv_heads),
        grid=(batch,),
        in_specs=[pl.BlockSpec(memory_space=pltpu.SMEM),
                  pl.BlockSpec((seq, q_width), lambda b: (b, 0)),
                  pl.BlockSpec((seq, 2 * kv_width), lambda b: (b, q_width // (2 * kv_width)))],
        out_specs=pl.BlockSpec((seq, q_width), lambda b: (b, 0)),
        out_shape=pltpu.HBM((m, q_width), BF16),
        scratch_shapes=[pltpu.VMEM((n_kv_heads, WINDOW + seq, LANES), BF16)] * 4
        + [pltpu.VMEM((2, 4 * WINDOW, 2 * WINDOW), F32),
           pltpu.VMEM((n_slabs, 4 * WINDOW, 2 * WINDOW), F32),
           pltpu.VMEM((n_slabs, 4 * WINDOW, 2 * WINDOW), BF16),
           pltpu.VMEM((2 * n_slabs, SUBLANES, 2 * WINDOW), F32)],
        compiler_params=_compiler_params(
            1,
            2 * _nbytes((seq, q_width), BF16) + _nbytes((seq, 2 * kv_width), BF16),
            4 * _nbytes((n_kv_heads, WINDOW + seq, LANES), BF16)
            + (8 + 2 * n_slabs) * _nbytes((4 * WINDOW, 2 * WINDOW), F32)
            + 4 * _nbytes((seq, LANES), F32)),
        name="swa_attention",
    )(sinks, qkv, qkv)


MERGE_ROW_CHUNK = 128


def _merge_kernel(h_ref, yc_ref, at_ref, wga_ref, wgb_ref, woc_ref, woa_ref, wnext_ref,
                  o_ref, wnext_bf_ref, woc_bf, woa_bf):
    @pl.when(pl.program_id(1) == 0)
    def _():
        woc_bf[...] = woc_ref[...].astype(BF16)
        woa_bf[...] = woa_ref[...].astype(BF16)

    wnext_bf_ref[...] = wnext_ref[...].astype(BF16)
    for rows in _row_chunks(h_ref.shape[0], MERGE_ROW_CHUNK):
        h = h_ref[rows, :]
        ya = _dot(yc_ref[rows, :], woc_bf[...])
        yb = _dot(at_ref[rows, :], woa_bf[...])
        ga = _dot(h, wga_ref[...])
        gb = _dot(h, wgb_ref[...])
        o_ref[rows, :] = (jax.nn.sigmoid(ga) * ya + jax.nn.sigmoid(gb) * yb).astype(o_ref.dtype)


def _merge(h, yconv, attn, w_gates_bf, w_out_conv, w_out_attn, w_next, bm=2048, bn=512):
    m, d = h.shape
    kc, ka = yconv.shape[1], attn.shape[1]
    nj, ni = d // bn, m // bm
    side_in, side_out, side_shape, side_bytes = _side_cast_specs(w_next, nj * ni, lambda j, i: j * ni + i)
    return pl.pallas_call(
        _merge_kernel,
        grid=(nj, ni),
        in_specs=[pl.BlockSpec((bm, d), lambda j, i: (i, 0)),
                  pl.BlockSpec((bm, kc), lambda j, i: (i, 0)),
                  pl.BlockSpec((bm, ka), lambda j, i: (i, 0)),
                  pl.BlockSpec((d, bn), lambda j, i: (0, j)),
                  pl.BlockSpec((d, bn), lambda j, i: (0, nj + j)),
                  pl.BlockSpec((kc, bn), lambda j, i: (0, j)),
                  pl.BlockSpec((ka, bn), lambda j, i: (0, j)),
                  side_in],
        out_specs=[pl.BlockSpec((bm, bn), lambda j, i: (i, j)), side_out],
        out_shape=[pltpu.HBM((m, d), BF16), side_shape],
        scratch_shapes=[pltpu.VMEM((kc, bn), BF16), pltpu.VMEM((ka, bn), BF16)],
        compiler_params=_compiler_params(
            2,
            _nbytes((bm, d + kc + ka), BF16) + _nbytes((2 * d, bn), BF16) + _nbytes((kc + ka, bn), F32)
            + _nbytes((bm, bn), BF16) + side_bytes,
            _nbytes((kc + ka, bn), BF16) + 10 * _nbytes((MERGE_ROW_CHUNK, bn), F32)),
        name="gated_merge",
    )(h, yconv, attn, w_gates_bf, w_gates_bf, w_out_conv, w_out_attn, w_next)


def _rope_tables(seq):
    half = ROT_DIM // 2
    inv_freq = (1.0 / (np.float32(ROPE_THETA) ** (np.arange(0, ROT_DIM, 2, dtype=np.float32) / ROT_DIM))
                ).astype(np.float32)
    ang = (np.arange(seq, dtype=np.float32)[:, None] * inv_freq[None, :]).astype(np.float64)
    cos, sin = np.cos(ang), np.sin(ang)
    ones = np.ones((seq, HEAD_DIM - ROT_DIM))
    zeros, zeros_tail = np.zeros((seq, half)), np.zeros((seq, HEAD_DIM - ROT_DIM))
    per_head = [np.concatenate([cos, cos, ones], axis=1),
                np.concatenate([-sin, zeros, zeros_tail], axis=1),
                np.concatenate([zeros, sin, zeros_tail], axis=1)]
    table = np.concatenate([np.tile(t, (1, LANES // HEAD_DIM)) for t in per_head], axis=1)
    return jnp.asarray(table.astype(np.float32))


def _swiglu_half_step(x, h, w_gu, w_down, next_gain):
    act, w_down_bf = _gate_up(h, w_gu, w_down)
    return _matmul_residual(act, w_down_bf, x, 0.5, next_gain)


def kernel(x, g_ffn1, w_gu1, w_down1, g_mix, w_in, conv_w, q_norm_g, k_norm_g, sinks,
           w_out_conv, w_out_attn, w_o, g_ffn2, w_gu2, w_down2):
    batch, seq, d = x.shape
    depth = w_in.shape[0]
    conv_width = conv_w.shape[2]
    q_width = w_out_attn.shape[1]
    kv_width = q_width // GROUP
    q_col = 3 * conv_width
    gate_col = q_col + q_width + 2 * kv_width
    rope = _rope_tables(seq)

    xf = x.reshape(batch * seq, d)
    h = _rmsnorm(xf, g_ffn1[0])
    for l in range(depth):
        xf, h = _swiglu_half_step(xf, h, w_gu1[l], w_down1[l], g_mix[l])

        yconv, w_gates_bf = _conv_branch(h, w_in[l], conv_w[l], batch, seq, conv_width, gate_col, 2 * d)
        qkv_bn = 2 * kv_width
        assert q_width % qkv_bn == 0
        gains = jnp.concatenate(
            [jnp.tile(q_norm_g[l] * (ATTN_SCALE * LOG2_E), q_width // HEAD_DIM),
             jnp.tile(k_norm_g[l], kv_width // HEAD_DIM), jnp.ones((kv_width,), F32)]
        ).reshape(-1, 1, qkv_bn)
        raw_cols = jnp.concatenate(
            [jnp.zeros((q_width + kv_width,), F32), jnp.ones((kv_width,), F32)]
        ).reshape(-1, 1, qkv_bn)
        qkv = _qkv_proj(h, w_in[l], gains, raw_cols, rope, batch, seq, q_col, bn=qkv_bn)
        attn = _attention(qkv, sinks[l], batch, seq, q_width, kv_width)
        merged, w_o_bf = _merge(h, yconv, attn, w_gates_bf, w_out_conv[l], w_out_attn[l], w_o[l])
        xf, h = _matmul_residual(merged, w_o_bf, xf, 1.0, g_ffn2[l])

        next_gain = g_ffn1[l + 1] if l + 1 < depth else None
        xf, h = _swiglu_half_step(xf, h, w_gu2[l], w_down2[l], next_gain)
    return xf.reshape(batch, seq, d)
```

```python
import functools
import math

import jax
import jax.numpy as jnp
import numpy as np
from jax import lax
from jax.experimental import pallas as pl
from jax.experimental.pallas import tpu as pltpu

F32 = jnp.float32
BF16 = jnp.bfloat16

HEAD_DIM = 64
GROUP = 4
CONV_K = 3
WINDOW = 128
ROT_DIM = HEAD_DIM // 4
ROPE_THETA = 500000.0
RMS_EPS = 1e-6
ATTN_SCALE = HEAD_DIM ** -0.5
LOG2_E = 1.4426950408889634
NEG_INF = -1e30

LANES = 128
SUBLANES = 8
MXU_DIM = 256
VMEM_CAP_V7X = 60 * 1024 * 1024


def _nbytes(shape, dtype):
    n = 1
    for s in shape:
        n *= s
    return n * jnp.dtype(dtype).itemsize


def _compiler_params(n_grid_axes, pipelined_bytes, resident_bytes):
    need = 2 * pipelined_bytes + resident_bytes
    return pltpu.CompilerParams(
        dimension_semantics=("arbitrary",) * n_grid_axes,
        vmem_limit_bytes=min(VMEM_CAP_V7X, need + need // 8),
    )


def _dot(a, b):
    return jnp.dot(a, b, preferred_element_type=F32)


ROW_CHUNK = 1024


def _row_chunks(n_rows, chunk=ROW_CHUNK):
    chunk = min(chunk, n_rows)
    assert n_rows % chunk == 0
    return [pl.ds(r * chunk, chunk) for r in range(n_rows // chunk)]


def _tapered_chunks(n_rows, chunk=ROW_CHUNK):
    chunk = min(chunk, n_rows)
    assert n_rows % chunk == 0 and chunk % 2 == 0
    sizes = [chunk] * (n_rows // chunk - 1) + [chunk // 2, chunk // 2]
    starts = [sum(sizes[:i]) for i in range(len(sizes))]
    return list(zip(starts, sizes))


def _rmsnorm_kernel(x_ref, g_ref, o_ref):
    x = x_ref[...]
    ms = jnp.mean(x * x, axis=-1, keepdims=True)
    o_ref[...] = (x * lax.rsqrt(ms + RMS_EPS) * g_ref[...]).astype(o_ref.dtype)


def _rmsnorm(x, g, bm=1024):
    m, d = x.shape
    return pl.pallas_call(
        _rmsnorm_kernel,
        grid=(m // bm,),
        in_specs=[pl.BlockSpec((bm, d), lambda i: (i, 0)),
                  pl.BlockSpec((1, d), lambda i: (0, 0))],
        out_specs=pl.BlockSpec((bm, d), lambda i: (i, 0)),
        out_shape=pltpu.HBM((m, d), BF16),
        compiler_params=_compiler_params(
            1, _nbytes((bm, d), F32) + _nbytes((bm, d), BF16), 2 * _nbytes((bm, d), F32)),
        name="rmsnorm",
    )(x, g.reshape(1, d))


def _side_cast_specs(w_next, n_steps, step_index, col0=0, n_cols=None):
    k = w_next.shape[0]
    n_cols = w_next.shape[1] if n_cols is None else n_cols
    cols = math.gcd(col0, n_cols)
    n_col_blocks = n_cols // cols
    assert n_steps % n_col_blocks == 0
    n_row_blocks = n_steps // n_col_blocks
    assert k % n_row_blocks == 0 and (k // n_row_blocks) % (2 * SUBLANES) == 0 and cols % LANES == 0
    rows = k // n_row_blocks
    first = col0 // cols

    def in_map(*ids):
        t = step_index(*ids)
        return t // n_col_blocks, first + t % n_col_blocks

    def out_map(*ids):
        t = step_index(*ids)
        return t // n_col_blocks, t % n_col_blocks

    return (pl.BlockSpec((rows, cols), in_map), pl.BlockSpec((rows, cols), out_map),
            pltpu.HBM((k, n_cols), BF16), _nbytes((rows, cols), F32) + _nbytes((rows, cols), BF16))


GATE_UP_ROW_CHUNK = 256


def _gate_up_kernel(h_ref, wg_ref, wu_ref, wnext_ref, o_ref, wnext_bf_ref, wgu_bf):
    bn = wg_ref.shape[1]

    @pl.when(pl.program_id(1) == 0)
    def _():
        wgu_bf[:, :bn] = wg_ref[...].astype(BF16)
        wgu_bf[:, bn:] = wu_ref[...].astype(BF16)

    wnext_bf_ref[...] = wnext_ref[...].astype(BF16)
    for r0, rc in _tapered_chunks(h_ref.shape[0], GATE_UP_ROW_CHUNK):
        rows = pl.ds(r0, rc)
        gu = _dot(h_ref[rows, :], wgu_bf[...])
        g, u = gu[:, :bn], gu[:, bn:]
        o_ref[rows, :] = (g * jax.nn.sigmoid(g) * u).astype(o_ref.dtype)


def _gate_up(h, w_gu, w_next, bm=2048, bn=512):
    m, d = h.shape
    f = w_gu.shape[1] // 2
    nj, ni = f // bn, m // bm
    side_in, side_out, side_shape, side_bytes = _side_cast_specs(w_next, nj * ni, lambda j, i: j * ni + i)
    return pl.pallas_call(
        _gate_up_kernel,
        grid=(nj, ni),
        in_specs=[pl.BlockSpec((bm, d), lambda j, i: (i, 0)),
                  pl.BlockSpec((d, bn), lambda j, i: (0, j)),
                  pl.BlockSpec((d, bn), lambda j, i: (0, j + nj)),
                  side_in],
        out_specs=[pl.BlockSpec((bm, bn), lambda j, i: (i, j)), side_out],
        out_shape=[pltpu.HBM((m, f), BF16), side_shape],
        scratch_shapes=[pltpu.VMEM((d, 2 * bn), BF16)],
        compiler_params=_compiler_params(
            2,
            _nbytes((bm, d), BF16) + 2 * _nbytes((d, bn), F32) + _nbytes((bm, bn), BF16) + side_bytes,
            2 * _nbytes((d, bn), BF16) + 6 * _nbytes((ROW_CHUNK, bn), F32)),
        name="gate_up",
    )(h, w_gu, w_gu, w_next)


RESIDENT_CHUNKS = 2
LONG_CONTRACTION = 4096


def _resident_chunk_rows(rows, k):
    return rows if k >= LONG_CONTRACTION else rows // RESIDENT_CHUNKS


def _matmul_residual_kernel(*refs, scale, with_norm):
    if with_norm:
        a_ref, w_ref, r_ref, g_ref, x_ref, h_ref = refs
    else:
        a_ref, w_ref, r_ref, x_ref = refs
    for rows in _row_chunks(a_ref.shape[0], _resident_chunk_rows(*a_ref.shape)):
        x = r_ref[rows, :] + scale * _dot(a_ref[rows, :], w_ref[...])
        x_ref[rows, :] = x
        if with_norm:
            ms = jnp.mean(x * x, axis=-1, keepdims=True)
            h_ref[rows, :] = (x * lax.rsqrt(ms + RMS_EPS) * g_ref[...]).astype(h_ref.dtype)


def _matmul_residual(a, w_bf, res, scale, norm_gain=None):
    m, k = a.shape
    n = w_bf.shape[1]
    with_norm = norm_gain is not None

    def resident_bytes(rows):
        return _nbytes((k, n), BF16) + 6 * _nbytes((_resident_chunk_rows(rows, k), n), F32)

    def pipelined_bytes(rows):
        return (_nbytes((rows, k), BF16) + 2 * _nbytes((rows, n), F32)
                + (_nbytes((rows, n), BF16) if with_norm else 0))

    bm = next(rows for rows in (1024, 512, 256, 128)
              if m % rows == 0
              and (2 * pipelined_bytes(rows) + resident_bytes(rows)) * 9 // 8 <= VMEM_CAP_V7X)
    row_spec = lambda width: pl.BlockSpec((bm, width), lambda i: (i, 0))
    in_specs = [row_spec(k),
                pl.BlockSpec((k, n), lambda i: (0, 0), pipeline_mode=pl.Buffered(1)),
                row_spec(n)]
    operands = [a, w_bf, res]
    out_specs = [row_spec(n)]
    out_shape = [pltpu.HBM((m, n), F32)]
    if with_norm:
        in_specs.append(pl.BlockSpec((1, n), lambda i: (0, 0)))
        operands.append(norm_gain.reshape(1, n))
        out_specs.append(row_spec(n))
        out_shape.append(pltpu.HBM((m, n), BF16))
    outs = pl.pallas_call(
        functools.partial(_matmul_residual_kernel, scale=scale, with_norm=with_norm),
        grid=(m // bm,),
        in_specs=in_specs,
        out_specs=out_specs,
        out_shape=out_shape,
        compiler_params=_compiler_params(1, pipelined_bytes(bm), resident_bytes(bm)),
        name="matmul_residual",
    )(*operands)
    return outs if with_norm else (outs[0], None)


def _conv_branch_kernel(h_ref, wx_ref, wb_ref, wc_ref, cw_ref, wnext_ref, o_ref, wnext_bf_ref,
                        wx_bf, wb_bf, wc_bf, u_pad):
    seq = h_ref.shape[0]
    wnext_bf_ref[...] = wnext_ref[...].astype(BF16)

    @pl.when(pl.program_id(1) == 0)
    def _():
        wx_bf[...] = wx_ref[...].astype(BF16)
        wb_bf[...] = wb_ref[...].astype(BF16)
        wc_bf[...] = wc_ref[...].astype(BF16)
        u_pad[0:SUBLANES, :] = jnp.zeros((SUBLANES, u_pad.shape[1]), F32)

    cw = cw_ref[...]
    for r0, rc in _tapered_chunks(seq):
        h = h_ref[r0:r0 + rc, :]
        u = _dot(h, wc_bf[...]) * _dot(h, wx_bf[...])
        p0 = SUBLANES + r0
        u_pad[p0:p0 + rc, :] = u
        y = (cw[0:1, :] * u_pad[p0 - 2:p0 - 2 + rc, :]
             + cw[1:2, :] * u_pad[p0 - 1:p0 - 1 + rc, :])
        y = y + cw[2:3, :] * u
        o_ref[r0:r0 + rc, :] = (_dot(h, wb_bf[...]) * y).astype(o_ref.dtype)


def _conv_branch(h, w_in, conv_w, batch, seq, width, next_col0, next_cols, bn=256):
    m, d = h.shape
    nj = width // bn
    side_in, side_out, side_shape, side_bytes = _side_cast_specs(
        w_in, nj * batch, lambda j, b: j * batch + b, next_col0, next_cols)
    return pl.pallas_call(
        _conv_branch_kernel,
        grid=(nj, batch),
        in_specs=[pl.BlockSpec((seq, d), lambda j, b: (b, 0)),
                  pl.BlockSpec((d, bn), lambda j, b: (0, j)),
                  pl.BlockSpec((d, bn), lambda j, b: (0, j + nj)),
                  pl.BlockSpec((d, bn), lambda j, b: (0, j + 2 * nj)),
                  pl.BlockSpec((CONV_K, bn), lambda j, b: (0, j)),
                  side_in],
        out_specs=[pl.BlockSpec((seq, bn), lambda j, b: (b, j)), side_out],
        out_shape=[pltpu.HBM((m, width), BF16), side_shape],
        scratch_shapes=[pltpu.VMEM((d, bn), BF16)] * 3
        + [pltpu.VMEM((SUBLANES + seq, bn), F32)],
        compiler_params=_compiler_params(
            2,
            _nbytes((seq, d), BF16) + 3 * _nbytes((d, bn), F32) + _nbytes((seq, bn), BF16) + side_bytes,
            3 * _nbytes((d, bn), BF16) + _nbytes((seq, bn), F32) + 8 * _nbytes((ROW_CHUNK, bn), F32)),
        name="conv_branch",
    )(h, w_in, w_in, w_in, conv_w, w_in)


def _head_norm_rope(x, gain, rope):
    width = x.shape[1]
    cos, sin_hi, sin_lo = (rope[:, i * LANES:(i + 1) * LANES] for i in range(3))
    r = lax.broadcasted_iota(jnp.int32, (MXU_DIM, MXU_DIM), 0) // HEAD_DIM
    c = lax.broadcasted_iota(jnp.int32, (MXU_DIM, MXU_DIM), 1) // HEAD_DIM
    ones_blockdiag = (r == c).astype(BF16)
    out = []
    for t in range(width // MXU_DIM):
        xt = x[:, t * MXU_DIM:(t + 1) * MXU_DIM]
        ss = _dot((xt * xt).astype(BF16), ones_blockdiag)
        y = xt * lax.rsqrt(ss * (1.0 / HEAD_DIM) + RMS_EPS) * gain[:, t * MXU_DIM:(t + 1) * MXU_DIM]
        for s in range(MXU_DIM // LANES):
            ys = y[:, s * LANES:(s + 1) * LANES]
            half = ROT_DIM // 2
            rot = (ys * cos
                   + pltpu.roll(ys, LANES - half, 1) * sin_hi
                   + pltpu.roll(ys, half, 1) * sin_lo)
            out.append(rot)
    return jnp.concatenate(out, axis=1)


def _qkv_kernel(h_ref, w_ref, gain_ref, raw_ref, rope_ref, o_ref, w_bf):
    @pl.when(pl.program_id(1) == 0)
    def _():
        w_bf[...] = w_ref[...].astype(BF16)

    gain = gain_ref[...]
    is_raw = raw_ref[...] != 0.0
    seq = h_ref.shape[0]
    for r0, rc in _tapered_chunks(seq):
        acc = _dot(h_ref[r0:r0 + rc, :], w_bf[...])
        y = _head_norm_rope(acc, gain, rope_ref[r0:r0 + rc, :])
        o_ref[r0:r0 + rc, :] = jnp.where(is_raw, acc, y).astype(o_ref.dtype)


def _qkv_proj(h, w_in, gains, raw_cols, rope, batch, seq, q_col, bn):
    m, d = h.shape
    assert q_col % bn == 0
    nj = gains.shape[0]
    first = q_col // bn
    table_spec = pl.BlockSpec((seq, 3 * LANES), lambda j, b: (0, 0))
    col_spec = pl.BlockSpec((None, 1, bn), lambda j, b: (j, 0, 0))
    return pl.pallas_call(
        _qkv_kernel,
        grid=(nj, batch),
        in_specs=[pl.BlockSpec((seq, d), lambda j, b: (b, 0)),
                  pl.BlockSpec((d, bn), lambda j, b: (0, first + j)),
                  col_spec, col_spec, table_spec],
        out_specs=pl.BlockSpec((seq, bn), lambda j, b: (b, j)),
        out_shape=pltpu.HBM((m, nj * bn), BF16),
        scratch_shapes=[pltpu.VMEM((d, bn), BF16)],
        compiler_params=_compiler_params(
            2,
            _nbytes((seq, d), BF16) + _nbytes((d, bn), F32) + _nbytes((seq, bn), BF16)
            + _nbytes((seq, 3 * LANES), F32),
            _nbytes((d, bn), BF16) + 10 * _nbytes((ROW_CHUNK, bn), F32)),
        name="qkv_proj",
    )(h, w_in, gains, raw_cols, rope)


ATTN_BLOCKS_PER_ITER = 4


def _attention_kernel(sink_ref, q_ref, kv_ref, o_ref, k_lo, k_hi, v_lo, v_hi, bias, s_scr, p_scr, inv_scr,
                      *, n_kv_heads):
    seq = q_ref.shape[0]
    kw = n_kv_heads * HEAD_DIM
    two_w = 2 * WINDOW
    assert LANES == 2 * HEAD_DIM and GROUP == 4

    lane = lax.broadcasted_iota(jnp.int32, (seq, LANES), 1)
    zero_block = jnp.zeros((WINDOW, LANES), BF16)
    for src0, lo_ref, hi_ref in ((0, k_lo, k_hi), (kw, v_lo, v_hi)):
        for grp in range(kw // LANES):
            x = kv_ref[:, src0 + grp * LANES:src0 + (grp + 1) * LANES]
            swapped = pltpu.roll(x, HEAD_DIM, 1)
            zero = jnp.zeros_like(x)
            even_lo, odd_lo = jnp.where(lane < HEAD_DIM, x, zero), jnp.where(lane < HEAD_DIM, swapped, zero)
            even_hi, odd_hi = jnp.where(lane >= HEAD_DIM, swapped, zero), jnp.where(lane >= HEAD_DIM, x, zero)
            for hk, lo, hi in ((2 * grp, even_lo, even_hi), (2 * grp + 1, odd_lo, odd_hi)):
                lo_ref[hk, 0:WINDOW, :] = zero_block
                hi_ref[hk, 0:WINDOW, :] = zero_block
                lo_ref[hk, WINDOW:WINDOW + seq, :] = lo
                hi_ref[hk, WINDOW:WINDOW + seq, :] = hi

    @pl.when(pl.program_id(0) == 0)
    def _():
        key = lax.broadcasted_iota(jnp.int32, (2 * two_w, two_w), 0) % two_w
        qry = lax.broadcasted_iota(jnp.int32, (2 * two_w, two_w), 1) % WINDOW
        diff = qry + WINDOW - key
        in_window = (diff >= 0) & (diff < WINDOW)
        bias[0] = jnp.where(in_window & (key >= WINDOW), 0.0, NEG_INF)
        bias[1] = jnp.where(in_window, 0.0, NEG_INF)

    n_slabs = s_scr.shape[0]
    blocks_per_iter = n_slabs // n_kv_heads

    def blocks(i, carry):
        items = [(u * n_kv_heads + hk, hk, i * blocks_per_iter + u)
                 for u in range(blocks_per_iter) for hk in range(n_kv_heads)]
        for slab, hk, n in items:
            r0 = pl.multiple_of(n * WINDOW, WINDOW)
            c0 = hk * GROUP * HEAD_DIM
            q_pairs = jnp.concatenate([q_ref[pl.ds(r0, WINDOW), c0:c0 + LANES],
                                       q_ref[pl.ds(r0, WINDOW), c0 + LANES:c0 + 2 * LANES]], axis=0)
            keys = jnp.concatenate([k_lo[hk, pl.ds(r0, two_w), :], k_hi[hk, pl.ds(r0, two_w), :]], axis=0)
            s_scr[slab] = lax.dot_general(keys, q_pairs, (((1,), (1,)), ((), ())), preferred_element_type=F32)
        for slab, hk, n in items:
            t = jnp.minimum(n, 1)
            for half in range(2):
                for pair in range(2):
                    rows = slice(half * two_w, (half + 1) * two_w)
                    cols = slice(pair * WINDOW, (pair + 1) * WINDOW)
                    s_h = s_scr[slab, rows, cols] + bias[t, rows, cols]
                    sink = LOG2_E * sink_ref[hk * GROUP + 2 * pair + half]
                    mx = jnp.maximum(jnp.max(s_h, axis=0, keepdims=True), sink)
                    p = jnp.exp2(s_h - mx)
                    inv = 1.0 / (jnp.sum(p, axis=0, keepdims=True) + jnp.exp2(sink - mx))
                    inv_scr[2 * slab + half, :, cols] = jnp.broadcast_to(inv, (SUBLANES, WINDOW))
                    p_scr[slab, rows, cols] = p.astype(BF16)
        for slab, hk, n in items:
            r0 = pl.multiple_of(n * WINDOW, WINDOW)
            c0 = hk * GROUP * HEAD_DIM
            vals = jnp.concatenate([v_lo[hk, pl.ds(r0, two_w), :], v_hi[hk, pl.ds(r0, two_w), :]], axis=0)
            o_t = lax.dot_general(vals, p_scr[slab], (((0,), (0,)), ((), ())), preferred_element_type=F32)
            o_t = jnp.concatenate([o_t[0:HEAD_DIM, :] * inv_scr[2 * slab, 0:1, :],
                                   o_t[HEAD_DIM:LANES, :] * inv_scr[2 * slab + 1, 0:1, :]], axis=0)
            for pair in range(2):
                o_ref[pl.ds(r0, WINDOW), c0 + pair * LANES:c0 + (pair + 1) * LANES] = (
                    o_t[:, pair * WINDOW:(pair + 1) * WINDOW].T.astype(o_ref.dtype))
        return carry

    assert (seq // WINDOW) % blocks_per_iter == 0
    lax.fori_loop(0, seq // WINDOW // blocks_per_iter, blocks, 0)


def _attention(qkv, sinks, batch, seq, q_width, kv_width):
    m = qkv.shape[0]
    assert q_width % (2 * kv_width) == 0
    n_kv_heads = kv_width // HEAD_DIM
    n_slabs = ATTN_BLOCKS_PER_ITER * n_kv_heads
    return pl.pallas_call(
        functools.partial(_attention_kernel, n_kv_heads=n_kv_heads),
        grid=(batch,),
        in_specs=[pl.BlockSpec(memory_space=pltpu.SMEM),
                  pl.BlockSpec((seq, q_width), lambda b: (b, 0)),
                  pl.BlockSpec((seq, 2 * kv_width), lambda b: (b, q_width // (2 * kv_width)))],
        out_specs=pl.BlockSpec((seq, q_width), lambda b: (b, 0)),
        out_shape=pltpu.HBM((m, q_width), BF16),
        scratch_shapes=[pltpu.VMEM((n_kv_heads, WINDOW + seq, LANES), BF16)] * 4
        + [pltpu.VMEM((2, 4 * WINDOW, 2 * WINDOW), F32),
           pltpu.VMEM((n_slabs, 4 * WINDOW, 2 * WINDOW), F32),
           pltpu.VMEM((n_slabs, 4 * WINDOW, 2 * WINDOW), BF16),
           pltpu.VMEM((2 * n_slabs, SUBLANES, 2 * WINDOW), F32)],
        compiler_params=_compiler_params(
            1,
            2 * _nbytes((seq, q_width), BF16) + _nbytes((seq, 2 * kv_width), BF16),
            4 * _nbytes((n_kv_heads, WINDOW + seq, LANES), BF16)
            + (8 + 2 * n_slabs) * _nbytes((4 * WINDOW, 2 * WINDOW), F32)
            + 4 * _nbytes((seq, LANES), F32)),
        name="swa_attention",
    )(sinks, qkv, qkv)


MERGE_ROW_CHUNK = 128


def _merge_kernel(h_ref, yc_ref, at_ref, wga_ref, wgb_ref, woc_ref, woa_ref, wnext_ref,
                  o_ref, wnext_bf_ref, woc_bf, woa_bf):
    @pl.when(pl.program_id(1) == 0)
    def _():
        woc_bf[...] = woc_ref[...].astype(BF16)
        woa_bf[...] = woa_ref[...].astype(BF16)

    wnext_bf_ref[...] = wnext_ref[...].astype(BF16)
    for rows in _row_chunks(h_ref.shape[0], MERGE_ROW_CHUNK):
        h = h_ref[rows, :]
        ya = _dot(yc_ref[rows, :], woc_bf[...])
        yb = _dot(at_ref[rows, :], woa_bf[...])
        ga = _dot(h, wga_ref[...])
        gb = _dot(h, wgb_ref[...])
        o_ref[rows, :] = (jax.nn.sigmoid(ga) * ya + jax.nn.sigmoid(gb) * yb).astype(o_ref.dtype)


def _merge(h, yconv, attn, w_gates_bf, w_out_conv, w_out_attn, w_next, bm=2048, bn=512):
    m, d = h.shape
    kc, ka = yconv.shape[1], attn.shape[1]
    nj, ni = d // bn, m // bm
    side_in, side_out, side_shape, side_bytes = _side_cast_specs(w_next, nj * ni, lambda j, i: j * ni + i)
    return pl.pallas_call(
        _merge_kernel,
        grid=(nj, ni),
        in_specs=[pl.BlockSpec((bm, d), lambda j, i: (i, 0)),
                  pl.BlockSpec((bm, kc), lambda j, i: (i, 0)),
                  pl.BlockSpec((bm, ka), lambda j, i: (i, 0)),
                  pl.BlockSpec((d, bn), lambda j, i: (0, j)),
                  pl.BlockSpec((d, bn), lambda j, i: (0, nj + j)),
                  pl.BlockSpec((kc, bn), lambda j, i: (0, j)),
                  pl.BlockSpec((ka, bn), lambda j, i: (0, j)),
                  side_in],
        out_specs=[pl.BlockSpec((bm, bn), lambda j, i: (i, j)), side_out],
        out_shape=[pltpu.HBM((m, d), BF16), side_shape],
        scratch_shapes=[pltpu.VMEM((kc, bn), BF16), pltpu.VMEM((ka, bn), BF16)],
        compiler_params=_compiler_params(
            2,
            _nbytes((bm, d + kc + ka), BF16) + _nbytes((2 * d, bn), BF16) + _nbytes((kc + ka, bn), F32)
            + _nbytes((bm, bn), BF16) + side_bytes,
            _nbytes((kc + ka, bn), BF16) + 10 * _nbytes((MERGE_ROW_CHUNK, bn), F32)),
        name="gated_merge",
    )(h, yconv, attn, w_gates_bf, w_gates_bf, w_out_conv, w_out_attn, w_next)


def _rope_tables(seq):
    half = ROT_DIM // 2
    inv_freq = (1.0 / (np.float32(ROPE_THETA) ** (np.arange(0, ROT_DIM, 2, dtype=np.float32) / ROT_DIM))
                ).astype(np.float32)
    ang = (np.arange(seq, dtype=np.float32)[:, None] * inv_freq[None, :]).astype(np.float64)
    cos, sin = np.cos(ang), np.sin(ang)
    ones = np.ones((seq, HEAD_DIM - ROT_DIM))
    zeros, zeros_tail = np.zeros((seq, half)), np.zeros((seq, HEAD_DIM - ROT_DIM))
    per_head = [np.concatenate([cos, cos, ones], axis=1),
                np.concatenate([-sin, zeros, zeros_tail], axis=1),
                np.concatenate([zeros, sin, zeros_tail], axis=1)]
    table = np.concatenate([np.tile(t, (1, LANES // HEAD_DIM)) for t in per_head], axis=1)
    return jnp.asarray(table.astype(np.float32))


def _swiglu_half_step(x, h, w_gu, w_down, next_gain):
    act, w_down_bf = _gate_up(h, w_gu, w_down)
    return _matmul_residual(act, w_down_bf, x, 0.5, next_gain)


def kernel(x, g_ffn1, w_gu1, w_down1, g_mix, w_in, conv_w, q_norm_g, k_norm_g, sinks,
           w_out_conv, w_out_attn, w_o, g_ffn2, w_gu2, w_down2):
    batch, seq, d = x.shape
    depth = w_in.shape[0]
    conv_width = conv_w.shape[2]
    q_width = w_out_attn.shape[1]
    kv_width = q_width // GROUP
    q_col = 3 * conv_width
    gate_col = q_col + q_width + 2 * kv_width
    rope = _rope_tables(seq)

    xf = x.reshape(batch * seq, d)
    h = _rmsnorm(xf, g_ffn1[0])
    for l in range(depth):
        xf, h = _swiglu_half_step(xf, h, w_gu1[l], w_down1[l], g_mix[l])

        yconv, w_gates_bf = _conv_branch(h, w_in[l], conv_w[l], batch, seq, conv_width, gate_col, 2 * d)
        qkv_bn = 2 * kv_width
        assert q_width % qkv_bn == 0
        gains = jnp.concatenate(
            [jnp.tile(q_norm_g[l] * (ATTN_SCALE * LOG2_E), q_width // HEAD_DIM),
             jnp.tile(k_norm_g[l], kv_width // HEAD_DIM), jnp.ones((kv_width,), F32)]
        ).reshape(-1, 1, qkv_bn)
        raw_cols = jnp.concatenate(
            [jnp.zeros((q_width + kv_width,), F32), jnp.ones((kv_width,), F32)]
        ).reshape(-1, 1, qkv_bn)
        qkv = _qkv_proj(h, w_in[l], gains, raw_cols, rope, batch, seq, q_col, bn=qkv_bn)
        attn = _attention(qkv, sinks[l], batch, seq, q_width, kv_width)
        merged, w_o_bf = _merge(h, yconv, attn, w_gates_bf, w_out_conv[l], w_out_attn[l], w_o[l])
        xf, h = _matmul_residual(merged, w_o_bf, xf, 1.0, g_ffn2[l])

        next_gain = g_ffn1[l + 1] if l + 1 < depth else None
        xf, h = _swiglu_half_step(xf, h, w_gu2[l], w_down2[l], next_gain)
    return xf.reshape(batch, seq, d)
```

```python
import functools
import math

import jax
import jax.numpy as jnp
import numpy as np
from jax import lax
from jax.experimental import pallas as pl
from jax.experimental.pallas import tpu as pltpu

F32 = jnp.float32
BF16 = jnp.bfloat16

HEAD_DIM = 64
GROUP = 4
CONV_K = 3
WINDOW = 128
ROT_DIM = HEAD_DIM // 4
ROPE_THETA = 500000.0
RMS_EPS = 1e-6
ATTN_SCALE = HEAD_DIM ** -0.5
LOG2_E = 1.4426950408889634
NEG_INF = -1e30

LANES = 128
SUBLANES = 8
MXU_DIM = 256
VMEM_CAP_V7X = 60 * 1024 * 1024


def _nbytes(shape, dtype):
    n = 1
    for s in shape:
        n *= s
    return n * jnp.dtype(dtype).itemsize


def _compiler_params(n_grid_axes, pipelined_bytes, resident_bytes):
    need = 2 * pipelined_bytes + resident_bytes
    return pltpu.CompilerParams(
        dimension_semantics=("arbitrary",) * n_grid_axes,
        vmem_limit_bytes=min(VMEM_CAP_V7X, need + need // 8),
    )


def _dot(a, b):
    return jnp.dot(a, b, preferred_element_type=F32)


ROW_CHUNK = 1024


def _row_chunks(n_rows, chunk=ROW_CHUNK):
    chunk = min(chunk, n_rows)
    assert n_rows % chunk == 0
    return [pl.ds(r * chunk, chunk) for r in range(n_rows // chunk)]


def _tapered_chunks(n_rows, chunk=ROW_CHUNK):
    chunk = min(chunk, n_rows)
    assert n_rows % chunk == 0 and chunk % 2 == 0
    sizes = [chunk] * (n_rows // chunk - 1) + [chunk // 2, chunk // 2]
    starts = [sum(sizes[:i]) for i in range(len(sizes))]
    return list(zip(starts, sizes))


def _rmsnorm_kernel(x_ref, g_ref, o_ref):
    x = x_ref[...]
    ms = jnp.mean(x * x, axis=-1, keepdims=True)
    o_ref[...] = (x * lax.rsqrt(ms + RMS_EPS) * g_ref[...]).astype(o_ref.dtype)


def _rmsnorm(x, g, bm=1024):
    m, d = x.shape
    return pl.pallas_call(
        _rmsnorm_kernel,
        grid=(m // bm,),
        in_specs=[pl.BlockSpec((bm, d), lambda i: (i, 0)),
                  pl.BlockSpec((1, d), lambda i: (0, 0))],
        out_specs=pl.BlockSpec((bm, d), lambda i: (i, 0)),
        out_shape=pltpu.HBM((m, d), BF16),
        compiler_params=_compiler_params(
            1, _nbytes((bm, d), F32) + _nbytes((bm, d), BF16), 2 * _nbytes((bm, d), F32)),
        name="rmsnorm",
    )(x, g.reshape(1, d))


def _side_cast_specs(w_next, n_steps, step_index, col0=0, n_cols=None):
    k = w_next.shape[0]
    n_cols = w_next.shape[1] if n_cols is None else n_cols
    cols = math.gcd(col0, n_cols)
    n_col_blocks = n_cols // cols
    assert n_steps % n_col_blocks == 0
    n_row_blocks = n_steps // n_col_blocks
    assert k % n_row_blocks == 0 and (k // n_row_blocks) % (2 * SUBLANES) == 0 and cols % LANES == 0
    rows = k // n_row_blocks
    first = col0 // cols

    def in_map(*ids):
        t = step_index(*ids)
        return t // n_col_blocks, first + t % n_col_blocks

    def out_map(*ids):
        t = step_index(*ids)
        return t // n_col_blocks, t % n_col_blocks

    return (pl.BlockSpec((rows, cols), in_map), pl.BlockSpec((rows, cols), out_map),
            pltpu.HBM((k, n_cols), BF16), _nbytes((rows, cols), F32) + _nbytes((rows, cols), BF16))


GATE_UP_ROW_CHUNK = 256


def _gate_up_kernel(h_ref, wg_ref, wu_ref, wnext_ref, o_ref, wnext_bf_ref, wgu_bf):
    bn = wg_ref.shape[1]

    @pl.when(pl.program_id(1) == 0)
    def _():
        wgu_bf[:, :bn] = wg_ref[...].astype(BF16)
        wgu_bf[:, bn:] = wu_ref[...].astype(BF16)

    wnext_bf_ref[...] = wnext_ref[...].astype(BF16)
    for r0, rc in _tapered_chunks(h_ref.shape[0], GATE_UP_ROW_CHUNK):
        rows = pl.ds(r0, rc)
        gu = _dot(h_ref[rows, :], wgu_bf[...])
        g, u = gu[:, :bn], gu[:, bn:]
        o_ref[rows, :] = (g * jax.nn.sigmoid(g) * u).astype(o_ref.dtype)


def _gate_up(h, w_gu, w_next, bm=2048, bn=512):
    m, d = h.shape
    f = w_gu.shape[1] // 2
    nj, ni = f // bn, m // bm
    side_in, side_out, side_shape, side_bytes = _side_cast_specs(w_next, nj * ni, lambda j, i: j * ni + i)
    return pl.pallas_call(
        _gate_up_kernel,
        grid=(nj, ni),
        in_specs=[pl.BlockSpec((bm, d), lambda j, i: (i, 0)),
                  pl.BlockSpec((d, bn), lambda j, i: (0, j)),
                  pl.BlockSpec((d, bn), lambda j, i: (0, j + nj)),
                  side_in],
        out_specs=[pl.BlockSpec((bm, bn), lambda j, i: (i, j)), side_out],
        out_shape=[pltpu.HBM((m, f), BF16), side_shape],
        scratch_shapes=[pltpu.VMEM((d, 2 * bn), BF16)],
        compiler_params=_compiler_params(
            2,
            _nbytes((bm, d), BF16) + 2 * _nbytes((d, bn), F32) + _nbytes((bm, bn), BF16) + side_bytes,
            2 * _nbytes((d, bn), BF16) + 6 * _nbytes((ROW_CHUNK, bn), F32)),
        name="gate_up",
    )(h, w_gu, w_gu, w_next)


RESIDENT_CHUNKS = 2
LONG_CONTRACTION = 2048


def _resident_chunk_rows(rows, k):
    return rows if k >= LONG_CONTRACTION else rows // RESIDENT_CHUNKS


def _matmul_residual_kernel(*refs, scale, with_norm):
    if with_norm:
        a_ref, w_ref, r_ref, g_ref, x_ref, h_ref = refs
    else:
        a_ref, w_ref, r_ref, x_ref = refs
    for rows in _row_chunks(a_ref.shape[0], _resident_chunk_rows(*a_ref.shape)):
        x = r_ref[rows, :] + scale * _dot(a_ref[rows, :], w_ref[...])
        x_ref[rows, :] = x
        if with_norm:
            ms = jnp.mean(x * x, axis=-1, keepdims=True)
            h_ref[rows, :] = (x * lax.rsqrt(ms + RMS_EPS) * g_ref[...]).astype(h_ref.dtype)


def _matmul_residual(a, w_bf, res, scale, norm_gain=None):
    m, k = a.shape
    n = w_bf.shape[1]
    with_norm = norm_gain is not None

    def resident_bytes(rows):
        return _nbytes((k, n), BF16) + 3 * _nbytes((_resident_chunk_rows(rows, k), n), F32)

    def pipelined_bytes(rows):
        return (_nbytes((rows, k), BF16) + 2 * _nbytes((rows, n), F32)
                + (_nbytes((rows, n), BF16) if with_norm else 0))

    bm = next(rows for rows in (1024, 512, 256, 128)
              if m % rows == 0
              and (2 * pipelined_bytes(rows) + resident_bytes(rows)) * 9 // 8 <= VMEM_CAP_V7X)
    row_spec = lambda width: pl.BlockSpec((bm, width), lambda i: (i, 0))
    in_specs = [row_spec(k),
                pl.BlockSpec((k, n), lambda i: (0, 0), pipeline_mode=pl.Buffered(1)),
                row_spec(n)]
    operands = [a, w_bf, res]
    out_specs = [row_spec(n)]
    out_shape = [pltpu.HBM((m, n), F32)]
    if with_norm:
        in_specs.append(pl.BlockSpec((1, n), lambda i: (0, 0)))
        operands.append(norm_gain.reshape(1, n))
        out_specs.append(row_spec(n))
        out_shape.append(pltpu.HBM((m, n), BF16))
    outs = pl.pallas_call(
        functools.partial(_matmul_residual_kernel, scale=scale, with_norm=with_norm),
        grid=(m // bm,),
        in_specs=in_specs,
        out_specs=out_specs,
        out_shape=out_shape,
        compiler_params=_compiler_params(1, pipelined_bytes(bm), resident_bytes(bm)),
        name="matmul_residual",
    )(*operands)
    return outs if with_norm else (outs[0], None)


def _conv_branch_kernel(h_ref, wx_ref, wb_ref, wc_ref, cw_ref, wnext_ref, o_ref, wnext_bf_ref,
                        wx_bf, wb_bf, wc_bf, u_pad):
    seq = h_ref.shape[0]
    wnext_bf_ref[...] = wnext_ref[...].astype(BF16)

    @pl.when(pl.program_id(1) == 0)
    def _():
        wx_bf[...] = wx_ref[...].astype(BF16)
        wb_bf[...] = wb_ref[...].astype(BF16)
        wc_bf[...] = wc_ref[...].astype(BF16)
        u_pad[0:SUBLANES, :] = jnp.zeros((SUBLANES, u_pad.shape[1]), F32)

    cw = cw_ref[...]
    for r0, rc in _tapered_chunks(seq):
        h = h_ref[r0:r0 + rc, :]
        u = _dot(h, wc_bf[...]) * _dot(h, wx_bf[...])
        p0 = SUBLANES + r0
        u_pad[p0:p0 + rc, :] = u
        y = (cw[0:1, :] * u_pad[p0 - 2:p0 - 2 + rc, :]
             + cw[1:2, :] * u_pad[p0 - 1:p0 - 1 + rc, :])
        y = y + cw[2:3, :] * u
        o_ref[r0:r0 + rc, :] = (_dot(h, wb_bf[...]) * y).astype(o_ref.dtype)


def _conv_branch(h, w_in, conv_w, batch, seq, width, next_col0, next_cols, bn=256):
    m, d = h.shape
    nj = width // bn
    side_in, side_out, side_shape, side_bytes = _side_cast_specs(
        w_in, nj * batch, lambda j, b: j * batch + b, next_col0, next_cols)
    return pl.pallas_call(
        _conv_branch_kernel,
        grid=(nj, batch),
        in_specs=[pl.BlockSpec((seq, d), lambda j, b: (b, 0)),
                  pl.BlockSpec((d, bn), lambda j, b: (0, j)),
                  pl.BlockSpec((d, bn), lambda j, b: (0, j + nj)),
                  pl.BlockSpec((d, bn), lambda j, b: (0, j + 2 * nj)),
                  pl.BlockSpec((CONV_K, bn), lambda j, b: (0, j)),
                  side_in],
        out_specs=[pl.BlockSpec((seq, bn), lambda j, b: (b, j)), side_out],
        out_shape=[pltpu.HBM((m, width), BF16), side_shape],
        scratch_shapes=[pltpu.VMEM((d, bn), BF16)] * 3
        + [pltpu.VMEM((SUBLANES + seq, bn), F32)],
        compiler_params=_compiler_params(
            2,
            _nbytes((seq, d), BF16) + 3 * _nbytes((d, bn), F32) + _nbytes((seq, bn), BF16) + side_bytes,
            3 * _nbytes((d, bn), BF16) + _nbytes((seq, bn), F32) + 8 * _nbytes((ROW_CHUNK, bn), F32)),
        name="conv_branch",
    )(h, w_in, w_in, w_in, conv_w, w_in)


def _head_norm_rope(x, gain, rope):
    width = x.shape[1]
    cos, sin_hi, sin_lo = (rope[:, i * LANES:(i + 1) * LANES] for i in range(3))
    r = lax.broadcasted_iota(jnp.int32, (MXU_DIM, MXU_DIM), 0) // HEAD_DIM
    c = lax.broadcasted_iota(jnp.int32, (MXU_DIM, MXU_DIM), 1) // HEAD_DIM
    ones_blockdiag = (r == c).astype(BF16)
    out = []
    for t in range(width // MXU_DIM):
        xt = x[:, t * MXU_DIM:(t + 1) * MXU_DIM]
        ss = _dot((xt * xt).astype(BF16), ones_blockdiag)
        y = xt * lax.rsqrt(ss * (1.0 / HEAD_DIM) + RMS_EPS) * gain[:, t * MXU_DIM:(t + 1) * MXU_DIM]
        for s in range(MXU_DIM // LANES):
            ys = y[:, s * LANES:(s + 1) * LANES]
            half = ROT_DIM // 2
            rot = (ys * cos
                   + pltpu.roll(ys, LANES - half, 1) * sin_hi
                   + pltpu.roll(ys, half, 1) * sin_lo)
            out.append(rot)
    return jnp.concatenate(out, axis=1)


def _qkv_kernel(h_ref, w_ref, gain_ref, raw_ref, rope_ref, o_ref, w_bf):
    @pl.when(pl.program_id(1) == 0)
    def _():
        w_bf[...] = w_ref[...].astype(BF16)

    gain = gain_ref[...]
    is_raw = raw_ref[...] != 0.0
    seq = h_ref.shape[0]
    for r0, rc in _tapered_chunks(seq):
        acc = _dot(h_ref[r0:r0 + rc, :], w_bf[...])
        y = _head_norm_rope(acc, gain, rope_ref[r0:r0 + rc, :])
        o_ref[r0:r0 + rc, :] = jnp.where(is_raw, acc, y).astype(o_ref.dtype)


def _qkv_proj(h, w_in, gains, raw_cols, rope, batch, seq, q_col, bn):
    m, d = h.shape
    assert q_col % bn == 0
    nj = gains.shape[0]
    first = q_col // bn
    table_spec = pl.BlockSpec((seq, 3 * LANES), lambda j, b: (0, 0))
    col_spec = pl.BlockSpec((None, 1, bn), lambda j, b: (j, 0, 0))
    return pl.pallas_call(
        _qkv_kernel,
        grid=(nj, batch),
        in_specs=[pl.BlockSpec((seq, d), lambda j, b: (b, 0)),
                  pl.BlockSpec((d, bn), lambda j, b: (0, first + j)),
                  col_spec, col_spec, table_spec],
        out_specs=pl.BlockSpec((seq, bn), lambda j, b: (b, j)),
        out_shape=pltpu.HBM((m, nj * bn), BF16),
        scratch_shapes=[pltpu.VMEM((d, bn), BF16)],
        compiler_params=_compiler_params(
            2,
            _nbytes((seq, d), BF16) + _nbytes((d, bn), F32) + _nbytes((seq, bn), BF16)
            + _nbytes((seq, 3 * LANES), F32),
            _nbytes((d, bn), BF16) + 10 * _nbytes((ROW_CHUNK, bn), F32)),
        name="qkv_proj",
    )(h, w_in, gains, raw_cols, rope)


ATTN_BLOCKS_PER_ITER = 4


def _attention_kernel(sink_ref, q_ref, kv_ref, o_ref, k_lo, k_hi, v_lo, v_hi, bias, s_scr, p_scr, inv_scr,
                      *, n_kv_heads):
    seq = q_ref.shape[0]
    kw = n_kv_heads * HEAD_DIM
    two_w = 2 * WINDOW
    assert LANES == 2 * HEAD_DIM and GROUP == 4

    lane = lax.broadcasted_iota(jnp.int32, (seq, LANES), 1)
    zero_block = jnp.zeros((WINDOW, LANES), BF16)
    for src0, lo_ref, hi_ref in ((0, k_lo, k_hi), (kw, v_lo, v_hi)):
        for grp in range(kw // LANES):
            x = kv_ref[:, src0 + grp * LANES:src0 + (grp + 1) * LANES]
            swapped = pltpu.roll(x, HEAD_DIM, 1)
            zero = jnp.zeros_like(x)
            even_lo, odd_lo = jnp.where(lane < HEAD_DIM, x, zero), jnp.where(lane < HEAD_DIM, swapped, zero)
            even_hi, odd_hi = jnp.where(lane >= HEAD_DIM, swapped, zero), jnp.where(lane >= HEAD_DIM, x, zero)
            for hk, lo, hi in ((2 * grp, even_lo, even_hi), (2 * grp + 1, odd_lo, odd_hi)):
                lo_ref[hk, 0:WINDOW, :] = zero_block
                hi_ref[hk, 0:WINDOW, :] = zero_block
                lo_ref[hk, WINDOW:WINDOW + seq, :] = lo
                hi_ref[hk, WINDOW:WINDOW + seq, :] = hi

    @pl.when(pl.program_id(0) == 0)
    def _():
        key = lax.broadcasted_iota(jnp.int32, (2 * two_w, two_w), 0) % two_w
        qry = lax.broadcasted_iota(jnp.int32, (2 * two_w, two_w), 1) % WINDOW
        diff = qry + WINDOW - key
        in_window = (diff >= 0) & (diff < WINDOW)
        bias[0] = jnp.where(in_window & (key >= WINDOW), 0.0, NEG_INF)
        bias[1] = jnp.where(in_window, 0.0, NEG_INF)

    n_slabs = s_scr.shape[0]
    blocks_per_iter = n_slabs // n_kv_heads

    def blocks(i, carry):
        items = [(u * n_kv_heads + hk, hk, i * blocks_per_iter + u)
                 for u in range(blocks_per_iter) for hk in range(n_kv_heads)]
        for slab, hk, n in items:
            r0 = pl.multiple_of(n * WINDOW, WINDOW)
            c0 = hk * GROUP * HEAD_DIM
            q_pairs = jnp.concatenate([q_ref[pl.ds(r0, WINDOW), c0:c0 + LANES],
                                       q_ref[pl.ds(r0, WINDOW), c0 + LANES:c0 + 2 * LANES]], axis=0)
            keys = jnp.concatenate([k_lo[hk, pl.ds(r0, two_w), :], k_hi[hk, pl.ds(r0, two_w), :]], axis=0)
            s_scr[slab] = lax.dot_general(keys, q_pairs, (((1,), (1,)), ((), ())), preferred_element_type=F32)
        for slab, hk, n in items:
            t = jnp.minimum(n, 1)
            for half in range(2):
                for pair in range(2):
                    rows = slice(half * two_w, (half + 1) * two_w)
                    cols = slice(pair * WINDOW, (pair + 1) * WINDOW)
                    s_h = s_scr[slab, rows, cols] + bias[t, rows, cols]
                    sink = LOG2_E * sink_ref[hk * GROUP + 2 * pair + half]
                    mx = jnp.maximum(jnp.max(s_h, axis=0, keepdims=True), sink)
                    p = jnp.exp2(s_h - mx)
                    inv = 1.0 / (jnp.sum(p, axis=0, keepdims=True) + jnp.exp2(sink - mx))
                    inv_scr[2 * slab + half, :, cols] = jnp.broadcast_to(inv, (SUBLANES, WINDOW))
                    p_scr[slab, rows, cols] = p.astype(BF16)
        for slab, hk, n in items:
            r0 = pl.multiple_of(n * WINDOW, WINDOW)
            c0 = hk * GROUP * HEAD_DIM
            vals = jnp.concatenate([v_lo[hk, pl.ds(r0, two_w), :], v_hi[hk, pl.ds(r0, two_w), :]], axis=0)
            o_t = lax.dot_general(vals, p_scr[slab], (((0,), (0,)), ((), ())), preferred_element_type=F32)
            o_t = jnp.concatenate([o_t[0:HEAD_DIM, :] * inv_scr[2 * slab, 0:1, :],
                                   o_t[HEAD_DIM:LANES, :] * inv_scr[2 * slab + 1, 0:1, :]], axis=0)
            for pair in range(2):
                o_ref[pl.ds(r0, WINDOW), c0 + pair * LANES:c0 + (pair + 1) * LANES] = (
                    o_t[:, pair * WINDOW:(pair + 1) * WINDOW].T.astype(o_ref.dtype))
        return carry

    assert (seq // WINDOW) % blocks_per_iter == 0
    lax.fori_loop(0, seq // WINDOW // blocks_per_iter, blocks, 0)


def _attention(qkv, sinks, batch, seq, q_width, kv_width):
    m = qkv.shape[0]
    assert q_width % (2 * kv_width) == 0
    n_kv_heads = kv_width // HEAD_DIM
    n_slabs = ATTN_BLOCKS_PER_ITER * n_kv_heads
    return pl.pallas_call(
        functools.partial(_attention_kernel, n_kv_heads=n_kv_heads),
        grid=(batch,),
        in_specs=[pl.BlockSpec(memory_space=pltpu.SMEM),
                  pl.BlockSpec((seq, q_width), lambda b: (b, 0)),
                  pl.BlockSpec((seq, 2 * kv_width), lambda b: (b, q_width // (2 * kv_width)))],
        out_specs=pl.BlockSpec((seq, q_width), lambda b: (b, 0)),
        out_shape=pltpu.HBM((m, q_width), BF16),
        scratch_shapes=[pltpu.VMEM((n_kv_heads, WINDOW + seq, LANES), BF16)] * 4
        + [pltpu.VMEM((2, 4 * WINDOW, 2 * WINDOW), F32),
           pltpu.VMEM((n_slabs, 4 * WINDOW, 2 * WINDOW), F32),
           pltpu.VMEM((n_slabs, 4 * WINDOW, 2 * WINDOW), BF16),
           pltpu.VMEM((2 * n_slabs, SUBLANES, 2 * WINDOW), F32)],
        compiler_params=_compiler_params(
            1,
            2 * _nbytes((seq, q_width), BF16) + _nbytes((seq, 2 * kv_width), BF16),
            4 * _nbytes((n_kv_heads, WINDOW + seq, LANES), BF16)
            + (8 + 2 * n_slabs) * _nbytes((4 * WINDOW, 2 * WINDOW), F32)
            + 4 * _nbytes((seq, LANES), F32)),
        name="swa_attention",
    )(sinks, qkv, qkv)


MERGE_ROW_CHUNK = 128


def _merge_kernel(h_ref, yc_ref, at_ref, wga_ref, wgb_ref, woc_ref, woa_ref, wnext_ref,
                  o_ref, wnext_bf_ref, woc_bf, woa_bf):
    @pl.when(pl.program_id(1) == 0)
    def _():
        woc_bf[...] = woc_ref[...].astype(BF16)
        woa_bf[...] = woa_ref[...].astype(BF16)

    wnext_bf_ref[...] = wnext_ref[...].astype(BF16)
    for rows in _row_chunks(h_ref.shape[0], MERGE_ROW_CHUNK):
        h = h_ref[rows, :]
        ya = _dot(yc_ref[rows, :], woc_bf[...])
        yb = _dot(at_ref[rows, :], woa_bf[...])
        ga = _dot(h, wga_ref[...])
        gb = _dot(h, wgb_ref[...])
        o_ref[rows, :] = (jax.nn.sigmoid(ga) * ya + jax.nn.sigmoid(gb) * yb).astype(o_ref.dtype)


def _merge(h, yconv, attn, w_gates_bf, w_out_conv, w_out_attn, w_next, bm=2048, bn=512):
    m, d = h.shape
    kc, ka = yconv.shape[1], attn.shape[1]
    nj, ni = d // bn, m // bm
    side_in, side_out, side_shape, side_bytes = _side_cast_specs(w_next, nj * ni, lambda j, i: j * ni + i)
    return pl.pallas_call(
        _merge_kernel,
        grid=(nj, ni),
        in_specs=[pl.BlockSpec((bm, d), lambda j, i: (i, 0)),
                  pl.BlockSpec((bm, kc), lambda j, i: (i, 0)),
                  pl.BlockSpec((bm, ka), lambda j, i: (i, 0)),
                  pl.BlockSpec((d, bn), lambda j, i: (0, j)),
                  pl.BlockSpec((d, bn), lambda j, i: (0, nj + j)),
                  pl.BlockSpec((kc, bn), lambda j, i: (0, j)),
                  pl.BlockSpec((ka, bn), lambda j, i: (0, j)),
                  side_in],
        out_specs=[pl.BlockSpec((bm, bn), lambda j, i: (i, j)), side_out],
        out_shape=[pltpu.HBM((m, d), BF16), side_shape],
        scratch_shapes=[pltpu.VMEM((kc, bn), BF16), pltpu.VMEM((ka, bn), BF16)],
        compiler_params=_compiler_params(
            2,
            _nbytes((bm, d + kc + ka), BF16) + _nbytes((2 * d, bn), BF16) + _nbytes((kc + ka, bn), F32)
            + _nbytes((bm, bn), BF16) + side_bytes,
            _nbytes((kc + ka, bn), BF16) + 10 * _nbytes((MERGE_ROW_CHUNK, bn), F32)),
        name="gated_merge",
    )(h, yconv, attn, w_gates_bf, w_gates_bf, w_out_conv, w_out_attn, w_next)


def _rope_tables(seq):
    half = ROT_DIM // 2
    inv_freq = (1.0 / (np.float32(ROPE_THETA) ** (np.arange(0, ROT_DIM, 2, dtype=np.float32) / ROT_DIM))
                ).astype(np.float32)
    ang = (np.arange(seq, dtype=np.float32)[:, None] * inv_freq[None, :]).astype(np.float64)
    cos, sin = np.cos(ang), np.sin(ang)
    ones = np.ones((seq, HEAD_DIM - ROT_DIM))
    zeros, zeros_tail = np.zeros((seq, half)), np.zeros((seq, HEAD_DIM - ROT_DIM))
    per_head = [np.concatenate([cos, cos, ones], axis=1),
                np.concatenate([-sin, zeros, zeros_tail], axis=1),
                np.concatenate([zeros, sin, zeros_tail], axis=1)]
    table = np.concatenate([np.tile(t, (1, LANES // HEAD_DIM)) for t in per_head], axis=1)
    return jnp.asarray(table.astype(np.float32))


def _swiglu_half_step(x, h, w_gu, w_down, next_gain):
    act, w_down_bf = _gate_up(h, w_gu, w_down)
    return _matmul_residual(act, w_down_bf, x, 0.5, next_gain)


def kernel(x, g_ffn1, w_gu1, w_down1, g_mix, w_in, conv_w, q_norm_g, k_norm_g, sinks,
           w_out_conv, w_out_attn, w_o, g_ffn2, w_gu2, w_down2):
    batch, seq, d = x.shape
    depth = w_in.shape[0]
    conv_width = conv_w.shape[2]
    q_width = w_out_attn.shape[1]
    kv_width = q_width // GROUP
    q_col = 3 * conv_width
    gate_col = q_col + q_width + 2 * kv_width
    rope = _rope_tables(seq)

    xf = x.reshape(batch * seq, d)
    h = _rmsnorm(xf, g_ffn1[0])
    for l in range(depth):
        xf, h = _swiglu_half_step(xf, h, w_gu1[l], w_down1[l], g_mix[l])

        yconv, w_gates_bf = _conv_branch(h, w_in[l], conv_w[l], batch, seq, conv_width, gate_col, 2 * d)
        qkv_bn = 2 * kv_width
        assert q_width % qkv_bn == 0
        gains = jnp.concatenate(
            [jnp.tile(q_norm_g[l] * (ATTN_SCALE * LOG2_E), q_width // HEAD_DIM),
             jnp.tile(k_norm_g[l], kv_width // HEAD_DIM), jnp.ones((kv_width,), F32)]
        ).reshape(-1, 1, qkv_bn)
        raw_cols = jnp.concatenate(
            [jnp.zeros((q_width + kv_width,), F32), jnp.ones((kv_width,), F32)]
        ).reshape(-1, 1, qkv_bn)
        qkv = _qkv_proj(h, w_in[l], gains, raw_cols, rope, batch, seq, q_col, bn=qkv_bn)
        attn = _attention(qkv, sinks[l], batch, seq, q_width, kv_width)
        merged, w_o_bf = _merge(h, yconv, attn, w_gates_bf, w_out_conv[l], w_out_attn[l], w_o[l])
        xf, h = _matmul_residual(merged, w_o_bf, xf, 1.0, g_ffn2[l])

        next_gain = g_ffn1[l + 1] if l + 1 < depth else None
        xf, h = _swiglu_half_step(xf, h, w_gu2[l], w_down2[l], next_gain)
    return xf.reshape(batch, seq, d)
```
